```python
import math
import jax
import jax.numpy as jnp
from jax import lax
import numpy as np

D_MODEL = 1024
BATCH = 8
SEQ = 2048
DEPTH = 1
DEC_BATCH = 128
DEC_SEQ = 8
PAST_LEN = 2048
PAGE_SIZE = 128

HEAD_DIM = 64
A_HEADS = 8
B_HEADS = 8
IDX_HEADS = 8
IDX_DIM = 64
DSA_TOPK_MAX = 256
N_BUCKETS = 32
MAX_DISTANCE = 128
N_GROUPS = 4
EXPERTS_PER_GROUP = 4
N_EXPERTS = N_GROUPS * EXPERTS_PER_GROUP
EXPERT_HIDDEN = 512
Q_BLOCK = 128
SPARSE_Q_BLOCK = 32
LN_EPS = 1e-5
ALPHA = (2 * DEPTH) ** 0.25
BETA = (8 * DEPTH) ** -0.25
A_QK_WIDTH = A_HEADS * 2 * HEAD_DIM
A_V_WIDTH = A_HEADS * 2 * HEAD_DIM
B_WIDTH = B_HEADS * HEAD_DIM
IN_SPLIT_SIZES = (A_QK_WIDTH, A_QK_WIDTH, A_V_WIDTH, B_WIDTH, B_WIDTH, B_WIDTH,
                  IDX_HEADS * IDX_DIM, IDX_DIM, IDX_HEADS, D_MODEL, D_MODEL)
IN_WIDTH = sum(IN_SPLIT_SIZES)

kernel_name = "hybrid_diff_dsa_hmoe_decode_step"


def _layer_norm(x, g, b):
    xf = x.astype(jnp.float32)
    mu = jnp.mean(xf, axis=-1, keepdims=True)
    var = jnp.mean(jnp.square(xf - mu), axis=-1, keepdims=True)
    y = (xf - mu) * lax.rsqrt(var + LN_EPS) * g.astype(jnp.float32) + b.astype(jnp.float32)
    return y.astype(x.dtype)


def _rel_bucket(dist):
    n = jnp.maximum(dist, 0)
    max_exact = N_BUCKETS // 2
    nf = jnp.maximum(n, 1).astype(jnp.float32)
    large = max_exact + (jnp.log(nf / max_exact) / math.log(MAX_DISTANCE / max_exact)
                         * (N_BUCKETS - max_exact)).astype(jnp.int32)
    large = jnp.minimum(large, N_BUCKETS - 1)
    return jnp.where(n < max_exact, n, large)


def _query_blocks(fn, qs, q_pos, block):
    b, t = qs[0].shape[:2]
    nb = t // block
    split = lambda a: jnp.moveaxis(a.reshape((b, nb, block) + a.shape[2:]), 1, 0)
    out = lax.map(lambda args: fn(*args[0], args[1]),
                  (tuple(split(a) for a in qs), q_pos.reshape(nb, block)))
    return jnp.moveaxis(out, 0, 1).reshape((b, t) + out.shape[3:])


def _diff_block(q, q_pos, k, v, k_pos, bias_tab, lam, lam_init, subln_g):
    s = jnp.einsum('bqhcd,bkhcd->bhcqk', q, k).astype(jnp.float32) * HEAD_DIM ** -0.5
    bias = bias_tab[_rel_bucket(q_pos[:, None] - k_pos[None, :])].astype(jnp.float32)
    s = s + jnp.transpose(bias, (2, 0, 1))[None, :, None]
    s = jnp.where(k_pos[None, :] <= q_pos[:, None], s, -jnp.inf)
    p = jax.nn.softmax(s, axis=-1)
    a = p[:, :, 0] - lam * p[:, :, 1]
    o = jnp.einsum('bhqk,bkhe->bqhe', a.astype(v.dtype), v).astype(jnp.float32)
    o = o * lax.rsqrt(jnp.mean(o * o, axis=-1, keepdims=True) + LN_EPS)
    o = o * subln_g.astype(jnp.float32) * (1.0 - lam_init)
    return o.astype(q.dtype)


def _dsa_block(q, qi, wi, q_pos, k, v, ki, k_pos, bias_tab, topk):
    rel = jax.nn.relu(jnp.einsum('bqhd,bkd->bqhk', qi, ki).astype(jnp.float32) * IDX_DIM ** -0.5)
    score = jnp.einsum('bqh,bqhk->bqk', wi.astype(jnp.float32) * IDX_HEADS ** -0.5, rel)
    score = jnp.where(k_pos[None, None, :] <= q_pos[None, :, None], score, -jnp.inf)
    top_s, top_i = lax.top_k(score, topk)
    valid = jnp.isfinite(top_s)
    take = jax.vmap(lambda arr, idx: arr[idx])
    ks = take(k, top_i)
    vs = take(v, top_i)
    s = jnp.einsum('bqhd,bqkhd->bhqk', q, ks).astype(jnp.float32) * HEAD_DIM ** -0.5
    bias = bias_tab[_rel_bucket(q_pos[None, :, None] - k_pos[top_i])]
    s = s + jnp.transpose(bias, (0, 3, 1, 2)).astype(jnp.float32)
    s = jnp.where(valid[:, None], s, -jnp.inf)
    p = jax.nn.softmax(s, axis=-1)
    return jnp.einsum('bhqk,bqkhd->bqhd', p.astype(v.dtype), vs)


def _unpack_rows(ra, rb, ri):
    b, t = ra.shape[:2]
    ka = ra[:, :, 0].reshape(b, t, A_HEADS, 2, HEAD_DIM)
    return ka, ra[:, :, 1], rb[:, :, 0], rb[:, :, 1], ri


def _hmoe(h, w_rg, b_rg, w_re, b_re, w_up, w_down):
    b, t, _ = h.shape
    lg = (h @ w_rg + b_rg).astype(jnp.float32)
    pg = jax.nn.softmax(lg, axis=-1)
    g_sel = jnp.argmax(lg, axis=-1)
    p_g = jnp.take_along_axis(pg, g_sel[..., None], axis=-1)
    le = (h @ w_re + b_re).astype(jnp.float32).reshape(b, t, N_GROUPS, EXPERTS_PER_GROUP)
    le = jnp.take_along_axis(le, g_sel[..., None, None], axis=2)[..., 0, :]
    pe = jax.nn.softmax(le, axis=-1)
    top_p, top_i = lax.top_k(pe, 2)
    top_p = top_p / jnp.sum(top_p, axis=-1, keepdims=True)
    eid = g_sel[..., None] * EXPERTS_PER_GROUP + top_i
    combine = jnp.sum(jax.nn.one_hot(eid, N_EXPERTS, dtype=jnp.float32)
                      * (p_g * top_p)[..., None], axis=-2).astype(h.dtype)
    out = jnp.zeros_like(h)
    for e in range(N_EXPERTS):
        gate, up = jnp.split(h @ w_up[e], 2, axis=-1)
        out = out + combine[..., e:e + 1] * ((jax.nn.silu(gate) * up) @ w_down[e])
    return out


def _layer(x, c, attend, w_ada, b_ada, w_in, w_ba, w_bb, w_out, ln1_g, ln1_b,
           w_rg, b_rg, w_re, b_re, w_up, w_down, ln2_g, ln2_b):
    b, t, _ = x.shape
    ada = (jax.nn.silu(c) @ w_ada + b_ada)[:, None, :]
    sh1, sc1, g1, sh2, sc2, g2 = jnp.split(ada, 6, axis=-1)
    h = x * (1 + sc1) + sh1
    offsets = [int(o) for o in np.cumsum(IN_SPLIT_SIZES)[:-1]]
    (qa, ka, va, qb, kb, vb, qi, ki, wi, ga, gb) = jnp.split(h @ w_in, offsets, axis=-1)
    qa = qa.reshape(b, t, A_HEADS, 2, HEAD_DIM)
    ra = jnp.stack([ka.reshape(b, t, A_HEADS, 2 * HEAD_DIM),
                    va.reshape(b, t, A_HEADS, 2 * HEAD_DIM)], axis=2)
    qb = qb.reshape(b, t, B_HEADS, HEAD_DIM)
    rb = jnp.stack([kb.reshape(b, t, B_HEADS, HEAD_DIM),
                    vb.reshape(b, t, B_HEADS, HEAD_DIM)], axis=2)
    qi = qi.reshape(b, t, IDX_HEADS, IDX_DIM)
    oa, ob = attend(qa, qb, qi, wi, ra, rb, ki)
    ya = oa.reshape(b, t, A_V_WIDTH) @ w_ba
    yb = ob.reshape(b, t, B_WIDTH) @ w_bb
    mix = (jax.nn.sigmoid(ga) * ya + jax.nn.sigmoid(gb) * yb) @ w_out
    x = _layer_norm(ALPHA * x + g1 * mix, ln1_g, ln1_b)
    h2 = x * (1 + sc2) + sh2
    x = _layer_norm(ALPHA * x + g2 * _hmoe(h2, w_rg, b_rg, w_re, b_re, w_up, w_down), ln2_g, ln2_b)
    return x, ra, rb, ki


def setup_inputs(seed: int = 0) -> dict:
    key = jax.random.key(seed)
    ks = jax.random.split(key, 40)
    f32 = jnp.float32
    nrm = lambda k, shape, scale: jax.random.normal(k, shape, f32) * scale
    n_pages = PAST_LEN // PAGE_SIZE
    n_used = DEC_BATCH * n_pages
    n_pool = n_used + max(1, n_used // 4)
    page_table = jax.random.permutation(ks[0], n_pool)[:n_used].reshape(DEC_BATCH, n_pages).astype(jnp.int32)
    return {
        "x_prompt": nrm(ks[1], (BATCH, SEQ, D_MODEL), 1.0),
        "x_sample": nrm(ks[2], (DEC_BATCH, DEC_SEQ, D_MODEL), 1.0),
        "cache_kv_diff": nrm(ks[3], (DEPTH, n_pool, PAGE_SIZE, 2, A_HEADS, 2 * HEAD_DIM), 1.0),
        "cache_kv_dsa": nrm(ks[4], (DEPTH, n_pool, PAGE_SIZE, 2, B_HEADS, HEAD_DIM), 1.0),
        "cache_kidx": nrm(ks[5], (DEPTH, n_pool, PAGE_SIZE, IDX_DIM), 1.0),
        "page_table": page_table,
        "c_prompt": nrm(ks[6], (BATCH, D_MODEL), 1.0),
        "c_sample": nrm(ks[7], (DEC_BATCH, D_MODEL), 1.0),
        "rel_bias": nrm(ks[8], (N_BUCKETS, A_HEADS + B_HEADS), 0.5),
        "w_ada": nrm(ks[9], (DEPTH, D_MODEL, 6 * D_MODEL), 0.5 * D_MODEL ** -0.5),
        "b_ada": nrm(ks[10], (DEPTH, 6 * D_MODEL), 0.02),
        "w_in": nrm(ks[11], (DEPTH, D_MODEL, IN_WIDTH), D_MODEL ** -0.5),
        "lambda_q1": nrm(ks[12], (DEPTH, HEAD_DIM), 0.1),
        "lambda_k1": nrm(ks[13], (DEPTH, HEAD_DIM), 0.1),
        "lambda_q2": nrm(ks[14], (DEPTH, HEAD_DIM), 0.1),
        "lambda_k2": nrm(ks[15], (DEPTH, HEAD_DIM), 0.1),
        "subln_g": 1.0 + nrm(ks[16], (DEPTH, 2 * HEAD_DIM), 0.02),
        "w_branch_a": nrm(ks[17], (DEPTH, A_V_WIDTH, D_MODEL), A_V_WIDTH ** -0.5),
        "w_branch_b": nrm(ks[18], (DEPTH, B_WIDTH, D_MODEL), B_WIDTH ** -0.5),
        "w_out": nrm(ks[19], (DEPTH, D_MODEL, D_MODEL), BETA * D_MODEL ** -0.5),
        "ln1_g": 1.0 + nrm(ks[20], (DEPTH, D_MODEL), 0.02),
        "ln1_b": nrm(ks[21], (DEPTH, D_MODEL), 0.02),
        "w_router_group": nrm(ks[22], (DEPTH, D_MODEL, N_GROUPS), D_MODEL ** -0.5),
        "b_router_group": nrm(ks[23], (DEPTH, N_GROUPS), 0.01),
        "w_router_expert": nrm(ks[24], (DEPTH, D_MODEL, N_EXPERTS), D_MODEL ** -0.5),
        "b_router_expert": nrm(ks[25], (DEPTH, N_EXPERTS), 0.01),
        "w_up": nrm(ks[26], (DEPTH, N_EXPERTS, D_MODEL, 2 * EXPERT_HIDDEN), D_MODEL ** -0.5),
        "w_down": nrm(ks[27], (DEPTH, N_EXPERTS, EXPERT_HIDDEN, D_MODEL), BETA * EXPERT_HIDDEN ** -0.5),
        "ln2_g": 1.0 + nrm(ks[28], (DEPTH, D_MODEL), 0.02),
        "ln2_b": nrm(ks[29], (DEPTH, D_MODEL), 0.02),
    }


def reference(x_prompt, x_sample, cache_kv_diff, cache_kv_dsa, cache_kidx, page_table,
              c_prompt, c_sample, rel_bias, w_ada, b_ada, w_in, lambda_q1, lambda_k1,
              lambda_q2, lambda_k2, subln_g, w_branch_a, w_branch_b, w_out, ln1_g, ln1_b,
              w_router_group, b_router_group, w_router_expert, b_router_expert,
              w_up, w_down, ln2_g, ln2_b):
    topk_prompt = min(DSA_TOPK_MAX, SEQ // 4)
    topk_sample = min(DSA_TOPK_MAX, (PAST_LEN + DEC_SEQ) // 4)
    pos_p = jnp.arange(SEQ, dtype=jnp.int32)
    q_pos_s = PAST_LEN + jnp.arange(DEC_SEQ, dtype=jnp.int32)
    k_pos_s = jnp.arange(PAST_LEN + DEC_SEQ, dtype=jnp.int32)
    bias_a = rel_bias[:, :A_HEADS]
    bias_b = rel_bias[:, A_HEADS:]
    xp, xs = x_prompt, x_sample
    rows_p_a, rows_p_b, rows_p_i = [], [], []
    rows_s_a, rows_s_b, rows_s_i = [], [], []
    for l in range(DEPTH):
        lam_init = 0.8 - 0.6 * math.exp(-0.3 * l)
        lam = (jnp.exp(jnp.sum(lambda_q1[l].astype(jnp.float32) * lambda_k1[l].astype(jnp.float32)))
               - jnp.exp(jnp.sum(lambda_q2[l].astype(jnp.float32) * lambda_k2[l].astype(jnp.float32)))
               + lam_init)
        g_sub = subln_g[l]

        def attend_prompt(qa, qb, qi, wi, ra, rb, ri):
            ka, va, kb, vb, ki = _unpack_rows(ra, rb, ri)
            oa = _query_blocks(
                lambda q, qp: _diff_block(q, qp, ka, va, pos_p, bias_a, lam, lam_init, g_sub),
                (qa,), pos_p, Q_BLOCK)
            ob = _query_blocks(
                lambda q, qi_, wi_, qp: _dsa_block(q, qi_, wi_, qp, kb, vb, ki, pos_p, bias_b, topk_prompt),
                (qb, qi, wi), pos_p, SPARSE_Q_BLOCK)
            return oa, ob

        def attend_sample(qa, qb, qi, wi, ra, rb, ri):
            def one_seq(args):
                qa1, qb1, qi1, wi1, ra1, rb1, ri1, pt = args
                past_a = cache_kv_diff[l, pt].reshape((PAST_LEN,) + ra1.shape[1:])
                past_b = cache_kv_dsa[l, pt].reshape((PAST_LEN,) + rb1.shape[1:])
                past_i = cache_kidx[l, pt].reshape((PAST_LEN,) + ri1.shape[1:])
                full_a = jnp.concatenate([past_a, ra1.astype(past_a.dtype)], axis=0)[None]
                full_b = jnp.concatenate([past_b, rb1.astype(past_b.dtype)], axis=0)[None]
                full_i = jnp.concatenate([past_i, ri1.astype(past_i.dtype)], axis=0)[None]
                ka, va, kb, vb, ki = _unpack_rows(full_a, full_b, full_i)
                oa = _diff_block(qa1[None], q_pos_s, ka, va, k_pos_s, bias_a, lam, lam_init, g_sub)
                ob = _dsa_block(qb1[None], qi1[None], wi1[None], q_pos_s, kb, vb, ki, k_pos_s,
                                bias_b, topk_sample)
                return oa[0].astype(qa1.dtype), ob[0].astype(qb1.dtype)
            return lax.map(one_seq, (qa, qb, qi, wi, ra, rb, ri, page_table))

        weights = (w_ada[l], b_ada[l], w_in[l], w_branch_a[l], w_branch_b[l], w_out[l],
                   ln1_g[l], ln1_b[l], w_router_group[l], b_router_group[l],
                   w_router_expert[l], b_router_expert[l], w_up[l], w_down[l], ln2_g[l], ln2_b[l])
        xp, ra_p, rb_p, ri_p = _layer(xp, c_prompt, attend_prompt, *weights)
        xs, ra_s, rb_s, ri_s = _layer(xs, c_sample, attend_sample, *weights)
        rows_p_a.append(ra_p)
        rows_p_b.append(rb_p)
        rows_p_i.append(ri_p)
        rows_s_a.append(ra_s)
        rows_s_b.append(rb_s)
        rows_s_i.append(ri_s)
    return (xp, xs,
            jnp.stack(rows_p_a, 0), jnp.stack(rows_p_b, 0), jnp.stack(rows_p_i, 0),
            jnp.stack(rows_s_a, 0), jnp.stack(rows_s_b, 0), jnp.stack(rows_s_i, 0))
```

```python
import functools
import math

import jax
import jax.numpy as jnp
from jax import lax
from jax.experimental import pallas as pl
from jax.experimental.pallas import tpu as pltpu

D_MODEL = 1024
HEAD_DIM = 64
A_HEADS = 8
B_HEADS = 8
IDX_HEADS = 8
IDX_DIM = 64
DSA_TOPK_MAX = 256
N_BUCKETS = 32
MAX_DISTANCE = 128
N_GROUPS = 4
EXPERTS_PER_GROUP = 4
N_EXPERTS = N_GROUPS * EXPERTS_PER_GROUP
EXPERT_HIDDEN = 512
LN_EPS = 1e-5

LANES = 128
TB = 256
ROW_TILE = 256
MOE_TILE = 512
PAGE_ROWS = 128
PAGES_PER_STEP = 4
VMEM_LIMIT = 56 * 1024 * 1024

A_WIDTH = A_HEADS * 2 * HEAD_DIM
B_WIDTH = B_HEADS * HEAD_DIM
I_WIDTH = IDX_HEADS * IDX_DIM
C_QA, C_KVA, C_QB, C_KVB, C_QI, C_KI, C_WI, C_G = 0, 1024, 3072, 3584, 4608, 5120, 5248, 5376
W2_WIDTH = C_G + 2 * D_MODEL

F32 = jnp.float32
BF16 = jnp.bfloat16
NEG_INF = float("-inf")
INT_MIN = -2 ** 31


def _cparams(sem):
    return pltpu.CompilerParams(dimension_semantics=sem, vmem_limit_bytes=VMEM_LIMIT)


def _dot(a, b):
    return jnp.dot(a, b, preferred_element_type=F32)


def _dot_nt(a, b):
    return lax.dot_general(a, b, (((1,), (1,)), ((), ())), preferred_element_type=F32)


def _split(a):
    hi = a.astype(BF16)
    lo = (a - hi.astype(F32)).astype(BF16)
    return hi, lo


def _dot3(a, b):
    a_hi, a_lo = _split(a)
    b_hi, b_lo = _split(b)
    return _dot(a_hi, b_hi) + _dot(a_lo, b_hi) + _dot(a_hi, b_lo)


def _sigmoid(x):
    return 1.0 / (1.0 + jnp.exp(-x))


def _layer_norm(u, g, b):
    mu = jnp.mean(u, axis=-1, keepdims=True)
    d = u - mu
    var = jnp.mean(d * d, axis=-1, keepdims=True)
    return d * lax.rsqrt(var + LN_EPS) * g + b


def _bias_kernel(tab_ref, dist_ref, out_ref):
    h = pl.program_id(0)
    d = dist_ref[...]
    n = jnp.maximum(d, 0)
    max_exact = N_BUCKETS // 2
    nf = jnp.maximum(n, 1).astype(F32)
    large = max_exact + (jnp.log(nf / max_exact) / math.log(MAX_DISTANCE / max_exact)
                         * (N_BUCKETS - max_exact)).astype(jnp.int32)
    large = jnp.minimum(large, N_BUCKETS - 1)
    bucket = jnp.where(n < max_exact, n, large)
    last = tab_ref[N_BUCKETS - 1, h]
    acc = jnp.zeros(d.shape, F32)
    for m in range(N_BUCKETS - 1):
        acc = jnp.where(bucket == m, tab_ref[m, h] - last, acc)
    out_ref[0] = jnp.where(d < 0, NEG_INF, acc)


def _bias_tiles(rel_bias, dist):
    n_heads = rel_bias.shape[1]
    r, c = dist.shape
    return pl.pallas_call(
        _bias_kernel,
        grid=(n_heads,),
        in_specs=[pl.BlockSpec(memory_space=pltpu.SMEM),
                  pl.BlockSpec((r, c), lambda h: (0, 0))],
        out_specs=pl.BlockSpec((1, r, c), lambda h: (h, 0, 0)),
        out_shape=jax.ShapeDtypeStruct((n_heads, r, c), F32),
        compiler_params=_cparams(("arbitrary",)),
        name="bias_tiles",
    )(rel_bias, dist)


def _ada_kernel(c_ref, w_ref, b_ref, o_ref):
    c = c_ref[...]
    o_ref[...] = _dot3(c * _sigmoid(c), w_ref[...]) + b_ref[...]


def _ada(c_all, w_ada, b_ada):
    r, d = c_all.shape
    n = w_ada.shape[1]
    tn = 512
    return pl.pallas_call(
        _ada_kernel,
        grid=(n // tn,),
        in_specs=[pl.BlockSpec((r, d), lambda j: (0, 0)),
                  pl.BlockSpec((d, tn), lambda j: (0, j)),
                  pl.BlockSpec((1, tn), lambda j: (0, j))],
        out_specs=pl.BlockSpec((r, tn), lambda j: (0, j)),
        out_shape=jax.ShapeDtypeStruct((r, n), F32),
        compiler_params=_cparams(("arbitrary",)),
        name="ada",
    )(c_all, w_ada, b_ada.reshape(1, n))


def _inproj_kernel(x_ref, sh_ref, sc_ref, w_ref, qa_ref, ra_ref, kva_ref, qb_ref, rb_ref, kvb_ref,
                   qi_ref, ki_ref, kk_ref, wi_ref, sg_ref):
    h = (x_ref[...] * (1.0 + sc_ref[0]) + sh_ref[0]).astype(BF16)
    q_scale = HEAD_DIM ** -0.5

    def mm(c0, n):
        return _dot(h, w_ref[:, c0:c0 + n])

    for c in range(0, A_WIDTH, 512):
        qa_ref[:, c:c + 512] = (mm(C_QA + c, 512) * q_scale).astype(BF16)
    for c in range(0, 2 * A_WIDTH, 512):
        a = mm(C_KVA + c, 512)
        ra_ref[:, c:c + 512] = a
        kva_ref[:, c:c + 512] = a.astype(BF16)
    qb_ref[...] = (mm(C_QB, 512) * q_scale).astype(BF16)
    for c in range(0, 2 * B_WIDTH, 512):
        a = mm(C_KVB + c, 512)
        rb_ref[:, c:c + 512] = a
        kvb_ref[:, c:c + 512] = a.astype(BF16)
    qi_ref[...] = (mm(C_QI, 512) * q_scale).astype(BF16)
    a = mm(C_KI, 2 * LANES)
    ki_ref[...] = a[:, :IDX_DIM]
    kk_ref[...] = a[:, :LANES].astype(BF16)
    wi_ref[...] = a[:, LANES:] * IDX_HEADS ** -0.5
    for c in range(0, 2 * D_MODEL, 512):
        sg_ref[:, c:c + 512] = _sigmoid(mm(C_G + c, 512))


def _row_vec_spec(arr, col, tm, rows_per_batch):
    if arr.shape[1] == 1:
        per = rows_per_batch // tm
        return pl.BlockSpec((1, 1, D_MODEL), lambda i, *_: (i // per, 0, col))
    return pl.BlockSpec((1, tm, D_MODEL), lambda i, *_: (i, 0, col))


def _in_proj(x, ada3, w2, rows_per_batch):
    n = x.shape[0]
    tm = ROW_TILE
    row = lambda w: pl.BlockSpec((tm, w), lambda i: (i, 0))
    outs = [("qa", A_WIDTH, BF16), ("ra", 2 * A_WIDTH, F32), ("kva", 2 * A_WIDTH, BF16),
            ("qb", B_WIDTH, BF16), ("rb", 2 * B_WIDTH, F32), ("kvb", 2 * B_WIDTH, BF16),
            ("qi", I_WIDTH, BF16), ("ki", IDX_DIM, F32), ("kk", LANES, BF16), ("wi", LANES, F32),
            ("sg", 2 * D_MODEL, F32)]
    res = pl.pallas_call(
        _inproj_kernel,
        grid=(n // tm,),
        in_specs=[row(D_MODEL),
                  _row_vec_spec(ada3, 0, tm, rows_per_batch),
                  _row_vec_spec(ada3, 1, tm, rows_per_batch),
                  pl.BlockSpec((D_MODEL, W2_WIDTH), lambda i: (0, 0), pipeline_mode=pl.Buffered(1))],
        out_specs=[row(w) for _, w, _ in outs],
        out_shape=[jax.ShapeDtypeStruct((n, w), dt) for _, w, dt in outs],
        compiler_params=_cparams(("arbitrary",)),
        name="in_proj",
    )(x, ada3, ada3, w2)
    return {name: r for (name, _, _), r in zip(outs, res)}


def _lambda_value(lam_ref, lam_init):
    a = jnp.sum(lam_ref[0:1, :] * lam_ref[1:2, :], axis=-1, keepdims=True)
    b = jnp.sum(lam_ref[2:3, :] * lam_ref[3:4, :], axis=-1, keepdims=True)
    return jnp.exp(a) - jnp.exp(b) + lam_init


def _sub_layer_norm(o, g, lam_init):
    o = o * lax.rsqrt(jnp.mean(o * o, axis=-1, keepdims=True) + LN_EPS)
    return o * g * (1.0 - lam_init)


def _diff_attn_kernel(lam_ref, g_ref, q_ref, k_ref, v_ref, bias_ref, o_ref, *, lam_init):
    qi = pl.program_id(2)
    q = q_ref[...]
    lo = lax.broadcasted_iota(jnp.int32, q.shape, 1) < HEAD_DIM
    zero = jnp.zeros_like(q)
    q1 = jnp.where(lo, q, zero)
    q2 = jnp.where(lo, zero, q)
    tq = q.shape[0]

    def update(s, m, l, acc, v):
        mn = jnp.maximum(m, jnp.max(s, axis=-1, keepdims=True))
        p = jnp.exp(s - mn)
        al = jnp.exp(m - mn)
        return mn, al * l + jnp.sum(p, axis=-1, keepdims=True), al * acc + _dot(p.astype(BF16), v)

    def body(j, carry):
        m0, l0, a0, m1, l1, a1 = carry
        off = pl.multiple_of(j * TB, TB)
        k = k_ref[pl.ds(off, TB), :]
        v = v_ref[pl.ds(off, TB), :]
        b = bias_ref[0, jnp.minimum(qi - j, 2)]
        m0, l0, a0 = update(_dot_nt(q1, k) + b, m0, l0, a0, v)
        m1, l1, a1 = update(_dot_nt(q2, k) + b, m1, l1, a1, v)
        return m0, l0, a0, m1, l1, a1

    col = lambda val: jnp.full((tq, 1), val, F32)
    acc0 = jnp.zeros((tq, 2 * HEAD_DIM), F32)
    _, l0, a0, _, l1, a1 = lax.fori_loop(
        0, qi + 1, body, (col(NEG_INF), col(0.0), acc0, col(NEG_INF), col(0.0), acc0))
    lam = _lambda_value(lam_ref, lam_init)
    o = a0 / l0 - lam * (a1 / l1)
    o_ref[...] = _sub_layer_norm(o, g_ref[...], lam_init).astype(BF16)


def _diff_attn_prompt(lam4, g_sub, qa, kva, bias_p, b, t, lam_init):
    nq = t // TB
    return pl.pallas_call(
        functools.partial(_diff_attn_kernel, lam_init=lam_init),
        grid=(b, A_HEADS, nq),
        in_specs=[pl.BlockSpec((4, HEAD_DIM), lambda bi, h, i: (0, 0)),
                  pl.BlockSpec((1, 2 * HEAD_DIM), lambda bi, h, i: (0, 0)),
                  pl.BlockSpec((TB, LANES), lambda bi, h, i: (bi * nq + i, h)),
                  pl.BlockSpec((t, LANES), lambda bi, h, i: (bi, h)),
                  pl.BlockSpec((t, LANES), lambda bi, h, i: (bi, A_HEADS + h)),
                  pl.BlockSpec((1, 3, TB, TB), lambda bi, h, i: (h, 0, 0, 0))],
        out_specs=pl.BlockSpec((TB, LANES), lambda bi, h, i: (bi * nq + i, h)),
        out_shape=jax.ShapeDtypeStruct((b * t, A_WIDTH), BF16),
        compiler_params=_cparams(("arbitrary", "arbitrary", "arbitrary")),
        name="diff_attn_prompt",
    )(lam4, g_sub, qa, kva, kva, bias_p)


KEY_NEG_INF = INT_MIN + 0x7FFFFF


def _key_to_float(key):
    bits = jnp.where(key < 0, key ^ jnp.int32(0x7FFFFFFF), key)
    return jnp.where(key < KEY_NEG_INF, NEG_INF, lax.bitcast_convert_type(bits, F32))


def _topk_select(sc_ref, nt, topk, tri_ref):
    rows, tw = sc_ref.shape[1], sc_ref.shape[2]
    kf = float(topk)

    def count_ge(cf):
        def body(j, acc):
            g = jnp.where(sc_ref[j] >= cf, 1.0, 0.0)
            r = g[:, :LANES]
            for c in range(1, tw // LANES):
                r = r + g[:, c * LANES:(c + 1) * LANES]
            return acc + r
        acc = lax.fori_loop(0, nt, body, jnp.zeros((rows, LANES), F32))
        return jnp.sum(acc, axis=-1, keepdims=True)

    def search(p, t):
        cand = t + lax.shift_left(jnp.int32(1), (31 - p).astype(jnp.int32))
        c = count_ge(_key_to_float(cand))
        return jnp.where(c >= kf, cand, t)

    t = lax.fori_loop(0, 32, search, jnp.full((rows, 1), INT_MIN, jnp.int32))
    t_lo = _key_to_float(t)
    t_hi = _key_to_float(t + 1)
    need = kf - count_ge(t_hi)
    tri = tri_ref[...]

    def finish(j, c):
        s = sc_ref[j]
        gt = s >= t_hi
        eq = jnp.logical_and(s >= t_lo, jnp.logical_not(gt))
        e = jnp.where(eq, 1.0, 0.0)
        rank = _dot(e.astype(BF16), tri) + c
        sel = jnp.logical_or(gt, jnp.logical_and(eq, rank <= need))
        sel = jnp.logical_and(sel, s > NEG_INF)
        sc_ref[j] = jnp.where(sel, 0.0, NEG_INF)
        return c + jnp.sum(e, axis=-1, keepdims=True)

    lax.fori_loop(0, nt, finish, jnp.zeros((rows, 1), F32))


def _softmax_update(s, m, l, acc, v):
    mn = jnp.maximum(m, jnp.max(s, axis=-1, keepdims=True))
    ms = jnp.where(mn == NEG_INF, 0.0, mn)
    p = jnp.exp(s - ms)
    al = jnp.exp(m - ms)
    return mn, al * l + jnp.sum(p, axis=-1, keepdims=True), al * acc + _dot(p.astype(BF16), v)


def _dsa_kernel(qi_ref, kk_ref, wi_ref, qb_ref, kb_ref, vb_ref, bias_ref, tri_ref, o_ref, sc_ref,
                *, topk):
    qblk = pl.program_id(1)
    nt = qblk + 1
    tq = qi_ref.shape[0]
    lo = lax.broadcasted_iota(jnp.int32, (tq, LANES), 1) < HEAD_DIM
    zero = jnp.zeros((tq, LANES), BF16)

    def halves(qp):
        return jnp.where(lo, qp, zero), jnp.where(lo, zero, qp)

    wi = wi_ref[...]

    def index_tile(j, _):
        off = pl.multiple_of(j * TB, TB)
        kk = kk_ref[pl.ds(off, TB), :]
        acc = jnp.zeros((tq, TB), F32)
        for m in range(IDX_HEADS // 2):
            q_lo, q_hi = halves(qi_ref[:, m * LANES:(m + 1) * LANES])
            acc = acc + wi[:, 2 * m:2 * m + 1] * jnp.maximum(_dot_nt(q_lo, kk), 0.0)
            acc = acc + wi[:, 2 * m + 1:2 * m + 2] * jnp.maximum(_dot_nt(q_hi, kk), 0.0)
        row = lax.broadcasted_iota(jnp.int32, (tq, TB), 0) + qblk * TB
        col = lax.broadcasted_iota(jnp.int32, (tq, TB), 1) + j * TB
        sc_ref[j] = jnp.where(col <= row, acc, NEG_INF)
        return 0

    lax.fori_loop(0, nt, index_tile, 0)
    _topk_select(sc_ref, nt, topk, tri_ref)

    for m in range(B_HEADS // 2):
        q_halves = halves(qb_ref[:, m * LANES:(m + 1) * LANES])
        outs = []
        for half in range(2):
            h = 2 * m + half
            qm = q_halves[half]

            def body(j, carry, qm=qm, h=h, m=m):
                mx, l, acc = carry
                off = pl.multiple_of(j * TB, TB)
                k = kb_ref[pl.ds(off, TB), m * LANES:(m + 1) * LANES]
                v = vb_ref[pl.ds(off, TB), m * LANES:(m + 1) * LANES]
                s = _dot_nt(qm, k) + bias_ref[h, jnp.minimum(qblk - j, 2)] + sc_ref[j]
                return _softmax_update(s, mx, l, acc, v)

            _, l, acc = lax.fori_loop(
                0, nt, body,
                (jnp.full((tq, 1), NEG_INF, F32), jnp.zeros((tq, 1), F32), jnp.zeros((tq, LANES), F32)))
            outs.append(acc / l)
        o_ref[:, m * LANES:(m + 1) * LANES] = jnp.where(lo, outs[0], outs[1]).astype(BF16)


def _dsa_prompt(p, bias_b, tri, b, t, topk):
    nq = t // TB
    return pl.pallas_call(
        functools.partial(_dsa_kernel, topk=topk),
        grid=(b, nq),
        in_specs=[pl.BlockSpec((TB, I_WIDTH), lambda bi, i: (bi * nq + i, 0)),
                  pl.BlockSpec((t, LANES), lambda bi, i: (bi, 0)),
                  pl.BlockSpec((TB, LANES), lambda bi, i: (bi * nq + i, 0)),
                  pl.BlockSpec((TB, B_WIDTH), lambda bi, i: (bi * nq + i, 0)),
                  pl.BlockSpec((t, B_WIDTH), lambda bi, i: (bi, 0)),
                  pl.BlockSpec((t, B_WIDTH), lambda bi, i: (bi, 1)),
                  pl.BlockSpec((B_HEADS, 3, TB, TB), lambda bi, i: (0, 0, 0, 0)),
                  pl.BlockSpec((TB, TB), lambda bi, i: (0, 0))],
        out_specs=pl.BlockSpec((TB, B_WIDTH), lambda bi, i: (bi * nq + i, 0)),
        out_shape=jax.ShapeDtypeStruct((b * t, B_WIDTH), BF16),
        scratch_shapes=[pltpu.VMEM((nq, TB, TB), F32)],
        compiler_params=_cparams(("arbitrary", "arbitrary")),
        name="dsa_prompt",
    )(p["qi"], p["kk"], p["wi"], p["qb"], p["kvb"], p["kvb"], bias_b, tri)


def _sample_idx_kernel(pt_ref, q_ref, w_ref, kn_ref, *rest, n_pages):
    page_refs, o_ref = rest[:n_pages], rest[n_pages]
    q = q_ref[0]
    w = w_ref[0]
    nq = q.shape[0] // IDX_HEADS

    def score(keys):
        rel = jnp.maximum(_dot_nt(q, keys), 0.0) * w
        sc = rel[0:nq]
        for h in range(1, IDX_HEADS):
            sc = sc + rel[h * nq:(h + 1) * nq]
        return sc

    for k in range(n_pages):
        o_ref[k] = score(page_refs[k][0].astype(BF16))
    new = jnp.concatenate([kn_ref[0], jnp.zeros((PAGE_ROWS - nq, IDX_DIM), F32)], axis=0).astype(BF16)
    row = lax.broadcasted_iota(jnp.int32, (nq, PAGE_ROWS), 0)
    col = lax.broadcasted_iota(jnp.int32, (nq, PAGE_ROWS), 1)
    o_ref[n_pages] = jnp.where(col <= row, score(new), NEG_INF)


def _sample_idx(page_table, q_stack, w_stack, ki_new, cache_kidx):
    s, n_pages = page_table.shape
    nq = ki_new.shape[1]
    hq = q_stack.shape[1]
    page_spec = lambda k: pl.BlockSpec((1, PAGE_ROWS, IDX_DIM), lambda i, pt, k=k: (pt[i, k], 0, 0))
    grid_spec = pltpu.PrefetchScalarGridSpec(
        num_scalar_prefetch=1,
        grid=(s,),
        in_specs=[pl.BlockSpec((1, hq, IDX_DIM), lambda i, pt: (i, 0, 0)),
                  pl.BlockSpec((1, hq, 1), lambda i, pt: (i, 0, 0)),
                  pl.BlockSpec((1, nq, IDX_DIM), lambda i, pt: (i, 0, 0))]
                 + [page_spec(k) for k in range(n_pages)],
        out_specs=pl.BlockSpec((n_pages + 1, nq, PAGE_ROWS), lambda i, pt: (0, i, 0)),
    )
    return pl.pallas_call(
        functools.partial(_sample_idx_kernel, n_pages=n_pages),
        grid_spec=grid_spec,
        out_shape=jax.ShapeDtypeStruct((n_pages + 1, s * nq, PAGE_ROWS), F32),
        compiler_params=_cparams(("arbitrary",)),
        name="sample_idx",
    )(page_table, q_stack, w_stack, ki_new, *([cache_kidx] * n_pages))


def _select_kernel(sc_ref, tri_ref, o_ref, *, topk):
    o_ref[...] = sc_ref[...]
    _topk_select(o_ref, o_ref.shape[0], topk, tri_ref)


def _sample_select(scores, tri, topk):
    nt, rows, tw = scores.shape
    tr = min(rows, 256)
    return pl.pallas_call(
        functools.partial(_select_kernel, topk=topk),
        grid=(rows // tr,),
        in_specs=[pl.BlockSpec((nt, tr, tw), lambda i: (0, i, 0)),
                  pl.BlockSpec((tw, tw), lambda i: (0, 0))],
        out_specs=pl.BlockSpec((nt, tr, tw), lambda i: (0, i, 0)),
        out_shape=jax.ShapeDtypeStruct(scores.shape, F32),
        compiler_params=_cparams(("arbitrary",)),
        name="sample_select",
    )(scores, tri)


def _sample_attn_kernel(pt_ref, lam_ref, g_ref, qa_ref, qb_ref, mska_ref, mskb_ref, biasa_ref, biasb_ref,
                        kvan_ref, kvbn_ref, selp_ref, seln_ref, *rest, n_chunks, lam_init):
    g_pages = PAGES_PER_STEP
    kva_refs, kvb_refs = rest[:g_pages], rest[g_pages:2 * g_pages]
    oa_ref, ob_ref = rest[2 * g_pages], rest[2 * g_pages + 1]
    qa_s, qb_s, ma_s, la_s, acca_s, mb_s, lb_s, accb_s = rest[2 * g_pages + 2:]
    c = pl.program_id(1)
    nq = qa_ref.shape[1]
    rows_a = A_HEADS * 2 * nq
    rows_b = B_HEADS * nq

    @pl.when(c == 0)
    def _():
        qa_s[...] = (jnp.concatenate([qa_ref[0]] * (2 * A_HEADS), axis=0) * mska_ref[...]).astype(BF16)
        qb_s[...] = (jnp.concatenate([qb_ref[0]] * B_HEADS, axis=0) * mskb_ref[...]).astype(BF16)
        ma_s[...] = jnp.full(ma_s.shape, NEG_INF, F32)
        mb_s[...] = jnp.full(mb_s.shape, NEG_INF, F32)
        la_s[...] = jnp.zeros(la_s.shape, F32)
        lb_s[...] = jnp.zeros(lb_s.shape, F32)
        acca_s[...] = jnp.zeros(acca_s.shape, F32)
        accb_s[...] = jnp.zeros(accb_s.shape, F32)

    def diag_a(r):
        return jnp.concatenate(
            [r[h * 2 * nq:(h + 1) * 2 * nq, h * LANES:(h + 1) * LANES] for h in range(A_HEADS)], axis=0)

    def diag_b(r):
        return jnp.concatenate(
            [r[m * 2 * nq:(m + 1) * 2 * nq, m * LANES:(m + 1) * LANES] for m in range(B_HEADS // 2)], axis=0)

    def attend(kva, kvb, bias_a, bias_b, sel):
        ka, va = kva[:, :A_WIDTH].astype(BF16), kva[:, A_WIDTH:].astype(BF16)
        s = _dot_nt(qa_s[...], ka) + bias_a
        mn = jnp.maximum(ma_s[...], jnp.max(s, axis=-1, keepdims=True))
        p = jnp.exp(s - mn)
        al = jnp.exp(ma_s[...] - mn)
        la_s[...] = al * la_s[...] + jnp.sum(p, axis=-1, keepdims=True)
        acca_s[...] = al * acca_s[...] + diag_a(_dot(p.astype(BF16), va))
        ma_s[...] = mn

        kb, vb = kvb[:, :B_WIDTH].astype(BF16), kvb[:, B_WIDTH:].astype(BF16)
        s = _dot_nt(qb_s[...], kb) + bias_b + jnp.concatenate([sel] * B_HEADS, axis=0)
        mn = jnp.maximum(mb_s[...], jnp.max(s, axis=-1, keepdims=True))
        ms = jnp.where(mn == NEG_INF, 0.0, mn)
        p = jnp.exp(s - ms)
        al = jnp.exp(mb_s[...] - ms)
        lb_s[...] = al * lb_s[...] + jnp.sum(p, axis=-1, keepdims=True)
        accb_s[...] = al * accb_s[...] + diag_b(_dot(p.astype(BF16), vb))
        mb_s[...] = mn

    for g in range(g_pages):
        last = jnp.logical_and(c == n_chunks - 1, g == g_pages - 1)
        bias_a = jnp.where(last, biasa_ref[:, :PAGE_ROWS], 0.0)
        bias_b = jnp.where(last, biasb_ref[:, :PAGE_ROWS], 0.0)
        attend(kva_refs[g][0], kvb_refs[g][0], bias_a, bias_b, selp_ref[g])

    @pl.when(c == n_chunks - 1)
    def _():
        pad = lambda x: jnp.concatenate([x, jnp.zeros((PAGE_ROWS - nq, x.shape[1]), F32)], axis=0)
        attend(pad(kvan_ref[0]), pad(kvbn_ref[0]), biasa_ref[:, PAGE_ROWS:], biasb_ref[:, PAGE_ROWS:],
               seln_ref[0])
        lam = _lambda_value(lam_ref, lam_init)
        oa = acca_s[...] / la_s[...]
        for h in range(A_HEADS):
            o = oa[h * 2 * nq:h * 2 * nq + nq] - lam * oa[h * 2 * nq + nq:(h + 1) * 2 * nq]
            oa_ref[0, :, h * LANES:(h + 1) * LANES] = _sub_layer_norm(o, g_ref[...], lam_init)
        ob = accb_s[...] / lb_s[...]
        lo = lax.broadcasted_iota(jnp.int32, (nq, LANES), 1) < HEAD_DIM
        for m in range(B_HEADS // 2):
            ob_ref[0, :, m * LANES:(m + 1) * LANES] = jnp.where(
                lo, ob[2 * m * nq:(2 * m + 1) * nq], ob[(2 * m + 1) * nq:(2 * m + 2) * nq])


def _sample_attn(page_table, lam4, g_sub, qa, qb, mask_a, mask_b, bias_a, bias_b, kva_new, kvb_new, sel,
                 cache_a, cache_b, lam_init):
    s, n_pages = page_table.shape
    nq = qa.shape[1]
    g_pages = PAGES_PER_STEP
    n_chunks = n_pages // g_pages
    rows_a, rows_b = A_HEADS * 2 * nq, B_HEADS * nq
    const = lambda shape: pl.BlockSpec(shape, lambda i, c, pt: (0,) * len(shape))
    seq = lambda shape: pl.BlockSpec(shape, lambda i, c, pt: (i,) + (0,) * (len(shape) - 1))
    page = lambda w, g: pl.BlockSpec((1, PAGE_ROWS, w), lambda i, c, pt, g=g: (pt[i, c * g_pages + g], 0, 0))
    grid_spec = pltpu.PrefetchScalarGridSpec(
        num_scalar_prefetch=1,
        grid=(s, n_chunks),
        in_specs=[const((4, HEAD_DIM)), const((1, 2 * HEAD_DIM)),
                  seq((1, nq, A_WIDTH)), seq((1, nq, B_WIDTH)),
                  const((rows_a, A_WIDTH)), const((rows_b, B_WIDTH)),
                  const((rows_a, 2 * PAGE_ROWS)), const((rows_b, 2 * PAGE_ROWS)),
                  seq((1, nq, 2 * A_WIDTH)), seq((1, nq, 2 * B_WIDTH)),
                  pl.BlockSpec((g_pages, nq, PAGE_ROWS), lambda i, c, pt: (c, i, 0)),
                  pl.BlockSpec((1, nq, PAGE_ROWS), lambda i, c, pt: (n_pages, i, 0))]
                 + [page(2 * A_WIDTH, g) for g in range(g_pages)]
                 + [page(2 * B_WIDTH, g) for g in range(g_pages)],
        out_specs=[seq((1, nq, A_WIDTH)), seq((1, nq, B_WIDTH))],
        scratch_shapes=[pltpu.VMEM((rows_a, A_WIDTH), BF16), pltpu.VMEM((rows_b, B_WIDTH), BF16),
                        pltpu.VMEM((rows_a, 1), F32), pltpu.VMEM((rows_a, 1), F32),
                        pltpu.VMEM((rows_a, LANES), F32),
                        pltpu.VMEM((rows_b, 1), F32), pltpu.VMEM((rows_b, 1), F32),
                        pltpu.VMEM((rows_b, LANES), F32)],
    )
    return pl.pallas_call(
        functools.partial(_sample_attn_kernel, n_chunks=n_chunks, lam_init=lam_init),
        grid_spec=grid_spec,
        out_shape=[jax.ShapeDtypeStruct((s, nq, A_WIDTH), F32), jax.ShapeDtypeStruct((s, nq, B_WIDTH), F32)],
        compiler_params=_cparams(("arbitrary", "arbitrary")),
        name="sample_attn",
    )(page_table, lam4, g_sub, qa, qb, mask_a, mask_b, bias_a, bias_b, kva_new, kvb_new, sel, sel,
      *([cache_a] * g_pages), *([cache_b] * g_pages))


def _outproj_kernel(oa_ref, ob_ref, sg_ref, x_ref, g1_ref, sh2_ref, sc2_ref, wba_ref, wbb_ref, wout_ref,
                    lng_ref, lnb_ref, wr_ref, br_ref, x1_ref, h2_ref, comb_ref, *, alpha):
    ya = _dot(oa_ref[...], wba_ref[...])
    yb = _dot(ob_ref[...], wbb_ref[...])
    t = sg_ref[:, :D_MODEL] * ya + sg_ref[:, D_MODEL:] * yb
    mix = _dot(t.astype(BF16), wout_ref[...])
    x1 = _layer_norm(alpha * x_ref[...] + g1_ref[0] * mix, lng_ref[...], lnb_ref[...])
    x1_ref[...] = x1
    h2 = x1 * (1.0 + sc2_ref[0]) + sh2_ref[0]
    h2_ref[...] = h2.astype(BF16)

    logits = _dot3(h2, wr_ref[...]) + br_ref[...]
    lane = lax.broadcasted_iota(jnp.int32, logits.shape, 1).astype(F32)
    big = float(LANES)
    is_group = jnp.logical_and(lane >= N_EXPERTS, lane < N_EXPERTS + N_GROUPS)
    lg = jnp.where(is_group, logits, NEG_INF)
    mg = jnp.max(lg, axis=-1, keepdims=True)
    g_sel = jnp.min(jnp.where(lg == mg, lane, big), axis=-1, keepdims=True) - N_EXPERTS
    p_g = 1.0 / jnp.sum(jnp.exp(lg - mg), axis=-1, keepdims=True)
    first = g_sel * EXPERTS_PER_GROUP
    in_group = jnp.logical_and(lane >= first, lane < first + EXPERTS_PER_GROUP)
    le = jnp.where(in_group, logits, NEG_INF)
    ex = jnp.exp(le - jnp.max(le, axis=-1, keepdims=True))
    pe = jnp.where(in_group, ex / jnp.sum(ex, axis=-1, keepdims=True), -1.0)
    v1 = jnp.max(pe, axis=-1, keepdims=True)
    i1 = jnp.min(jnp.where(pe == v1, lane, big), axis=-1, keepdims=True)
    pe2 = jnp.where(lane == i1, -1.0, pe)
    v2 = jnp.max(pe2, axis=-1, keepdims=True)
    i2 = jnp.min(jnp.where(pe2 == v2, lane, big), axis=-1, keepdims=True)
    tot = v1 + v2
    comb_ref[...] = (jnp.where(lane == i1, p_g * (v1 / tot), 0.0)
                     + jnp.where(lane == i2, p_g * (v2 / tot), 0.0))


def _out_proj(oa, ob, sg, x, ada3, wba, wbb, wout, lng, lnb, wr, br, rows_per_batch, alpha):
    n = x.shape[0]
    tm = ROW_TILE
    row = lambda w: pl.BlockSpec((tm, w), lambda i: (i, 0))
    full = lambda a: pl.BlockSpec(a.shape, lambda i: (0,) * a.ndim)
    return pl.pallas_call(
        functools.partial(_outproj_kernel, alpha=alpha),
        grid=(n // tm,),
        in_specs=[row(A_WIDTH), row(B_WIDTH), row(2 * D_MODEL), row(D_MODEL),
                  _row_vec_spec(ada3, 2, tm, rows_per_batch),
                  _row_vec_spec(ada3, 3, tm, rows_per_batch),
                  _row_vec_spec(ada3, 4, tm, rows_per_batch),
                  full(wba), full(wbb), full(wout), full(lng), full(lnb), full(wr), full(br)],
        out_specs=[row(D_MODEL), row(D_MODEL), row(LANES)],
        out_shape=[jax.ShapeDtypeStruct((n, D_MODEL), F32), jax.ShapeDtypeStruct((n, D_MODEL), BF16),
                   jax.ShapeDtypeStruct((n, LANES), F32)],
        compiler_params=_cparams(("arbitrary",)),
        name="out_proj",
    )(oa, ob, sg, x, ada3, ada3, ada3, wba, wbb, wout, lng, lnb, wr, br)


def _moe_kernel(h_ref, x1_ref, comb_ref, g2_ref, wup_ref, wdn_ref, lng_ref, lnb_ref, o_ref, acc_ref, *, alpha):
    e = pl.program_id(1)

    @pl.when(e == 0)
    def _():
        acc_ref[...] = jnp.zeros(acc_ref.shape, F32)

    hid = _dot(h_ref[...], wup_ref[0])
    gate, up = hid[:, :EXPERT_HIDDEN], hid[:, EXPERT_HIDDEN:]
    act = (gate * _sigmoid(gate) * up).astype(BF16)
    y = _dot(act, wdn_ref[0])
    comb = comb_ref[...]
    lane = lax.broadcasted_iota(jnp.int32, comb.shape, 1)
    w = jnp.sum(jnp.where(lane == e, comb, 0.0), axis=-1, keepdims=True)
    acc_ref[...] += w * y

    @pl.when(e == N_EXPERTS - 1)
    def _():
        u = alpha * x1_ref[...] + g2_ref[0] * acc_ref[...]
        o_ref[...] = _layer_norm(u, lng_ref[...], lnb_ref[...])


def _moe(h2, x1, comb, ada3, wup, wdn, lng, lnb, rows_per_batch, alpha):
    n = x1.shape[0]
    tm = min(MOE_TILE, n)
    if ada3.shape[1] != 1:
        ada3 = ada3.reshape(n // tm, tm, ada3.shape[2])
    row = lambda w: pl.BlockSpec((tm, w), lambda i, e: (i, 0))
    full = lambda a: pl.BlockSpec(a.shape, lambda i, e: (0,) * a.ndim)
    return pl.pallas_call(
        functools.partial(_moe_kernel, alpha=alpha),
        grid=(n // tm, N_EXPERTS),
        in_specs=[row(D_MODEL), row(D_MODEL), row(LANES),
                  _row_vec_spec(ada3, 5, tm, rows_per_batch),
                  pl.BlockSpec((1, D_MODEL, 2 * EXPERT_HIDDEN), lambda i, e: (e, 0, 0)),
                  pl.BlockSpec((1, EXPERT_HIDDEN, D_MODEL), lambda i, e: (e, 0, 0)),
                  full(lng), full(lnb)],
        out_specs=row(D_MODEL),
        out_shape=jax.ShapeDtypeStruct((n, D_MODEL), F32),
        scratch_shapes=[pltpu.VMEM((tm, D_MODEL), F32)],
        compiler_params=_cparams(("arbitrary", "arbitrary")),
        name="moe",
    )(h2, x1, comb, ada3, wup, wdn, lng, lnb)


def _prompt_dist():
    i = jnp.arange(TB, dtype=jnp.int32)[:, None]
    j = jnp.arange(TB, dtype=jnp.int32)[None, :]
    return jnp.concatenate([d * TB + i - j for d in range(3)], axis=0)


def _sample_dist(nq, past_len):
    i = jnp.arange(nq, dtype=jnp.int32)[:, None]
    j = jnp.arange(PAGE_ROWS, dtype=jnp.int32)[None, :]
    last_page = past_len + i - (past_len - PAGE_ROWS + j)
    new = jnp.where(j < nq, i - j, -1)
    return jnp.concatenate([last_page, new], axis=1)


def kernel(x_prompt, x_sample, cache_kv_diff, cache_kv_dsa, cache_kidx, page_table, c_prompt, c_sample,
           rel_bias, w_ada, b_ada, w_in, lambda_q1, lambda_k1, lambda_q2, lambda_k2, subln_g, w_branch_a,
           w_branch_b, w_out, ln1_g, ln1_b, w_router_group, b_router_group, w_router_expert,
           b_router_expert, w_up, w_down, ln2_g, ln2_b):
    b, t, d = x_prompt.shape
    s, nq, _ = x_sample.shape
    depth = w_in.shape[0]
    n_pool = cache_kidx.shape[1]
    n_pages = page_table.shape[1]
    past_len = n_pages * PAGE_ROWS
    alpha = (2 * depth) ** 0.25
    topk_p = min(DSA_TOPK_MAX, t // 4)
    topk_s = min(DSA_TOPK_MAX, (past_len + nq) // 4)
    assert d == D_MODEL and t % TB == 0 and (s * nq) % ROW_TILE == 0 and n_pages % PAGES_PER_STEP == 0
    assert cache_kidx.shape[2] == PAGE_ROWS and nq <= 8

    bias_p = _bias_tiles(rel_bias, _prompt_dist()).reshape(A_HEADS + B_HEADS, 3, TB, TB)
    bias_s = _bias_tiles(rel_bias, _sample_dist(nq, past_len))
    bias_sa = jnp.broadcast_to(bias_s[:A_HEADS, None], (A_HEADS, 2, nq, 2 * PAGE_ROWS)).reshape(
        A_HEADS * 2 * nq, 2 * PAGE_ROWS)
    bias_sb = bias_s[A_HEADS:].reshape(B_HEADS * nq, 2 * PAGE_ROWS)
    tri_p = (jnp.arange(TB)[:, None] <= jnp.arange(TB)[None, :]).astype(BF16)
    tri_s = tri_p[:PAGE_ROWS, :PAGE_ROWS]
    lane_a = jnp.arange(A_WIDTH)[None, :] // HEAD_DIM
    mask_a = (lane_a == (jnp.arange(A_HEADS * 2 * nq)[:, None] // nq)).astype(F32)
    lane_b = jnp.arange(B_WIDTH)[None, :] // HEAD_DIM
    mask_b = (lane_b == (jnp.arange(B_HEADS * nq)[:, None] // nq)).astype(F32)

    xp = x_prompt.reshape(b * t, d)
    xs = x_sample.reshape(s * nq, d)
    c_all = jnp.concatenate([c_prompt, c_sample], axis=0)
    c_all = jnp.pad(c_all, ((0, -(b + s) % 8), (0, 0)))
    outs = [[] for _ in range(6)]
    for l in range(depth):
        lam_init = 0.8 - 0.6 * math.exp(-0.3 * l)
        lam4 = jnp.stack([lambda_q1[l], lambda_k1[l], lambda_q2[l], lambda_k2[l]]).astype(F32)
        g_sub = subln_g[l].reshape(1, 2 * HEAD_DIM)
        w = w_in[l]
        w2 = jnp.concatenate([w[:, :C_KI + IDX_DIM], w[:, C_KI:C_KI + IDX_DIM],
                              w[:, C_KI + IDX_DIM:C_KI + IDX_DIM + IDX_HEADS],
                              jnp.zeros((d, LANES - IDX_HEADS), w.dtype),
                              w[:, C_KI + IDX_DIM + IDX_HEADS:]], axis=1).astype(BF16)
        wba, wbb, wout = w_branch_a[l].astype(BF16), w_branch_b[l].astype(BF16), w_out[l].astype(BF16)
        wup, wdn = w_up[l].astype(BF16), w_down[l].astype(BF16)
        wr = jnp.concatenate([w_router_expert[l], w_router_group[l],
                              jnp.zeros((d, LANES - N_EXPERTS - N_GROUPS), F32)], axis=1)
        br = jnp.concatenate([b_router_expert[l], b_router_group[l],
                              jnp.zeros((LANES - N_EXPERTS - N_GROUPS,), F32)]).reshape(1, LANES)
        ln1 = (ln1_g[l].reshape(1, d), ln1_b[l].reshape(1, d))
        ln2 = (ln2_g[l].reshape(1, d), ln2_b[l].reshape(1, d))

        ada = _ada(c_all, w_ada[l], b_ada[l])
        ada_p = ada[:b].reshape(b, 1, 6 * d)
        ada_s = jnp.broadcast_to(ada[b:b + s, None], (s, nq, 6 * d)).reshape(s * nq // ROW_TILE, ROW_TILE, 6 * d)

        p = _in_proj(xp, ada_p, w2, t)
        oa = _diff_attn_prompt(lam4, g_sub, p["qa"], p["kva"], bias_p[:A_HEADS], b, t, lam_init)
        ob = _dsa_prompt(p, bias_p[A_HEADS:], tri_p, b, t, topk_p)
        x1, h2, comb = _out_proj(oa, ob, p["sg"], xp, ada_p, wba, wbb, wout, *ln1, wr, br, t, alpha)
        xp = _moe(h2, x1, comb, ada_p, wup, wdn, *ln2, t, alpha)
        outs[0].append(p["ra"].reshape(b, t, 2, A_HEADS, 2 * HEAD_DIM))
        outs[1].append(p["rb"].reshape(b, t, 2, B_HEADS, HEAD_DIM))
        outs[2].append(p["ki"].reshape(b, t, IDX_DIM))

        q = _in_proj(xs, ada_s, w2, nq)
        q_stack = q["qi"].reshape(s, nq, IDX_HEADS, IDX_DIM).transpose(0, 2, 1, 3).reshape(
            s, IDX_HEADS * nq, IDX_DIM)
        w_stack = q["wi"][:, :IDX_HEADS].reshape(s, nq, IDX_HEADS).transpose(0, 2, 1).reshape(
            s, IDX_HEADS * nq, 1)
        scores = _sample_idx(page_table, q_stack, w_stack, q["ki"].reshape(s, nq, IDX_DIM),
                             cache_kidx[l])
        sel = _sample_select(scores, tri_s, topk_s)
        oa_s, ob_s = _sample_attn(
            page_table, lam4, g_sub,
            q["qa"].astype(F32).reshape(s, nq, A_WIDTH), q["qb"].astype(F32).reshape(s, nq, B_WIDTH),
            mask_a, mask_b, bias_sa, bias_sb,
            q["ra"].reshape(s, nq, 2 * A_WIDTH), q["rb"].reshape(s, nq, 2 * B_WIDTH), sel,
            cache_kv_diff[l].reshape(n_pool, PAGE_ROWS, 2 * A_WIDTH),
            cache_kv_dsa[l].reshape(n_pool, PAGE_ROWS, 2 * B_WIDTH), lam_init)
        x1, h2, comb = _out_proj(oa_s.reshape(s * nq, A_WIDTH).astype(BF16),
                                 ob_s.reshape(s * nq, B_WIDTH).astype(BF16),
                                 q["sg"], xs, ada_s, wba, wbb, wout, *ln1, wr, br, nq, alpha)
        xs = _moe(h2, x1, comb, ada_s, wup, wdn, *ln2, nq, alpha)
        outs[3].append(q["ra"].reshape(s, nq, 2, A_HEADS, 2 * HEAD_DIM))
        outs[4].append(q["rb"].reshape(s, nq, 2, B_HEADS, HEAD_DIM))
        outs[5].append(q["ki"].reshape(s, nq, IDX_DIM))

    return (xp.reshape(b, t, d), xs.reshape(s, nq, d)) + tuple(jnp.stack(o, 0) for o in outs)
```

```python
import functools
import math

import jax
import jax.numpy as jnp
from jax import lax
from jax.experimental import pallas as pl
from jax.experimental.pallas import tpu as pltpu

D_MODEL = 1024
HEAD_DIM = 64
A_HEADS = 8
B_HEADS = 8
IDX_HEADS = 8
IDX_DIM = 64
DSA_TOPK_MAX = 256
N_BUCKETS = 32
MAX_DISTANCE = 128
N_GROUPS = 4
EXPERTS_PER_GROUP = 4
N_EXPERTS = N_GROUPS * EXPERTS_PER_GROUP
EXPERT_HIDDEN = 512
LN_EPS = 1e-5

LANES = 128
TB = 256
ROW_TILE = 256
MOE_TILE = 512
PAGE_ROWS = 128
PAGES_PER_STEP = 8
VMEM_LIMIT = 56 * 1024 * 1024

A_WIDTH = A_HEADS * 2 * HEAD_DIM
B_WIDTH = B_HEADS * HEAD_DIM
I_WIDTH = IDX_HEADS * IDX_DIM
C_QA, C_KVA, C_QB, C_KVB, C_QI, C_KI, C_WI, C_G = 0, 1024, 3072, 3584, 4608, 5120, 5248, 5376
W2_WIDTH = C_G + 2 * D_MODEL

F32 = jnp.float32
BF16 = jnp.bfloat16
NEG_INF = float("-inf")
INT_MIN = -2 ** 31


def _cparams(sem):
    return pltpu.CompilerParams(dimension_semantics=sem, vmem_limit_bytes=VMEM_LIMIT)


def _dot(a, b):
    return jnp.dot(a, b, preferred_element_type=F32)


def _dot_nt(a, b):
    return lax.dot_general(a, b, (((1,), (1,)), ((), ())), preferred_element_type=F32)


def _split(a):
    hi = a.astype(BF16)
    lo = (a - hi.astype(F32)).astype(BF16)
    return hi, lo


def _dot3(a, b):
    a_hi, a_lo = _split(a)
    b_hi, b_lo = _split(b)
    return _dot(a_hi, b_hi) + _dot(a_lo, b_hi) + _dot(a_hi, b_lo)


def _sigmoid(x):
    return 1.0 / (1.0 + jnp.exp(-x))


def _layer_norm(u, g, b):
    mu = jnp.mean(u, axis=-1, keepdims=True)
    d = u - mu
    var = jnp.mean(d * d, axis=-1, keepdims=True)
    return d * lax.rsqrt(var + LN_EPS) * g + b


def _bias_kernel(tab_ref, dist_ref, out_ref):
    h = pl.program_id(0)
    d = dist_ref[...]
    n = jnp.maximum(d, 0)
    max_exact = N_BUCKETS // 2
    nf = jnp.maximum(n, 1).astype(F32)
    large = max_exact + (jnp.log(nf / max_exact) / math.log(MAX_DISTANCE / max_exact)
                         * (N_BUCKETS - max_exact)).astype(jnp.int32)
    large = jnp.minimum(large, N_BUCKETS - 1)
    bucket = jnp.where(n < max_exact, n, large)
    last = tab_ref[N_BUCKETS - 1, h]
    acc = jnp.zeros(d.shape, F32)
    for m in range(N_BUCKETS - 1):
        acc = jnp.where(bucket == m, tab_ref[m, h] - last, acc)
    out_ref[0] = jnp.where(d < 0, NEG_INF, acc)


def _bias_tiles(rel_bias, dist):
    n_heads = rel_bias.shape[1]
    r, c = dist.shape
    return pl.pallas_call(
        _bias_kernel,
        grid=(n_heads,),
        in_specs=[pl.BlockSpec(memory_space=pltpu.SMEM),
                  pl.BlockSpec((r, c), lambda h: (0, 0))],
        out_specs=pl.BlockSpec((1, r, c), lambda h: (h, 0, 0)),
        out_shape=jax.ShapeDtypeStruct((n_heads, r, c), F32),
        compiler_params=_cparams(("arbitrary",)),
        name="bias_tiles",
    )(rel_bias, dist)


def _ada_kernel(c_ref, w_ref, b_ref, o_ref):
    c = c_ref[...]
    o_ref[...] = _dot3(c * _sigmoid(c), w_ref[...]) + b_ref[...]


def _ada(c_all, w_ada, b_ada):
    r, d = c_all.shape
    n = w_ada.shape[1]
    tn = 512
    return pl.pallas_call(
        _ada_kernel,
        grid=(n // tn,),
        in_specs=[pl.BlockSpec((r, d), lambda j: (0, 0)),
                  pl.BlockSpec((d, tn), lambda j: (0, j)),
                  pl.BlockSpec((1, tn), lambda j: (0, j))],
        out_specs=pl.BlockSpec((r, tn), lambda j: (0, j)),
        out_shape=jax.ShapeDtypeStruct((r, n), F32),
        compiler_params=_cparams(("arbitrary",)),
        name="ada",
    )(c_all, w_ada, b_ada.reshape(1, n))


def _inproj_kernel(x_ref, sh_ref, sc_ref, w_ref, qa_ref, ra_ref, kva_ref, qb_ref, rb_ref, kvb_ref,
                   qi_ref, ki_ref, kk_ref, wi_ref, sg_ref):
    h = (x_ref[...] * (1.0 + sc_ref[0]) + sh_ref[0]).astype(BF16)
    q_scale = HEAD_DIM ** -0.5

    def mm(c0, n):
        return _dot(h, w_ref[:, c0:c0 + n])

    for c in range(0, A_WIDTH, 512):
        qa_ref[:, c:c + 512] = (mm(C_QA + c, 512) * q_scale).astype(BF16)
    for c in range(0, 2 * A_WIDTH, 512):
        a = mm(C_KVA + c, 512)
        ra_ref[:, c:c + 512] = a
        kva_ref[:, c:c + 512] = a.astype(BF16)
    qb_ref[...] = (mm(C_QB, 512) * q_scale).astype(BF16)
    for c in range(0, 2 * B_WIDTH, 512):
        a = mm(C_KVB + c, 512)
        rb_ref[:, c:c + 512] = a
        kvb_ref[:, c:c + 512] = a.astype(BF16)
    qi_ref[...] = (mm(C_QI, 512) * q_scale).astype(BF16)
    a = mm(C_KI, 2 * LANES)
    ki_ref[...] = a[:, :IDX_DIM]
    kk_ref[...] = a[:, :LANES].astype(BF16)
    wi_ref[...] = a[:, LANES:] * IDX_HEADS ** -0.5
    for c in range(0, 2 * D_MODEL, 512):
        sg_ref[:, c:c + 512] = _sigmoid(mm(C_G + c, 512))


def _row_vec_spec(arr, col, tm, rows_per_batch):
    if arr.shape[1] == 1:
        per = rows_per_batch // tm
        return pl.BlockSpec((1, 1, D_MODEL), lambda i, *_: (i // per, 0, col))
    return pl.BlockSpec((1, tm, D_MODEL), lambda i, *_: (i, 0, col))


def _in_proj(x, ada3, w2, rows_per_batch):
    n = x.shape[0]
    tm = ROW_TILE
    row = lambda w: pl.BlockSpec((tm, w), lambda i: (i, 0))
    outs = [("qa", A_WIDTH, BF16), ("ra", 2 * A_WIDTH, F32), ("kva", 2 * A_WIDTH, BF16),
            ("qb", B_WIDTH, BF16), ("rb", 2 * B_WIDTH, F32), ("kvb", 2 * B_WIDTH, BF16),
            ("qi", I_WIDTH, BF16), ("ki", IDX_DIM, F32), ("kk", LANES, BF16), ("wi", LANES, F32),
            ("sg", 2 * D_MODEL, F32)]
    res = pl.pallas_call(
        _inproj_kernel,
        grid=(n // tm,),
        in_specs=[row(D_MODEL),
                  _row_vec_spec(ada3, 0, tm, rows_per_batch),
                  _row_vec_spec(ada3, 1, tm, rows_per_batch),
                  pl.BlockSpec((D_MODEL, W2_WIDTH), lambda i: (0, 0), pipeline_mode=pl.Buffered(1))],
        out_specs=[row(w) for _, w, _ in outs],
        out_shape=[jax.ShapeDtypeStruct((n, w), dt) for _, w, dt in outs],
        compiler_params=_cparams(("arbitrary",)),
        name="in_proj",
    )(x, ada3, ada3, w2)
    return {name: r for (name, _, _), r in zip(outs, res)}


SUB = 64


def _stack2(x):
    return jnp.concatenate([x, x], axis=0)


N_SUB = TB // SUB


def _two_pass_attend(q_stack, k_ref, v_ref, kcol, vcol, nt, near_bias, every_bias, s_ref, st_ref):
    rows = 2 * SUB
    st_ref[0] = jnp.full((N_SUB, rows, LANES), NEG_INF, F32)
    st_ref[1] = jnp.zeros((N_SUB, rows, LANES), F32)
    st_ref[2] = jnp.zeros((N_SUB, rows, LANES), F32)

    def first(near):
        def body(j, _):
            off = pl.multiple_of(j * TB, TB)
            k = k_ref[pl.ds(off, TB), kcol:kcol + LANES]
            for r in range(N_SUB):
                s = _dot_nt(q_stack(r), k)
                if near:
                    s = s + near_bias(r, j)
                if every_bias is not None:
                    s = s + every_bias(r, j)
                s_ref[j, r] = s
                st_ref[0, r] = jnp.maximum(st_ref[0, r], jnp.maximum(s[:, :LANES], s[:, LANES:]))
            return 0
        return body

    n_far = jnp.maximum(nt - 2, 0)
    lax.fori_loop(0, n_far, first(False), 0)
    lax.fori_loop(n_far, nt, first(True), 0)
    for r in range(N_SUB):
        st_ref[0, r] = jnp.broadcast_to(jnp.max(st_ref[0, r], axis=-1, keepdims=True), (rows, LANES))

    def second(j, _):
        off = pl.multiple_of(j * TB, TB)
        v = v_ref[pl.ds(off, TB), vcol:vcol + LANES]
        for r in range(N_SUB):
            s = s_ref[j, r]
            mb = st_ref[0, r]
            p_lo = jnp.exp(s[:, :LANES] - mb)
            p_hi = jnp.exp(s[:, LANES:] - mb)
            st_ref[1, r] += p_lo + p_hi
            st_ref[2, r] += _dot(jnp.concatenate([p_lo, p_hi], axis=1).astype(BF16), v)
        return 0

    lax.fori_loop(0, nt, second, 0)
    return lambda r: (st_ref[2, r], jnp.sum(st_ref[1, r], axis=-1, keepdims=True))


def _lambda_value(lam_ref, lam_init):
    a = jnp.sum(lam_ref[0:1, :] * lam_ref[1:2, :], axis=-1, keepdims=True)
    b = jnp.sum(lam_ref[2:3, :] * lam_ref[3:4, :], axis=-1, keepdims=True)
    return jnp.exp(a) - jnp.exp(b) + lam_init


def _sub_layer_norm(o, g, lam_init):
    o = o * lax.rsqrt(jnp.mean(o * o, axis=-1, keepdims=True) + LN_EPS)
    return o * g * (1.0 - lam_init)


def _masked_pair(q):
    lo = lax.broadcasted_iota(jnp.int32, q.shape, 1) < HEAD_DIM
    zero = jnp.zeros_like(q)
    return jnp.concatenate([jnp.where(lo, q, zero), jnp.where(lo, zero, q)], axis=0)


def _sub_rows(r):
    return slice(r * SUB, (r + 1) * SUB)


def _diff_attn_kernel(lam_ref, g_ref, q_ref, k_ref, v_ref, bias_ref, o_ref, s_ref, st_ref, *, lam_init):
    qi = pl.program_id(2)
    q_stack = lambda r: _masked_pair(q_ref[_sub_rows(r), :])
    near = lambda r, j: _stack2(bias_ref[0, jnp.minimum(qi - j, 2), _sub_rows(r), :])
    result = _two_pass_attend(q_stack, k_ref, v_ref, 0, 0, qi + 1, near, None, s_ref, st_ref)
    lam = _lambda_value(lam_ref, lam_init)
    for r in range(N_SUB):
        acc, l = result(r)
        o = acc / l
        o = o[:SUB] - lam * o[SUB:]
        o_ref[_sub_rows(r), :] = _sub_layer_norm(o, g_ref[...], lam_init).astype(BF16)


def _diff_attn_prompt(lam4, g_sub, qa, kva, bias_p, b, t, lam_init):
    nq = t // TB
    return pl.pallas_call(
        functools.partial(_diff_attn_kernel, lam_init=lam_init),
        grid=(b, A_HEADS, nq),
        in_specs=[pl.BlockSpec((4, HEAD_DIM), lambda bi, h, i: (0, 0)),
                  pl.BlockSpec((1, 2 * HEAD_DIM), lambda bi, h, i: (0, 0)),
                  pl.BlockSpec((TB, LANES), lambda bi, h, i: (bi * nq + i, h)),
                  pl.BlockSpec((t, LANES), lambda bi, h, i: (bi, h)),
                  pl.BlockSpec((t, LANES), lambda bi, h, i: (bi, A_HEADS + h)),
                  pl.BlockSpec((1, 3, TB, TB), lambda bi, h, i: (h, 0, 0, 0))],
        out_specs=pl.BlockSpec((TB, LANES), lambda bi, h, i: (bi * nq + i, h)),
        out_shape=jax.ShapeDtypeStruct((b * t, A_WIDTH), BF16),
        scratch_shapes=[pltpu.VMEM((nq, N_SUB, 2 * SUB, TB), F32), pltpu.VMEM((3, N_SUB, 2 * SUB, LANES), F32)],
        compiler_params=_cparams(("arbitrary", "arbitrary", "arbitrary")),
        name="diff_attn_prompt",
    )(lam4, g_sub, qa, kva, kva, bias_p)


KEY_NEG_INF = INT_MIN + 0x7FFFFF


def _key_to_float(key):
    bits = jnp.where(key < 0, key ^ jnp.int32(0x7FFFFFFF), key)
    return jnp.where(key < KEY_NEG_INF, NEG_INF, lax.bitcast_convert_type(bits, F32))


def _topk_select(sc_ref, nt, topk, tri_ref):
    rows, tw = sc_ref.shape[1], sc_ref.shape[2]
    kf = float(topk)

    def count_ge(cf):
        def body(j, acc):
            g = jnp.where(sc_ref[j] >= cf, 1.0, 0.0)
            r = g[:, :LANES]
            for c in range(1, tw // LANES):
                r = r + g[:, c * LANES:(c + 1) * LANES]
            return acc + r
        acc = lax.fori_loop(0, nt, body, jnp.zeros((rows, LANES), F32))
        return jnp.sum(acc, axis=-1, keepdims=True)

    def search(p, t):
        cand = t + lax.shift_left(jnp.int32(1), (31 - p).astype(jnp.int32))
        c = count_ge(_key_to_float(cand))
        return jnp.where(c >= kf, cand, t)

    t = lax.fori_loop(0, 32, search, jnp.full((rows, 1), INT_MIN, jnp.int32))
    t_lo = _key_to_float(t)
    t_hi = _key_to_float(t + 1)
    need = kf - count_ge(t_hi)
    tri = tri_ref[...]

    def finish(j, c):
        s = sc_ref[j]
        gt = s >= t_hi
        eq = jnp.logical_and(s >= t_lo, jnp.logical_not(gt))
        e = jnp.where(eq, 1.0, 0.0)
        rank = _dot(e.astype(BF16), tri) + c
        sel = jnp.logical_or(gt, jnp.logical_and(eq, rank <= need))
        sel = jnp.logical_and(sel, s > NEG_INF)
        sc_ref[j] = jnp.where(sel, 0.0, NEG_INF)
        return c + jnp.sum(e, axis=-1, keepdims=True)

    lax.fori_loop(0, nt, finish, jnp.zeros((rows, 1), F32))


def _dsa_kernel(qi_ref, kk_ref, wi_ref, qb_ref, kb_ref, vb_ref, bias_ref, tri_ref, o_ref, sc_ref, s_ref,
                st_ref, *, topk):
    qblk = pl.program_id(1)
    nt = qblk + 1
    tq = qi_ref.shape[0]
    lo = lax.broadcasted_iota(jnp.int32, (tq, LANES), 1) < HEAD_DIM
    zero = jnp.zeros((tq, LANES), BF16)

    def halves(qp):
        return jnp.where(lo, qp, zero), jnp.where(lo, zero, qp)

    wi = wi_ref[...]

    def index_tile(j, _):
        off = pl.multiple_of(j * TB, TB)
        kk = kk_ref[pl.ds(off, TB), :]
        acc = jnp.zeros((tq, TB), F32)
        for m in range(IDX_HEADS // 2):
            q_lo, q_hi = halves(qi_ref[:, m * LANES:(m + 1) * LANES])
            acc = acc + wi[:, 2 * m:2 * m + 1] * jnp.maximum(_dot_nt(q_lo, kk), 0.0)
            acc = acc + wi[:, 2 * m + 1:2 * m + 2] * jnp.maximum(_dot_nt(q_hi, kk), 0.0)
        row = lax.broadcasted_iota(jnp.int32, (tq, TB), 0) + qblk * TB
        col = lax.broadcasted_iota(jnp.int32, (tq, TB), 1) + j * TB
        sc_ref[j] = jnp.where(col <= row, acc, NEG_INF)
        return 0

    lax.fori_loop(0, nt, index_tile, 0)
    _topk_select(sc_ref, nt, topk, tri_ref)

    lo_sub = lax.broadcasted_iota(jnp.int32, (SUB, LANES), 1) < HEAD_DIM
    every = lambda r, j: _stack2(sc_ref[j, _sub_rows(r), :])
    for m in range(B_HEADS // 2):
        cols = slice(m * LANES, (m + 1) * LANES)
        q_stack = lambda r, cols=cols: _masked_pair(qb_ref[_sub_rows(r), cols])

        def near(r, j, m=m):
            d = jnp.minimum(qblk - j, 2)
            return jnp.concatenate([bias_ref[2 * m, d, _sub_rows(r), :], bias_ref[2 * m + 1, d, _sub_rows(r), :]],
                                   axis=0)

        result = _two_pass_attend(q_stack, kb_ref, vb_ref, m * LANES, m * LANES, nt, near, every, s_ref, st_ref)
        for r in range(N_SUB):
            acc, l = result(r)
            o = acc / l
            o_ref[_sub_rows(r), cols] = jnp.where(lo_sub, o[:SUB], o[SUB:]).astype(BF16)


def _dsa_prompt(p, bias_b, tri, b, t, topk):
    nq = t // TB
    return pl.pallas_call(
        functools.partial(_dsa_kernel, topk=topk),
        grid=(b, nq),
        in_specs=[pl.BlockSpec((TB, I_WIDTH), lambda bi, i: (bi * nq + i, 0)),
                  pl.BlockSpec((t, LANES), lambda bi, i: (bi, 0)),
                  pl.BlockSpec((TB, LANES), lambda bi, i: (bi * nq + i, 0)),
                  pl.BlockSpec((TB, B_WIDTH), lambda bi, i: (bi * nq + i, 0)),
                  pl.BlockSpec((t, B_WIDTH), lambda bi, i: (bi, 0)),
                  pl.BlockSpec((t, B_WIDTH), lambda bi, i: (bi, 1)),
                  pl.BlockSpec((B_HEADS, 3, TB, TB), lambda bi, i: (0, 0, 0, 0)),
                  pl.BlockSpec((TB, TB), lambda bi, i: (0, 0))],
        out_specs=pl.BlockSpec((TB, B_WIDTH), lambda bi, i: (bi * nq + i, 0)),
        out_shape=jax.ShapeDtypeStruct((b * t, B_WIDTH), BF16),
        scratch_shapes=[pltpu.VMEM((nq, TB, TB), F32), pltpu.VMEM((nq, N_SUB, 2 * SUB, TB), F32),
                        pltpu.VMEM((3, N_SUB, 2 * SUB, LANES), F32)],
        compiler_params=_cparams(("arbitrary", "arbitrary")),
        name="dsa_prompt",
    )(p["qi"], p["kk"], p["wi"], p["qb"], p["kvb"], p["kvb"], bias_b, tri)


def _sample_idx_kernel(pt_ref, q_ref, w_ref, kn_ref, *rest, n_pages):
    page_refs, o_ref = rest[:n_pages], rest[n_pages]
    q = q_ref[0]
    w = w_ref[0]
    nq = q.shape[0] // IDX_HEADS

    def combine(qk):
        rel = jnp.maximum(qk, 0.0) * w
        sc = rel[0:nq]
        for h in range(1, IDX_HEADS):
            sc = sc + rel[h * nq:(h + 1) * nq]
        return sc

    for k in range(n_pages):
        o_ref[k] = combine(_dot(q, page_refs[k][0, 0].astype(BF16)))
    new = jnp.concatenate([kn_ref[0], jnp.zeros((PAGE_ROWS - nq, IDX_DIM), F32)], axis=0).astype(BF16)
    row = lax.broadcasted_iota(jnp.int32, (nq, PAGE_ROWS), 0)
    col = lax.broadcasted_iota(jnp.int32, (nq, PAGE_ROWS), 1)
    o_ref[n_pages] = jnp.where(col <= row, combine(_dot_nt(q, new)), NEG_INF)


def _sample_idx(page_table, q_stack, w_stack, ki_new, cache_kidx_t, layer):
    s, n_pages = page_table.shape
    nq = ki_new.shape[1]
    hq = q_stack.shape[1]
    page_spec = lambda k: pl.BlockSpec((1, 1, IDX_DIM, PAGE_ROWS), lambda i, pt, k=k: (layer, pt[i, k], 0, 0))
    grid_spec = pltpu.PrefetchScalarGridSpec(
        num_scalar_prefetch=1,
        grid=(s,),
        in_specs=[pl.BlockSpec((1, hq, IDX_DIM), lambda i, pt: (i, 0, 0)),
                  pl.BlockSpec((1, hq, 1), lambda i, pt: (i, 0, 0)),
                  pl.BlockSpec((1, nq, IDX_DIM), lambda i, pt: (i, 0, 0))]
                 + [page_spec(k) for k in range(n_pages)],
        out_specs=pl.BlockSpec((n_pages + 1, nq, PAGE_ROWS), lambda i, pt: (0, i, 0)),
    )
    return pl.pallas_call(
        functools.partial(_sample_idx_kernel, n_pages=n_pages),
        grid_spec=grid_spec,
        out_shape=jax.ShapeDtypeStruct((n_pages + 1, s * nq, PAGE_ROWS), F32),
        compiler_params=_cparams(("arbitrary",)),
        name="sample_idx",
    )(page_table, q_stack, w_stack, ki_new, *([cache_kidx_t] * n_pages))


def _select_kernel(sc_ref, tri_ref, o_ref, *, topk):
    o_ref[...] = sc_ref[...]
    _topk_select(o_ref, o_ref.shape[0], topk, tri_ref)


def _sample_select(scores, tri, topk):
    nt, rows, tw = scores.shape
    tr = min(rows, 256)
    return pl.pallas_call(
        functools.partial(_select_kernel, topk=topk),
        grid=(rows // tr,),
        in_specs=[pl.BlockSpec((nt, tr, tw), lambda i: (0, i, 0)),
                  pl.BlockSpec((tw, tw), lambda i: (0, 0))],
        out_specs=pl.BlockSpec((nt, tr, tw), lambda i: (0, i, 0)),
        out_shape=jax.ShapeDtypeStruct(scores.shape, F32),
        compiler_params=_cparams(("arbitrary",)),
        name="sample_select",
    )(scores, tri)


def _sample_attn_kernel(pt_ref, lam_ref, g_ref, qa_ref, qb_ref, mska_ref, mskb_ref, biasa_ref, biasb_ref,
                        kvan_ref, kvbn_ref, selp_ref, seln_ref, *rest, n_chunks, lam_init):
    g_pages = PAGES_PER_STEP
    kva_refs, kvb_refs = rest[:g_pages], rest[g_pages:2 * g_pages]
    oa_ref, ob_ref = rest[2 * g_pages], rest[2 * g_pages + 1]
    qa_s, qb_s, ma_s, la_s, acca_s, mb_s, lb_s, accb_s = rest[2 * g_pages + 2:]
    c = pl.program_id(1)
    nq = qa_ref.shape[1]
    rows_a = A_HEADS * 2 * nq
    rows_b = B_HEADS * nq

    @pl.when(c == 0)
    def _():
        qa_s[...] = (jnp.concatenate([qa_ref[0]] * (2 * A_HEADS), axis=0) * mska_ref[...]).astype(BF16)
        qb_s[...] = (jnp.concatenate([qb_ref[0]] * B_HEADS, axis=0) * mskb_ref[...]).astype(BF16)
        ma_s[...] = jnp.full(ma_s.shape, NEG_INF, F32)
        mb_s[...] = jnp.full(mb_s.shape, NEG_INF, F32)
        la_s[...] = jnp.zeros(la_s.shape, F32)
        lb_s[...] = jnp.zeros(lb_s.shape, F32)
        acca_s[...] = jnp.zeros(acca_s.shape, F32)
        accb_s[...] = jnp.zeros(accb_s.shape, F32)

    def diag_a(r):
        return jnp.concatenate(
            [r[h * 2 * nq:(h + 1) * 2 * nq, h * LANES:(h + 1) * LANES] for h in range(A_HEADS)], axis=0)

    def diag_b(r):
        return jnp.concatenate(
            [r[m * 2 * nq:(m + 1) * 2 * nq, m * LANES:(m + 1) * LANES] for m in range(B_HEADS // 2)], axis=0)

    def update_a(s, values):
        mn = jnp.maximum(ma_s[...], jnp.max(s, axis=-1, keepdims=True))
        p = jnp.exp(s - mn)
        al = jnp.exp(ma_s[...] - mn)
        la_s[...] = al * la_s[...] + jnp.sum(p, axis=-1, keepdims=True)
        acca_s[...] = al * acca_s[...] + diag_a(values(p.astype(BF16)))
        ma_s[...] = mn

    def update_b(s, values):
        mn = jnp.maximum(mb_s[...], jnp.max(s, axis=-1, keepdims=True))
        ms = jnp.where(mn == NEG_INF, 0.0, mn)
        p = jnp.exp(s - ms)
        al = jnp.exp(mb_s[...] - ms)
        lb_s[...] = al * lb_s[...] + jnp.sum(p, axis=-1, keepdims=True)
        accb_s[...] = al * accb_s[...] + diag_b(values(p.astype(BF16)))
        mb_s[...] = mn

    def tile_rows(x, n):
        return jnp.concatenate([x] * n, axis=0)

    def gather_a(first):
        return jnp.concatenate(
            [jnp.concatenate([ref[0, 0, pl.ds(first + h, PAGE_ROWS, stride=2 * A_HEADS), :]
                              for h in range(A_HEADS)], axis=1) for ref in kva_refs], axis=0).astype(BF16)

    last = c == n_chunks - 1
    far = (g_pages - 1) * PAGE_ROWS
    bias_a = jnp.concatenate([jnp.zeros((rows_a, far), F32), jnp.where(last, biasa_ref[:, :PAGE_ROWS], 0.0)], axis=1)
    bias_b = jnp.concatenate([jnp.zeros((rows_b, far), F32), jnp.where(last, biasb_ref[:, :PAGE_ROWS], 0.0)], axis=1)

    ka = gather_a(0)
    va = gather_a(A_HEADS)
    update_a(_dot_nt(qa_s[...], ka) + bias_a, lambda p: _dot(p, va))

    kt = jnp.concatenate([ref[0, 0, 0].reshape(B_WIDTH, PAGE_ROWS) for ref in kvb_refs], axis=1).astype(BF16)
    vt = jnp.concatenate([ref[0, 0, 1].reshape(B_WIDTH, PAGE_ROWS) for ref in kvb_refs], axis=1).astype(BF16)
    sel = jnp.concatenate([tile_rows(selp_ref[g], B_HEADS) for g in range(g_pages)], axis=1)
    update_b(_dot(qb_s[...], kt) + bias_b + sel, lambda p: _dot_nt(p, vt))

    @pl.when(last)
    def _():
        pad = lambda x: jnp.concatenate([x, jnp.zeros((PAGE_ROWS - nq, x.shape[1]), F32)], axis=0).astype(BF16)
        kn, vn = pad(kvan_ref[0, :, :A_WIDTH]), pad(kvan_ref[0, :, A_WIDTH:])
        update_a(_dot_nt(qa_s[...], kn) + biasa_ref[:, PAGE_ROWS:], lambda p: _dot(p, vn))
        kn, vn = pad(kvbn_ref[0, :, :B_WIDTH]), pad(kvbn_ref[0, :, B_WIDTH:])
        update_b(_dot_nt(qb_s[...], kn) + biasb_ref[:, PAGE_ROWS:] + tile_rows(seln_ref[0], B_HEADS),
                 lambda p: _dot(p, vn))

        lam = _lambda_value(lam_ref, lam_init)
        oa = acca_s[...] / la_s[...]
        for h in range(A_HEADS):
            o = oa[h * 2 * nq:h * 2 * nq + nq] - lam * oa[h * 2 * nq + nq:(h + 1) * 2 * nq]
            oa_ref[0, :, h * LANES:(h + 1) * LANES] = _sub_layer_norm(o, g_ref[...], lam_init)
        ob = accb_s[...] / lb_s[...]
        lo = lax.broadcasted_iota(jnp.int32, (nq, LANES), 1) < HEAD_DIM
        for m in range(B_HEADS // 2):
            ob_ref[0, :, m * LANES:(m + 1) * LANES] = jnp.where(
                lo, ob[2 * m * nq:(2 * m + 1) * nq], ob[(2 * m + 1) * nq:(2 * m + 2) * nq])


def _sample_attn(page_table, lam4, g_sub, qa, qb, mask_a, mask_b, bias_a, bias_b, kva_new, kvb_new, sel,
                 cache_a, cache_bt, layer, lam_init):
    s, n_pages = page_table.shape
    nq = qa.shape[1]
    g_pages = PAGES_PER_STEP
    n_chunks = n_pages // g_pages
    rows_a, rows_b = A_HEADS * 2 * nq, B_HEADS * nq
    const = lambda shape: pl.BlockSpec(shape, lambda i, c, pt: (0,) * len(shape))
    seq = lambda shape: pl.BlockSpec(shape, lambda i, c, pt: (i,) + (0,) * (len(shape) - 1))
    page_a = lambda g: pl.BlockSpec((1, 1) + cache_a.shape[2:],
                                    lambda i, c, pt, g=g: (layer, pt[i, c * g_pages + g], 0, 0))
    page_b = lambda g: pl.BlockSpec((1, 1) + cache_bt.shape[2:],
                                    lambda i, c, pt, g=g: (layer, pt[i, c * g_pages + g], 0, 0, 0, 0))
    grid_spec = pltpu.PrefetchScalarGridSpec(
        num_scalar_prefetch=1,
        grid=(s, n_chunks),
        in_specs=[const((4, HEAD_DIM)), const((1, 2 * HEAD_DIM)),
                  seq((1, nq, A_WIDTH)), seq((1, nq, B_WIDTH)),
                  const((rows_a, A_WIDTH)), const((rows_b, B_WIDTH)),
                  const((rows_a, 2 * PAGE_ROWS)), const((rows_b, 2 * PAGE_ROWS)),
                  seq((1, nq, 2 * A_WIDTH)), seq((1, nq, 2 * B_WIDTH)),
                  pl.BlockSpec((g_pages, nq, PAGE_ROWS), lambda i, c, pt: (c, i, 0)),
                  pl.BlockSpec((1, nq, PAGE_ROWS), lambda i, c, pt: (n_pages, i, 0))]
                 + [page_a(g) for g in range(g_pages)]
                 + [page_b(g) for g in range(g_pages)],
        out_specs=[seq((1, nq, A_WIDTH)), seq((1, nq, B_WIDTH))],
        scratch_shapes=[pltpu.VMEM((rows_a, A_WIDTH), BF16), pltpu.VMEM((rows_b, B_WIDTH), BF16),
                        pltpu.VMEM((rows_a, 1), F32), pltpu.VMEM((rows_a, 1), F32),
                        pltpu.VMEM((rows_a, LANES), F32),
                        pltpu.VMEM((rows_b, 1), F32), pltpu.VMEM((rows_b, 1), F32),
                        pltpu.VMEM((rows_b, LANES), F32)],
    )
    return pl.pallas_call(
        functools.partial(_sample_attn_kernel, n_chunks=n_chunks, lam_init=lam_init),
        grid_spec=grid_spec,
        out_shape=[jax.ShapeDtypeStruct((s, nq, A_WIDTH), F32), jax.ShapeDtypeStruct((s, nq, B_WIDTH), F32)],
        compiler_params=_cparams(("arbitrary", "arbitrary")),
        name="sample_attn",
    )(page_table, lam4, g_sub, qa, qb, mask_a, mask_b, bias_a, bias_b, kva_new, kvb_new, sel, sel,
      *([cache_a] * g_pages), *([cache_bt] * g_pages))


def _outproj_kernel(oa_ref, ob_ref, sg_ref, x_ref, g1_ref, sh2_ref, sc2_ref, wba_ref, wbb_ref, wout_ref,
                    lng_ref, lnb_ref, wr_ref, br_ref, x1_ref, h2_ref, comb_ref, *, alpha):
    ya = _dot(oa_ref[...], wba_ref[...])
    yb = _dot(ob_ref[...], wbb_ref[...])
    t = sg_ref[:, :D_MODEL] * ya + sg_ref[:, D_MODEL:] * yb
    mix = _dot(t.astype(BF16), wout_ref[...])
    x1 = _layer_norm(alpha * x_ref[...] + g1_ref[0] * mix, lng_ref[...], lnb_ref[...])
    x1_ref[...] = x1
    h2 = x1 * (1.0 + sc2_ref[0]) + sh2_ref[0]
    h2_ref[...] = h2.astype(BF16)

    logits = _dot3(h2, wr_ref[...]) + br_ref[...]
    lane = lax.broadcasted_iota(jnp.int32, logits.shape, 1).astype(F32)
    big = float(LANES)
    is_group = jnp.logical_and(lane >= N_EXPERTS, lane < N_EXPERTS + N_GROUPS)
    lg = jnp.where(is_group, logits, NEG_INF)
    mg = jnp.max(lg, axis=-1, keepdims=True)
    g_sel = jnp.min(jnp.where(lg == mg, lane, big), axis=-1, keepdims=True) - N_EXPERTS
    p_g = 1.0 / jnp.sum(jnp.exp(lg - mg), axis=-1, keepdims=True)
    first = g_sel * EXPERTS_PER_GROUP
    in_group = jnp.logical_and(lane >= first, lane < first + EXPERTS_PER_GROUP)
    le = jnp.where(in_group, logits, NEG_INF)
    ex = jnp.exp(le - jnp.max(le, axis=-1, keepdims=True))
    pe = jnp.where(in_group, ex / jnp.sum(ex, axis=-1, keepdims=True), -1.0)
    v1 = jnp.max(pe, axis=-1, keepdims=True)
    i1 = jnp.min(jnp.where(pe == v1, lane, big), axis=-1, keepdims=True)
    pe2 = jnp.where(lane == i1, -1.0, pe)
    v2 = jnp.max(pe2, axis=-1, keepdims=True)
    i2 = jnp.min(jnp.where(pe2 == v2, lane, big), axis=-1, keepdims=True)
    tot = v1 + v2
    comb_ref[...] = (jnp.where(lane == i1, p_g * (v1 / tot), 0.0)
                     + jnp.where(lane == i2, p_g * (v2 / tot), 0.0))


def _out_proj(oa, ob, sg, x, ada3, wba, wbb, wout, lng, lnb, wr, br, rows_per_batch, alpha):
    n = x.shape[0]
    tm = ROW_TILE
    row = lambda w: pl.BlockSpec((tm, w), lambda i: (i, 0))
    full = lambda a: pl.BlockSpec(a.shape, lambda i: (0,) * a.ndim)
    return pl.pallas_call(
        functools.partial(_outproj_kernel, alpha=alpha),
        grid=(n // tm,),
        in_specs=[row(A_WIDTH), row(B_WIDTH), row(2 * D_MODEL), row(D_MODEL),
                  _row_vec_spec(ada3, 2, tm, rows_per_batch),
                  _row_vec_spec(ada3, 3, tm, rows_per_batch),
                  _row_vec_spec(ada3, 4, tm, rows_per_batch),
                  full(wba), full(wbb), full(wout), full(lng), full(lnb), full(wr), full(br)],
        out_specs=[row(D_MODEL), row(D_MODEL), row(LANES)],
        out_shape=[jax.ShapeDtypeStruct((n, D_MODEL), F32), jax.ShapeDtypeStruct((n, D_MODEL), BF16),
                   jax.ShapeDtypeStruct((n, LANES), F32)],
        compiler_params=_cparams(("arbitrary",)),
        name="out_proj",
    )(oa, ob, sg, x, ada3, ada3, ada3, wba, wbb, wout, lng, lnb, wr, br)


def _moe_kernel(h_ref, x1_ref, comb_ref, g2_ref, wup_ref, wdn_ref, lng_ref, lnb_ref, o_ref, acc_ref, *, alpha):
    e = pl.program_id(1)

    @pl.when(e == 0)
    def _():
        acc_ref[...] = jnp.zeros(acc_ref.shape, F32)

    hid = _dot(h_ref[...], wup_ref[0])
    gate, up = hid[:, :EXPERT_HIDDEN], hid[:, EXPERT_HIDDEN:]
    act = (gate * _sigmoid(gate) * up).astype(BF16)
    y = _dot(act, wdn_ref[0])
    comb = comb_ref[...]
    lane = lax.broadcasted_iota(jnp.int32, comb.shape, 1)
    w = jnp.sum(jnp.where(lane == e, comb, 0.0), axis=-1, keepdims=True)
    acc_ref[...] += w * y

    @pl.when(e == N_EXPERTS - 1)
    def _():
        u = alpha * x1_ref[...] + g2_ref[0] * acc_ref[...]
        o_ref[...] = _layer_norm(u, lng_ref[...], lnb_ref[...])


def _moe(h2, x1, comb, ada3, wup, wdn, lng, lnb, rows_per_batch, alpha):
    n = x1.shape[0]
    tm = min(MOE_TILE, n)
    if ada3.shape[1] != 1:
        ada3 = ada3.reshape(n // tm, tm, ada3.shape[2])
    row = lambda w: pl.BlockSpec((tm, w), lambda i, e: (i, 0))
    full = lambda a: pl.BlockSpec(a.shape, lambda i, e: (0,) * a.ndim)
    return pl.pallas_call(
        functools.partial(_moe_kernel, alpha=alpha),
        grid=(n // tm, N_EXPERTS),
        in_specs=[row(D_MODEL), row(D_MODEL), row(LANES),
                  _row_vec_spec(ada3, 5, tm, rows_per_batch),
                  pl.BlockSpec((1, D_MODEL, 2 * EXPERT_HIDDEN), lambda i, e: (e, 0, 0)),
                  pl.BlockSpec((1, EXPERT_HIDDEN, D_MODEL), lambda i, e: (e, 0, 0)),
                  full(lng), full(lnb)],
        out_specs=row(D_MODEL),
        out_shape=jax.ShapeDtypeStruct((n, D_MODEL), F32),
        scratch_shapes=[pltpu.VMEM((tm, D_MODEL), F32)],
        compiler_params=_cparams(("arbitrary", "arbitrary")),
        name="moe",
    )(h2, x1, comb, ada3, wup, wdn, lng, lnb)


def _prompt_dist():
    i = jnp.arange(TB, dtype=jnp.int32)[:, None]
    j = jnp.arange(TB, dtype=jnp.int32)[None, :]
    return jnp.concatenate([d * TB + i - j for d in range(3)], axis=0)


def _sample_dist(nq, past_len):
    i = jnp.arange(nq, dtype=jnp.int32)[:, None]
    j = jnp.arange(PAGE_ROWS, dtype=jnp.int32)[None, :]
    last_page = past_len + i - (past_len - PAGE_ROWS + j)
    new = jnp.where(j < nq, i - j, -1)
    return jnp.concatenate([last_page, new], axis=1)


def kernel(x_prompt, x_sample, cache_kv_diff, cache_kv_dsa, cache_kidx, page_table, c_prompt, c_sample,
           rel_bias, w_ada, b_ada, w_in, lambda_q1, lambda_k1, lambda_q2, lambda_k2, subln_g, w_branch_a,
           w_branch_b, w_out, ln1_g, ln1_b, w_router_group, b_router_group, w_router_expert,
           b_router_expert, w_up, w_down, ln2_g, ln2_b):
    b, t, d = x_prompt.shape
    s, nq, _ = x_sample.shape
    depth = w_in.shape[0]
    n_pool = cache_kidx.shape[1]
    cache_a = cache_kv_diff.reshape(depth, n_pool, PAGE_ROWS * 2 * A_HEADS, 2 * HEAD_DIM)
    cache_bt = jnp.transpose(cache_kv_dsa, (0, 1, 3, 4, 5, 2))
    cache_it = jnp.transpose(cache_kidx, (0, 1, 3, 2))
    n_pages = page_table.shape[1]
    past_len = n_pages * PAGE_ROWS
    alpha = (2 * depth) ** 0.25
    topk_p = min(DSA_TOPK_MAX, t // 4)
    topk_s = min(DSA_TOPK_MAX, (past_len + nq) // 4)
    assert d == D_MODEL and t % TB == 0 and (s * nq) % ROW_TILE == 0 and n_pages % PAGES_PER_STEP == 0
    assert cache_kidx.shape[2] == PAGE_ROWS and nq <= 8

    bias_p = _bias_tiles(rel_bias, _prompt_dist()).reshape(A_HEADS + B_HEADS, 3, TB, TB)
    bias_s = _bias_tiles(rel_bias, _sample_dist(nq, past_len))
    bias_sa = jnp.broadcast_to(bias_s[:A_HEADS, None], (A_HEADS, 2, nq, 2 * PAGE_ROWS)).reshape(
        A_HEADS * 2 * nq, 2 * PAGE_ROWS)
    bias_sb = bias_s[A_HEADS:].reshape(B_HEADS * nq, 2 * PAGE_ROWS)
    tri_p = (jnp.arange(TB)[:, None] <= jnp.arange(TB)[None, :]).astype(BF16)
    tri_s = tri_p[:PAGE_ROWS, :PAGE_ROWS]
    lane_a = jnp.arange(A_WIDTH)[None, :] // HEAD_DIM
    mask_a = (lane_a == (jnp.arange(A_HEADS * 2 * nq)[:, None] // nq)).astype(F32)
    lane_b = jnp.arange(B_WIDTH)[None, :] // HEAD_DIM
    mask_b = (lane_b == (jnp.arange(B_HEADS * nq)[:, None] // nq)).astype(F32)

    xp = x_prompt.reshape(b * t, d)
    xs = x_sample.reshape(s * nq, d)
    c_all = jnp.concatenate([c_prompt, c_sample], axis=0)
    c_all = jnp.pad(c_all, ((0, -(b + s) % 8), (0, 0)))
    outs = [[] for _ in range(6)]
    for l in range(depth):
        lam_init = 0.8 - 0.6 * math.exp(-0.3 * l)
        lam4 = jnp.stack([lambda_q1[l], lambda_k1[l], lambda_q2[l], lambda_k2[l]]).astype(F32)
        g_sub = subln_g[l].reshape(1, 2 * HEAD_DIM)
        w = w_in[l]
        w2 = jnp.concatenate([w[:, :C_KI + IDX_DIM], w[:, C_KI:C_KI + IDX_DIM],
                              w[:, C_KI + IDX_DIM:C_KI + IDX_DIM + IDX_HEADS],
                              jnp.zeros((d, LANES - IDX_HEADS), w.dtype),
                              w[:, C_KI + IDX_DIM + IDX_HEADS:]], axis=1).astype(BF16)
        wba, wbb, wout = w_branch_a[l].astype(BF16), w_branch_b[l].astype(BF16), w_out[l].astype(BF16)
        wup, wdn = w_up[l].astype(BF16), w_down[l].astype(BF16)
        wr = jnp.concatenate([w_router_expert[l], w_router_group[l],
                              jnp.zeros((d, LANES - N_EXPERTS - N_GROUPS), F32)], axis=1)
        br = jnp.concatenate([b_router_expert[l], b_router_group[l],
                              jnp.zeros((LANES - N_EXPERTS - N_GROUPS,), F32)]).reshape(1, LANES)
        ln1 = (ln1_g[l].reshape(1, d), ln1_b[l].reshape(1, d))
        ln2 = (ln2_g[l].reshape(1, d), ln2_b[l].reshape(1, d))

        ada = _ada(c_all, w_ada[l], b_ada[l])
        ada_p = ada[:b].reshape(b, 1, 6 * d)
        ada_s = jnp.broadcast_to(ada[b:b + s, None], (s, nq, 6 * d)).reshape(s * nq // ROW_TILE, ROW_TILE, 6 * d)

        p = _in_proj(xp, ada_p, w2, t)
        oa = _diff_attn_prompt(lam4, g_sub, p["qa"], p["kva"], bias_p[:A_HEADS], b, t, lam_init)
        ob = _dsa_prompt(p, bias_p[A_HEADS:], tri_p, b, t, topk_p)
        x1, h2, comb = _out_proj(oa, ob, p["sg"], xp, ada_p, wba, wbb, wout, *ln1, wr, br, t, alpha)
        xp = _moe(h2, x1, comb, ada_p, wup, wdn, *ln2, t, alpha)
        outs[0].append(p["ra"].reshape(b, t, 2, A_HEADS, 2 * HEAD_DIM))
        outs[1].append(p["rb"].reshape(b, t, 2, B_HEADS, HEAD_DIM))
        outs[2].append(p["ki"].reshape(b, t, IDX_DIM))

        q = _in_proj(xs, ada_s, w2, nq)
        q_stack = q["qi"].reshape(s, nq, IDX_HEADS, IDX_DIM).transpose(0, 2, 1, 3).reshape(
            s, IDX_HEADS * nq, IDX_DIM)
        w_stack = q["wi"][:, :IDX_HEADS].reshape(s, nq, IDX_HEADS).transpose(0, 2, 1).reshape(
            s, IDX_HEADS * nq, 1)
        scores = _sample_idx(page_table, q_stack, w_stack, q["ki"].reshape(s, nq, IDX_DIM), cache_it, l)
        sel = _sample_select(scores, tri_s, topk_s)
        oa_s, ob_s = _sample_attn(
            page_table, lam4, g_sub,
            q["qa"].astype(F32).reshape(s, nq, A_WIDTH), q["qb"].astype(F32).reshape(s, nq, B_WIDTH),
            mask_a, mask_b, bias_sa, bias_sb,
            q["ra"].reshape(s, nq, 2 * A_WIDTH), q["rb"].reshape(s, nq, 2 * B_WIDTH), sel,
            cache_a, cache_bt, l, lam_init)
        x1, h2, comb = _out_proj(oa_s.reshape(s * nq, A_WIDTH).astype(BF16),
                                 ob_s.reshape(s * nq, B_WIDTH).astype(BF16),
                                 q["sg"], xs, ada_s, wba, wbb, wout, *ln1, wr, br, nq, alpha)
        xs = _moe(h2, x1, comb, ada_s, wup, wdn, *ln2, nq, alpha)
        outs[3].append(q["ra"].reshape(s, nq, 2, A_HEADS, 2 * HEAD_DIM))
        outs[4].append(q["rb"].reshape(s, nq, 2, B_HEADS, HEAD_DIM))
        outs[5].append(q["ki"].reshape(s, nq, IDX_DIM))

    return (xp.reshape(b, t, d), xs.reshape(s, nq, d)) + tuple(jnp.stack(o, 0) for o in outs)
```

```python
import functools
import math

import jax
import jax.numpy as jnp
from jax import lax
from jax.experimental import pallas as pl
from jax.experimental.pallas import tpu as pltpu

D_MODEL = 1024
HEAD_DIM = 64
A_HEADS = 8
B_HEADS = 8
IDX_HEADS = 8
IDX_DIM = 64
DSA_TOPK_MAX = 256
N_BUCKETS = 32
MAX_DISTANCE = 128
N_GROUPS = 4
EXPERTS_PER_GROUP = 4
N_EXPERTS = N_GROUPS * EXPERTS_PER_GROUP
EXPERT_HIDDEN = 512
LN_EPS = 1e-5

LANES = 128
TB = 256
ROW_TILE = 256
MOE_TILE = 512
PAGE_ROWS = 128
PAGES_PER_STEP = 8
VMEM_LIMIT = 56 * 1024 * 1024

A_WIDTH = A_HEADS * 2 * HEAD_DIM
B_WIDTH = B_HEADS * HEAD_DIM
I_WIDTH = IDX_HEADS * IDX_DIM
C_QA, C_KVA, C_QB, C_KVB, C_QI, C_KI, C_WI, C_G = 0, 1024, 3072, 3584, 4608, 5120, 5248, 5376
W2_WIDTH = C_G + 2 * D_MODEL

F32 = jnp.float32
BF16 = jnp.bfloat16
NEG_INF = float("-inf")
INT_MIN = -2 ** 31


def _cparams(sem):
    return pltpu.CompilerParams(dimension_semantics=sem, vmem_limit_bytes=VMEM_LIMIT)


def _dot(a, b):
    return jnp.dot(a, b, preferred_element_type=F32)


def _dot_nt(a, b):
    return lax.dot_general(a, b, (((1,), (1,)), ((), ())), preferred_element_type=F32)


def _split(a):
    hi = a.astype(BF16)
    lo = (a - hi.astype(F32)).astype(BF16)
    return hi, lo


def _dot3(a, b):
    a_hi, a_lo = _split(a)
    b_hi, b_lo = _split(b)
    return _dot(a_hi, b_hi) + _dot(a_lo, b_hi) + _dot(a_hi, b_lo)


def _sigmoid(x):
    return 1.0 / (1.0 + jnp.exp(-x))


def _layer_norm(u, g, b):
    mu = jnp.mean(u, axis=-1, keepdims=True)
    d = u - mu
    var = jnp.mean(d * d, axis=-1, keepdims=True)
    return d * lax.rsqrt(var + LN_EPS) * g + b


def _bias_kernel(tab_ref, dist_ref, out_ref):
    h = pl.program_id(0)
    d = dist_ref[...]
    n = jnp.maximum(d, 0)
    max_exact = N_BUCKETS // 2
    nf = jnp.maximum(n, 1).astype(F32)
    large = max_exact + (jnp.log(nf / max_exact) / math.log(MAX_DISTANCE / max_exact)
                         * (N_BUCKETS - max_exact)).astype(jnp.int32)
    large = jnp.minimum(large, N_BUCKETS - 1)
    bucket = jnp.where(n < max_exact, n, large)
    last = tab_ref[N_BUCKETS - 1, h]
    acc = jnp.zeros(d.shape, F32)
    for m in range(N_BUCKETS - 1):
        acc = jnp.where(bucket == m, tab_ref[m, h] - last, acc)
    out_ref[0] = jnp.where(d < 0, NEG_INF, acc)


def _bias_tiles(rel_bias, dist):
    n_heads = rel_bias.shape[1]
    r, c = dist.shape
    return pl.pallas_call(
        _bias_kernel,
        grid=(n_heads,),
        in_specs=[pl.BlockSpec(memory_space=pltpu.SMEM),
                  pl.BlockSpec((r, c), lambda h: (0, 0))],
        out_specs=pl.BlockSpec((1, r, c), lambda h: (h, 0, 0)),
        out_shape=jax.ShapeDtypeStruct((n_heads, r, c), F32),
        compiler_params=_cparams(("arbitrary",)),
        name="bias_tiles",
    )(rel_bias, dist)


def _ada_kernel(c_ref, w_ref, b_ref, o_ref):
    c = c_ref[...]
    o_ref[...] = _dot3(c * _sigmoid(c), w_ref[...]) + b_ref[...]


def _ada(c_all, w_ada, b_ada):
    r, d = c_all.shape
    n = w_ada.shape[1]
    tn = 512
    return pl.pallas_call(
        _ada_kernel,
        grid=(n // tn,),
        in_specs=[pl.BlockSpec((r, d), lambda j: (0, 0)),
                  pl.BlockSpec((d, tn), lambda j: (0, j)),
                  pl.BlockSpec((1, tn), lambda j: (0, j))],
        out_specs=pl.BlockSpec((r, tn), lambda j: (0, j)),
        out_shape=jax.ShapeDtypeStruct((r, n), F32),
        compiler_params=_cparams(("arbitrary",)),
        name="ada",
    )(c_all, w_ada, b_ada.reshape(1, n))


def _inproj_kernel(x_ref, sh_ref, sc_ref, w_ref, qa_ref, ra_ref, kva_ref, qb_ref, rb_ref, kvb_ref,
                   qi_ref, ki_ref, kk_ref, wi_ref, sg_ref):
    h = (x_ref[...] * (1.0 + sc_ref[0]) + sh_ref[0]).astype(BF16)
    q_scale = HEAD_DIM ** -0.5

    def mm(c0, n):
        return _dot(h, w_ref[:, c0:c0 + n])

    for c in range(0, A_WIDTH, 512):
        qa_ref[:, c:c + 512] = (mm(C_QA + c, 512) * q_scale).astype(BF16)
    for c in range(0, 2 * A_WIDTH, 512):
        a = mm(C_KVA + c, 512)
        ra_ref[:, c:c + 512] = a
        kva_ref[:, c:c + 512] = a.astype(BF16)
    qb_ref[...] = (mm(C_QB, 512) * q_scale).astype(BF16)
    for c in range(0, 2 * B_WIDTH, 512):
        a = mm(C_KVB + c, 512)
        rb_ref[:, c:c + 512] = a
        kvb_ref[:, c:c + 512] = a.astype(BF16)
    qi_ref[...] = (mm(C_QI, 512) * q_scale).astype(BF16)
    a = mm(C_KI, 2 * LANES)
    ki_ref[...] = a[:, :IDX_DIM]
    kk_ref[...] = a[:, :LANES].astype(BF16)
    wi_ref[...] = a[:, LANES:] * IDX_HEADS ** -0.5
    for c in range(0, 2 * D_MODEL, 512):
        sg_ref[:, c:c + 512] = _sigmoid(mm(C_G + c, 512))


def _row_vec_spec(arr, col, tm, rows_per_batch):
    if arr.shape[1] == 1:
        per = rows_per_batch // tm
        return pl.BlockSpec((1, 1, D_MODEL), lambda i, *_: (i // per, 0, col))
    return pl.BlockSpec((1, tm, D_MODEL), lambda i, *_: (i, 0, col))


def _in_proj(x, ada3, w2, rows_per_batch):
    n = x.shape[0]
    tm = ROW_TILE
    row = lambda w: pl.BlockSpec((tm, w), lambda i: (i, 0))
    outs = [("qa", A_WIDTH, BF16), ("ra", 2 * A_WIDTH, F32), ("kva", 2 * A_WIDTH, BF16),
            ("qb", B_WIDTH, BF16), ("rb", 2 * B_WIDTH, F32), ("kvb", 2 * B_WIDTH, BF16),
            ("qi", I_WIDTH, BF16), ("ki", IDX_DIM, F32), ("kk", LANES, BF16), ("wi", LANES, F32),
            ("sg", 2 * D_MODEL, F32)]
    res = pl.pallas_call(
        _inproj_kernel,
        grid=(n // tm,),
        in_specs=[row(D_MODEL),
                  _row_vec_spec(ada3, 0, tm, rows_per_batch),
                  _row_vec_spec(ada3, 1, tm, rows_per_batch),
                  pl.BlockSpec((D_MODEL, W2_WIDTH), lambda i: (0, 0), pipeline_mode=pl.Buffered(1))],
        out_specs=[row(w) for _, w, _ in outs],
        out_shape=[jax.ShapeDtypeStruct((n, w), dt) for _, w, dt in outs],
        compiler_params=_cparams(("arbitrary",)),
        name="in_proj",
    )(x, ada3, ada3, w2)
    return {name: r for (name, _, _), r in zip(outs, res)}


def _stack2(x):
    return jnp.concatenate([x, x], axis=0)


LOG2_E = 1.4426950408889634
N_CHAINS = 4
N_BIAS_TILES = 4


def _bias_index(tile_distance):
    return jnp.clip(tile_distance, -1, N_BIAS_TILES - 2) + 1


def _lane_fold(x, op):
    r = x[:, :LANES]
    for c in range(1, x.shape[1] // LANES):
        r = op(r, x[:, c * LANES:(c + 1) * LANES])
    return r


def _two_pass_attend(q_ref, k_ref, v_ref, nt, near_bias, every_bias, s_ref, st_ref):
    rows = 2 * TB
    st_ref[0] = jnp.full((N_CHAINS, rows, LANES), NEG_INF, F32)
    st_ref[1] = jnp.zeros((N_CHAINS, rows, LANES), F32)
    st_ref[2] = jnp.zeros((N_CHAINS, rows, LANES), F32)

    def tiles(jj):
        return [(2 * jj + u, jnp.minimum(2 * jj + u, nt - 1)) for u in range(2)]

    def rows_of(ref, c, pair):
        return jnp.concatenate([ref[pl.ds(pl.multiple_of(jc * TB, TB), TB), c * LANES:(c + 1) * LANES]
                                for _, jc in pair], axis=0)

    def first(near):
        def body(jj, _):
            pair = tiles(jj)
            every = None
            if every_bias is not None:
                every = _stack2(jnp.concatenate([every_bias(jc) for _, jc in pair], axis=1))
            for c in range(N_CHAINS):
                s = _dot_nt(_masked_pair(q_ref[:, c * LANES:(c + 1) * LANES]), rows_of(k_ref, c, pair))
                if near:
                    s = s + jnp.concatenate([near_bias(c, j) for j, _ in pair], axis=1)
                if every is not None:
                    s = s + every
                s = s * LOG2_E
                s_ref[jj, c] = s
                st_ref[0, c] = jnp.maximum(st_ref[0, c], _lane_fold(s, jnp.maximum))
            return 0
        return body

    n_steps = (nt + 1) // 2
    n_far = jnp.maximum(nt - 2, 0) // 2
    lax.fori_loop(0, n_far, first(False), 0)
    lax.fori_loop(n_far, n_steps, first(True), 0)
    for c in range(N_CHAINS):
        st_ref[0, c] = jnp.broadcast_to(jnp.max(st_ref[0, c], axis=-1, keepdims=True), (rows, LANES))

    def second(jj, _):
        pair = tiles(jj)
        for c in range(N_CHAINS):
            s = s_ref[jj, c]
            mb = st_ref[0, c]
            p = [jnp.exp2(s[:, i * LANES:(i + 1) * LANES] - mb) for i in range(2 * TB // LANES)]
            st_ref[1, c] += (p[0] + p[1]) + (p[2] + p[3])
            st_ref[2, c] += _dot(jnp.concatenate(p, axis=1).astype(BF16), rows_of(v_ref, c, pair))
        return 0

    lax.fori_loop(0, n_steps, second, 0)
    return lambda c: (st_ref[2, c], jnp.sum(st_ref[1, c], axis=-1, keepdims=True))


def _lambda_value(lam_ref, lam_init):
    a = jnp.sum(lam_ref[0:1, :] * lam_ref[1:2, :], axis=-1, keepdims=True)
    b = jnp.sum(lam_ref[2:3, :] * lam_ref[3:4, :], axis=-1, keepdims=True)
    return jnp.exp(a) - jnp.exp(b) + lam_init


def _sub_layer_norm(o, g, lam_init):
    o = o * lax.rsqrt(jnp.mean(o * o, axis=-1, keepdims=True) + LN_EPS)
    return o * g * (1.0 - lam_init)


def _masked_pair(q):
    lo = lax.broadcasted_iota(jnp.int32, q.shape, 1) < HEAD_DIM
    zero = jnp.zeros_like(q)
    return jnp.concatenate([jnp.where(lo, q, zero), jnp.where(lo, zero, q)], axis=0)


def _attend_scratch(nq):
    return [pltpu.VMEM(((nq + 1) // 2, N_CHAINS, 2 * TB, 2 * TB), F32),
            pltpu.VMEM((3, N_CHAINS, 2 * TB, LANES), F32)]


def _diff_attn_kernel(lam_ref, g_ref, q_ref, k_ref, v_ref, bias_ref, o_ref, s_ref, st_ref, *, lam_init):
    qi = pl.program_id(2)
    near = lambda c, j: _stack2(bias_ref[c, _bias_index(qi - j)])
    result = _two_pass_attend(q_ref, k_ref, v_ref, qi + 1, near, None, s_ref, st_ref)
    lam = _lambda_value(lam_ref, lam_init)
    for c in range(N_CHAINS):
        acc, l = result(c)
        o = acc / l
        o = o[:TB] - lam * o[TB:]
        o_ref[:, c * LANES:(c + 1) * LANES] = _sub_layer_norm(o, g_ref[...], lam_init).astype(BF16)


def _diff_attn_prompt(lam4, g_sub, qa, kva, bias_p, b, t, lam_init):
    nq = t // TB
    groups = A_HEADS // N_CHAINS
    width = N_CHAINS * LANES
    return pl.pallas_call(
        functools.partial(_diff_attn_kernel, lam_init=lam_init),
        grid=(b, groups, nq),
        in_specs=[pl.BlockSpec((4, HEAD_DIM), lambda bi, h, i: (0, 0)),
                  pl.BlockSpec((1, 2 * HEAD_DIM), lambda bi, h, i: (0, 0)),
                  pl.BlockSpec((TB, width), lambda bi, h, i: (bi * nq + i, h)),
                  pl.BlockSpec((t, width), lambda bi, h, i: (bi, h)),
                  pl.BlockSpec((t, width), lambda bi, h, i: (bi, groups + h)),
                  pl.BlockSpec((N_CHAINS, N_BIAS_TILES, TB, TB), lambda bi, h, i: (h, 0, 0, 0))],
        out_specs=pl.BlockSpec((TB, width), lambda bi, h, i: (bi * nq + i, h)),
        out_shape=jax.ShapeDtypeStruct((b * t, A_WIDTH), BF16),
        scratch_shapes=_attend_scratch(nq),
        compiler_params=_cparams(("arbitrary", "arbitrary", "arbitrary")),
        name="diff_attn_prompt",
    )(lam4, g_sub, qa, kva, kva, bias_p)


KEY_NEG_INF = INT_MIN + 0x7FFFFF


def _key_to_float(key):
    bits = jnp.where(key < 0, key ^ jnp.int32(0x7FFFFFFF), key)
    return jnp.where(key < KEY_NEG_INF, NEG_INF, lax.bitcast_convert_type(bits, F32))


def _topk_select(sc_ref, nt, topk, tri_ref):
    rows, tw = sc_ref.shape[1], sc_ref.shape[2]
    kf = float(topk)

    def count_ge(cf):
        def body(j, acc):
            g = jnp.where(sc_ref[j] >= cf, 1.0, 0.0)
            r = g[:, :LANES]
            for c in range(1, tw // LANES):
                r = r + g[:, c * LANES:(c + 1) * LANES]
            return acc + r
        acc = lax.fori_loop(0, nt, body, jnp.zeros((rows, LANES), F32))
        return jnp.sum(acc, axis=-1, keepdims=True)

    def search(p, t):
        cand = t + lax.shift_left(jnp.int32(1), jnp.int32(31) - p)
        c = count_ge(_key_to_float(cand))
        return jnp.where(c >= kf, cand, t)

    t = lax.fori_loop(0, 32, search, jnp.full((rows, 1), INT_MIN, jnp.int32))
    t_lo = _key_to_float(t)
    t_hi = _key_to_float(t + 1)
    need = kf - count_ge(t_hi)
    tri = tri_ref[...]

    def finish(j, c):
        s = sc_ref[j]
        gt = s >= t_hi
        eq = jnp.logical_and(s >= t_lo, jnp.logical_not(gt))
        e = jnp.where(eq, 1.0, 0.0)
        rank = _dot(e.astype(BF16), tri) + c
        sel = jnp.logical_or(gt, jnp.logical_and(eq, rank <= need))
        sel = jnp.logical_and(sel, s > NEG_INF)
        sc_ref[j] = jnp.where(sel, 0.0, NEG_INF)
        return c + jnp.sum(e, axis=-1, keepdims=True)

    lax.fori_loop(0, nt, finish, jnp.zeros((rows, 1), F32))


def _dsa_kernel(qi_ref, kk_ref, wi_ref, qb_ref, kb_ref, vb_ref, bias_ref, tri_ref, o_ref, sc_ref, s_ref,
                st_ref, *, topk):
    qblk = pl.program_id(1)
    nt = qblk + 1
    tq = qi_ref.shape[0]
    lo = lax.broadcasted_iota(jnp.int32, (tq, LANES), 1) < HEAD_DIM
    zero = jnp.zeros((tq, LANES), BF16)

    def halves(qp):
        return jnp.where(lo, qp, zero), jnp.where(lo, zero, qp)

    wi = wi_ref[...]

    def index_tile(j, _):
        off = pl.multiple_of(j * TB, TB)
        kk = kk_ref[pl.ds(off, TB), :]
        acc = jnp.zeros((tq, TB), F32)
        for m in range(IDX_HEADS // 2):
            q_lo, q_hi = halves(qi_ref[:, m * LANES:(m + 1) * LANES])
            acc = acc + wi[:, 2 * m:2 * m + 1] * jnp.maximum(_dot_nt(q_lo, kk), 0.0)
            acc = acc + wi[:, 2 * m + 1:2 * m + 2] * jnp.maximum(_dot_nt(q_hi, kk), 0.0)
        row = lax.broadcasted_iota(jnp.int32, (tq, TB), 0) + qblk * TB
        col = lax.broadcasted_iota(jnp.int32, (tq, TB), 1) + j * TB
        sc_ref[j] = jnp.where(col <= row, acc, NEG_INF)
        return 0

    lax.fori_loop(0, nt, index_tile, 0)
    _topk_select(sc_ref, nt, topk, tri_ref)

    def near(m, j):
        d = _bias_index(qblk - j)
        return jnp.concatenate([bias_ref[2 * m, d], bias_ref[2 * m + 1, d]], axis=0)

    result = _two_pass_attend(qb_ref, kb_ref, vb_ref, nt, near, lambda jc: sc_ref[jc], s_ref, st_ref)
    lo_out = lax.broadcasted_iota(jnp.int32, (tq, LANES), 1) < HEAD_DIM
    for m in range(N_CHAINS):
        acc, l = result(m)
        o = acc / l
        o_ref[:, m * LANES:(m + 1) * LANES] = jnp.where(lo_out, o[:tq], o[tq:]).astype(BF16)


def _dsa_prompt(p, bias_b, tri, b, t, topk):
    nq = t // TB
    return pl.pallas_call(
        functools.partial(_dsa_kernel, topk=topk),
        grid=(b, nq),
        in_specs=[pl.BlockSpec((TB, I_WIDTH), lambda bi, i: (bi * nq + i, 0)),
                  pl.BlockSpec((t, LANES), lambda bi, i: (bi, 0)),
                  pl.BlockSpec((TB, LANES), lambda bi, i: (bi * nq + i, 0)),
                  pl.BlockSpec((TB, B_WIDTH), lambda bi, i: (bi * nq + i, 0)),
                  pl.BlockSpec((t, B_WIDTH), lambda bi, i: (bi, 0)),
                  pl.BlockSpec((t, B_WIDTH), lambda bi, i: (bi, 1)),
                  pl.BlockSpec((B_HEADS, N_BIAS_TILES, TB, TB), lambda bi, i: (0, 0, 0, 0),
                               pipeline_mode=pl.Buffered(1)),
                  pl.BlockSpec((TB, TB), lambda bi, i: (0, 0))],
        out_specs=pl.BlockSpec((TB, B_WIDTH), lambda bi, i: (bi * nq + i, 0)),
        out_shape=jax.ShapeDtypeStruct((b * t, B_WIDTH), BF16),
        scratch_shapes=[pltpu.VMEM((nq, TB, TB), F32)] + _attend_scratch(nq),
        compiler_params=_cparams(("arbitrary", "arbitrary")),
        name="dsa_prompt",
    )(p["qi"], p["kk"], p["wi"], p["qb"], p["kvb"], p["kvb"], bias_b, tri)


def _sample_idx_kernel(pt_ref, q_ref, w_ref, kn_ref, *rest, n_pages):
    page_refs, o_ref = rest[:n_pages], rest[n_pages]
    q = q_ref[0]
    w = w_ref[0]
    nq = q.shape[0] // IDX_HEADS

    def combine(qk):
        rel = jnp.maximum(qk, 0.0) * w
        sc = rel[0:nq]
        for h in range(1, IDX_HEADS):
            sc = sc + rel[h * nq:(h + 1) * nq]
        return sc

    for k in range(n_pages):
        o_ref[k] = combine(_dot(q, page_refs[k][0, 0].astype(BF16)))
    new = jnp.concatenate([kn_ref[0], jnp.zeros((PAGE_ROWS - nq, IDX_DIM), F32)], axis=0).astype(BF16)
    row = lax.broadcasted_iota(jnp.int32, (nq, PAGE_ROWS), 0)
    col = lax.broadcasted_iota(jnp.int32, (nq, PAGE_ROWS), 1)
    o_ref[n_pages] = jnp.where(col <= row, combine(_dot_nt(q, new)), NEG_INF)


def _sample_idx(page_table, q_stack, w_stack, ki_new, cache_kidx_t, layer):
    s, n_pages = page_table.shape
    nq = ki_new.shape[1]
    hq = q_stack.shape[1]
    page_spec = lambda k: pl.BlockSpec((1, 1, IDX_DIM, PAGE_ROWS), lambda i, pt, k=k: (layer, pt[i, k], 0, 0))
    grid_spec = pltpu.PrefetchScalarGridSpec(
        num_scalar_prefetch=1,
        grid=(s,),
        in_specs=[pl.BlockSpec((1, hq, IDX_DIM), lambda i, pt: (i, 0, 0)),
                  pl.BlockSpec((1, hq, 1), lambda i, pt: (i, 0, 0)),
                  pl.BlockSpec((1, nq, IDX_DIM), lambda i, pt: (i, 0, 0))]
                 + [page_spec(k) for k in range(n_pages)],
        out_specs=pl.BlockSpec((n_pages + 1, nq, PAGE_ROWS), lambda i, pt: (0, i, 0)),
    )
    return pl.pallas_call(
        functools.partial(_sample_idx_kernel, n_pages=n_pages),
        grid_spec=grid_spec,
        out_shape=jax.ShapeDtypeStruct((n_pages + 1, s * nq, PAGE_ROWS), F32),
        compiler_params=_cparams(("arbitrary",)),
        name="sample_idx",
    )(page_table, q_stack, w_stack, ki_new, *([cache_kidx_t] * n_pages))


def _select_kernel(sc_ref, tri_ref, o_ref, *, topk):
    o_ref[...] = sc_ref[...]
    _topk_select(o_ref, o_ref.shape[0], topk, tri_ref)


def _sample_select(scores, tri, topk):
    nt, rows, tw = scores.shape
    tr = min(rows, 256)
    return pl.pallas_call(
        functools.partial(_select_kernel, topk=topk),
        grid=(rows // tr,),
        in_specs=[pl.BlockSpec((nt, tr, tw), lambda i: (0, i, 0)),
                  pl.BlockSpec((tw, tw), lambda i: (0, 0))],
        out_specs=pl.BlockSpec((nt, tr, tw), lambda i: (0, i, 0)),
        out_shape=jax.ShapeDtypeStruct(scores.shape, F32),
        compiler_params=_cparams(("arbitrary",)),
        name="sample_select",
    )(scores, tri)


def _sample_attn_kernel(pt_ref, lam_ref, g_ref, qa_ref, qb_ref, mska_ref, mskb_ref, biasa_ref, biasb_ref,
                        kvan_ref, kvbn_ref, selp_ref, seln_ref, *rest, n_chunks, lam_init):
    g_pages = PAGES_PER_STEP
    kva_refs, kvb_refs = rest[:g_pages], rest[g_pages:2 * g_pages]
    oa_ref, ob_ref = rest[2 * g_pages], rest[2 * g_pages + 1]
    qa_s, qb_s, ma_s, la_s, acca_s, mb_s, lb_s, accb_s = rest[2 * g_pages + 2:]
    c = pl.program_id(1)
    nq = qa_ref.shape[1]
    rows_a = A_HEADS * 2 * nq
    rows_b = B_HEADS * nq

    @pl.when(c == 0)
    def _():
        qa_s[...] = (jnp.concatenate([qa_ref[0]] * (2 * A_HEADS), axis=0) * mska_ref[...]).astype(BF16)
        qb_s[...] = (jnp.concatenate([qb_ref[0]] * B_HEADS, axis=0) * mskb_ref[...]).astype(BF16)
        ma_s[...] = jnp.full(ma_s.shape, NEG_INF, F32)
        mb_s[...] = jnp.full(mb_s.shape, NEG_INF, F32)
        la_s[...] = jnp.zeros(la_s.shape, F32)
        lb_s[...] = jnp.zeros(lb_s.shape, F32)
        acca_s[...] = jnp.zeros(acca_s.shape, F32)
        accb_s[...] = jnp.zeros(accb_s.shape, F32)

    def diag_a(r):
        return jnp.concatenate(
            [r[h * 2 * nq:(h + 1) * 2 * nq, h * LANES:(h + 1) * LANES] for h in range(A_HEADS)], axis=0)

    def diag_b(r):
        return jnp.concatenate(
            [r[m * 2 * nq:(m + 1) * 2 * nq, m * LANES:(m + 1) * LANES] for m in range(B_HEADS // 2)], axis=0)

    def update_a(s, values):
        mn = jnp.maximum(ma_s[...], jnp.max(s, axis=-1, keepdims=True))
        p = jnp.exp(s - mn)
        al = jnp.exp(ma_s[...] - mn)
        la_s[...] = al * la_s[...] + jnp.sum(p, axis=-1, keepdims=True)
        acca_s[...] = al * acca_s[...] + diag_a(values(p.astype(BF16)))
        ma_s[...] = mn

    def update_b(s, values):
        mn = jnp.maximum(mb_s[...], jnp.max(s, axis=-1, keepdims=True))
        ms = jnp.where(mn == NEG_INF, 0.0, mn)
        p = jnp.exp(s - ms)
        al = jnp.exp(mb_s[...] - ms)
        lb_s[...] = al * lb_s[...] + jnp.sum(p, axis=-1, keepdims=True)
        accb_s[...] = al * accb_s[...] + diag_b(values(p.astype(BF16)))
        mb_s[...] = mn

    def tile_rows(x, n):
        return jnp.concatenate([x] * n, axis=0)

    def gather_a(first):
        return jnp.concatenate(
            [jnp.concatenate([ref[0, 0, pl.ds(first + h, PAGE_ROWS, stride=2 * A_HEADS), :]
                              for h in range(A_HEADS)], axis=1) for ref in kva_refs], axis=0).astype(BF16)

    last = c == n_chunks - 1
    far = (g_pages - 1) * PAGE_ROWS
    bias_a = jnp.concatenate([jnp.zeros((rows_a, far), F32), jnp.where(last, biasa_ref[:, :PAGE_ROWS], 0.0)], axis=1)
    bias_b = jnp.concatenate([jnp.zeros((rows_b, far), F32), jnp.where(last, biasb_ref[:, :PAGE_ROWS], 0.0)], axis=1)

    ka = gather_a(0)
    va = gather_a(A_HEADS)
    update_a(_dot_nt(qa_s[...], ka) + bias_a, lambda p: _dot(p, va))

    kt = jnp.concatenate([ref[0, 0, 0].reshape(B_WIDTH, PAGE_ROWS) for ref in kvb_refs], axis=1).astype(BF16)
    vt = jnp.concatenate([ref[0, 0, 1].reshape(B_WIDTH, PAGE_ROWS) for ref in kvb_refs], axis=1).astype(BF16)
    sel = jnp.concatenate([tile_rows(selp_ref[g], B_HEADS) for g in range(g_pages)], axis=1)
    update_b(_dot(qb_s[...], kt) + bias_b + sel, lambda p: _dot_nt(p, vt))

    @pl.when(last)
    def _():
        pad = lambda x: jnp.concatenate([x, jnp.zeros((PAGE_ROWS - nq, x.shape[1]), F32)], axis=0).astype(BF16)
        kn, vn = pad(kvan_ref[0, :, :A_WIDTH]), pad(kvan_ref[0, :, A_WIDTH:])
        update_a(_dot_nt(qa_s[...], kn) + biasa_ref[:, PAGE_ROWS:], lambda p: _dot(p, vn))
        kn, vn = pad(kvbn_ref[0, :, :B_WIDTH]), pad(kvbn_ref[0, :, B_WIDTH:])
        update_b(_dot_nt(qb_s[...], kn) + biasb_ref[:, PAGE_ROWS:] + tile_rows(seln_ref[0], B_HEADS),
                 lambda p: _dot(p, vn))

        lam = _lambda_value(lam_ref, lam_init)
        oa = acca_s[...] / la_s[...]
        for h in range(A_HEADS):
            o = oa[h * 2 * nq:h * 2 * nq + nq] - lam * oa[h * 2 * nq + nq:(h + 1) * 2 * nq]
            oa_ref[0, :, h * LANES:(h + 1) * LANES] = _sub_layer_norm(o, g_ref[...], lam_init)
        ob = accb_s[...] / lb_s[...]
        lo = lax.broadcasted_iota(jnp.int32, (nq, LANES), 1) < HEAD_DIM
        for m in range(B_HEADS // 2):
            ob_ref[0, :, m * LANES:(m + 1) * LANES] = jnp.where(
                lo, ob[2 * m * nq:(2 * m + 1) * nq], ob[(2 * m + 1) * nq:(2 * m + 2) * nq])


def _sample_attn(page_table, lam4, g_sub, qa, qb, mask_a, mask_b, bias_a, bias_b, kva_new, kvb_new, sel,
                 cache_a, cache_bt, layer, lam_init):
    s, n_pages = page_table.shape
    nq = qa.shape[1]
    g_pages = PAGES_PER_STEP
    n_chunks = n_pages // g_pages
    rows_a, rows_b = A_HEADS * 2 * nq, B_HEADS * nq
    const = lambda shape: pl.BlockSpec(shape, lambda i, c, pt: (0,) * len(shape))
    seq = lambda shape: pl.BlockSpec(shape, lambda i, c, pt: (i,) + (0,) * (len(shape) - 1))
    page_a = lambda g: pl.BlockSpec((1, 1) + cache_a.shape[2:],
                                    lambda i, c, pt, g=g: (layer, pt[i, c * g_pages + g], 0, 0))
    page_b = lambda g: pl.BlockSpec((1, 1) + cache_bt.shape[2:],
                                    lambda i, c, pt, g=g: (layer, pt[i, c * g_pages + g], 0, 0, 0, 0))
    grid_spec = pltpu.PrefetchScalarGridSpec(
        num_scalar_prefetch=1,
        grid=(s, n_chunks),
        in_specs=[const((4, HEAD_DIM)), const((1, 2 * HEAD_DIM)),
                  seq((1, nq, A_WIDTH)), seq((1, nq, B_WIDTH)),
                  const((rows_a, A_WIDTH)), const((rows_b, B_WIDTH)),
                  const((rows_a, 2 * PAGE_ROWS)), const((rows_b, 2 * PAGE_ROWS)),
                  seq((1, nq, 2 * A_WIDTH)), seq((1, nq, 2 * B_WIDTH)),
                  pl.BlockSpec((g_pages, nq, PAGE_ROWS), lambda i, c, pt: (c, i, 0)),
                  pl.BlockSpec((1, nq, PAGE_ROWS), lambda i, c, pt: (n_pages, i, 0))]
                 + [page_a(g) for g in range(g_pages)]
                 + [page_b(g) for g in range(g_pages)],
        out_specs=[seq((1, nq, A_WIDTH)), seq((1, nq, B_WIDTH))],
        scratch_shapes=[pltpu.VMEM((rows_a, A_WIDTH), BF16), pltpu.VMEM((rows_b, B_WIDTH), BF16),
                        pltpu.VMEM((rows_a, 1), F32), pltpu.VMEM((rows_a, 1), F32),
                        pltpu.VMEM((rows_a, LANES), F32),
                        pltpu.VMEM((rows_b, 1), F32), pltpu.VMEM((rows_b, 1), F32),
                        pltpu.VMEM((rows_b, LANES), F32)],
    )
    return pl.pallas_call(
        functools.partial(_sample_attn_kernel, n_chunks=n_chunks, lam_init=lam_init),
        grid_spec=grid_spec,
        out_shape=[jax.ShapeDtypeStruct((s, nq, A_WIDTH), F32), jax.ShapeDtypeStruct((s, nq, B_WIDTH), F32)],
        compiler_params=_cparams(("arbitrary", "arbitrary")),
        name="sample_attn",
    )(page_table, lam4, g_sub, qa, qb, mask_a, mask_b, bias_a, bias_b, kva_new, kvb_new, sel, sel,
      *([cache_a] * g_pages), *([cache_bt] * g_pages))


def _outproj_kernel(oa_ref, ob_ref, sg_ref, x_ref, g1_ref, sh2_ref, sc2_ref, wba_ref, wbb_ref, wout_ref,
                    lng_ref, lnb_ref, wr_ref, br_ref, x1_ref, h2_ref, comb_ref, *, alpha):
    ya = _dot(oa_ref[...], wba_ref[...])
    yb = _dot(ob_ref[...], wbb_ref[...])
    t = sg_ref[:, :D_MODEL] * ya + sg_ref[:, D_MODEL:] * yb
    mix = _dot(t.astype(BF16), wout_ref[...])
    x1 = _layer_norm(alpha * x_ref[...] + g1_ref[0] * mix, lng_ref[...], lnb_ref[...])
    x1_ref[...] = x1
    h2 = x1 * (1.0 + sc2_ref[0]) + sh2_ref[0]
    h2_ref[...] = h2.astype(BF16)

    logits = _dot3(h2, wr_ref[...]) + br_ref[...]
    lane = lax.broadcasted_iota(jnp.int32, logits.shape, 1).astype(F32)
    big = float(LANES)
    is_group = jnp.logical_and(lane >= N_EXPERTS, lane < N_EXPERTS + N_GROUPS)
    lg = jnp.where(is_group, logits, NEG_INF)
    mg = jnp.max(lg, axis=-1, keepdims=True)
    g_sel = jnp.min(jnp.where(lg == mg, lane, big), axis=-1, keepdims=True) - N_EXPERTS
    p_g = 1.0 / jnp.sum(jnp.exp(lg - mg), axis=-1, keepdims=True)
    first = g_sel * EXPERTS_PER_GROUP
    in_group = jnp.logical_and(lane >= first, lane < first + EXPERTS_PER_GROUP)
    le = jnp.where(in_group, logits, NEG_INF)
    ex = jnp.exp(le - jnp.max(le, axis=-1, keepdims=True))
    pe = jnp.where(in_group, ex / jnp.sum(ex, axis=-1, keepdims=True), -1.0)
    v1 = jnp.max(pe, axis=-1, keepdims=True)
    i1 = jnp.min(jnp.where(pe == v1, lane, big), axis=-1, keepdims=True)
    pe2 = jnp.where(lane == i1, -1.0, pe)
    v2 = jnp.max(pe2, axis=-1, keepdims=True)
    i2 = jnp.min(jnp.where(pe2 == v2, lane, big), axis=-1, keepdims=True)
    tot = v1 + v2
    comb_ref[...] = (jnp.where(lane == i1, p_g * (v1 / tot), 0.0)
                     + jnp.where(lane == i2, p_g * (v2 / tot), 0.0))


def _out_proj(oa, ob, sg, x, ada3, wba, wbb, wout, lng, lnb, wr, br, rows_per_batch, alpha):
    n = x.shape[0]
    tm = ROW_TILE
    row = lambda w: pl.BlockSpec((tm, w), lambda i: (i, 0))
    full = lambda a: pl.BlockSpec(a.shape, lambda i: (0,) * a.ndim)
    return pl.pallas_call(
        functools.partial(_outproj_kernel, alpha=alpha),
        grid=(n // tm,),
        in_specs=[row(A_WIDTH), row(B_WIDTH), row(2 * D_MODEL), row(D_MODEL),
                  _row_vec_spec(ada3, 2, tm, rows_per_batch),
                  _row_vec_spec(ada3, 3, tm, rows_per_batch),
                  _row_vec_spec(ada3, 4, tm, rows_per_batch),
                  full(wba), full(wbb), full(wout), full(lng), full(lnb), full(wr), full(br)],
        out_specs=[row(D_MODEL), row(D_MODEL), row(LANES)],
        out_shape=[jax.ShapeDtypeStruct((n, D_MODEL), F32), jax.ShapeDtypeStruct((n, D_MODEL), BF16),
                   jax.ShapeDtypeStruct((n, LANES), F32)],
        compiler_params=_cparams(("arbitrary",)),
        name="out_proj",
    )(oa, ob, sg, x, ada3, ada3, ada3, wba, wbb, wout, lng, lnb, wr, br)


def _moe_kernel(h_ref, x1_ref, comb_ref, g2_ref, wup_ref, wdn_ref, lng_ref, lnb_ref, o_ref, acc_ref, *, alpha):
    e = pl.program_id(1)

    @pl.when(e == 0)
    def _():
        acc_ref[...] = jnp.zeros(acc_ref.shape, F32)

    hid = _dot(h_ref[...], wup_ref[0])
    gate, up = hid[:, :EXPERT_HIDDEN], hid[:, EXPERT_HIDDEN:]
    act = (gate * _sigmoid(gate) * up).astype(BF16)
    y = _dot(act, wdn_ref[0])
    comb = comb_ref[...]
    lane = lax.broadcasted_iota(jnp.int32, comb.shape, 1)
    w = jnp.sum(jnp.where(lane == e, comb, 0.0), axis=-1, keepdims=True)
    acc_ref[...] += w * y

    @pl.when(e == N_EXPERTS - 1)
    def _():
        u = alpha * x1_ref[...] + g2_ref[0] * acc_ref[...]
        o_ref[...] = _layer_norm(u, lng_ref[...], lnb_ref[...])


def _moe(h2, x1, comb, ada3, wup, wdn, lng, lnb, rows_per_batch, alpha):
    n = x1.shape[0]
    tm = min(MOE_TILE, n)
    if ada3.shape[1] != 1:
        ada3 = ada3.reshape(n // tm, tm, ada3.shape[2])
    row = lambda w: pl.BlockSpec((tm, w), lambda i, e: (i, 0))
    full = lambda a: pl.BlockSpec(a.shape, lambda i, e: (0,) * a.ndim)
    return pl.pallas_call(
        functools.partial(_moe_kernel, alpha=alpha),
        grid=(n // tm, N_EXPERTS),
        in_specs=[row(D_MODEL), row(D_MODEL), row(LANES),
                  _row_vec_spec(ada3, 5, tm, rows_per_batch),
                  pl.BlockSpec((1, D_MODEL, 2 * EXPERT_HIDDEN), lambda i, e: (e, 0, 0)),
                  pl.BlockSpec((1, EXPERT_HIDDEN, D_MODEL), lambda i, e: (e, 0, 0)),
                  full(lng), full(lnb)],
        out_specs=row(D_MODEL),
        out_shape=jax.ShapeDtypeStruct((n, D_MODEL), F32),
        scratch_shapes=[pltpu.VMEM((tm, D_MODEL), F32)],
        compiler_params=_cparams(("arbitrary", "arbitrary")),
        name="moe",
    )(h2, x1, comb, ada3, wup, wdn, lng, lnb)


def _prompt_dist():
    i = jnp.arange(TB, dtype=jnp.int32)[:, None]
    j = jnp.arange(TB, dtype=jnp.int32)[None, :]
    return jnp.concatenate([d * TB + i - j for d in range(-1, N_BIAS_TILES - 1)], axis=0)


def _sample_dist(nq, past_len):
    i = jnp.arange(nq, dtype=jnp.int32)[:, None]
    j = jnp.arange(PAGE_ROWS, dtype=jnp.int32)[None, :]
    last_page = past_len + i - (past_len - PAGE_ROWS + j)
    new = jnp.where(j < nq, i - j, -1)
    return jnp.concatenate([last_page, new], axis=1)


def kernel(x_prompt, x_sample, cache_kv_diff, cache_kv_dsa, cache_kidx, page_table, c_prompt, c_sample,
           rel_bias, w_ada, b_ada, w_in, lambda_q1, lambda_k1, lambda_q2, lambda_k2, subln_g, w_branch_a,
           w_branch_b, w_out, ln1_g, ln1_b, w_router_group, b_router_group, w_router_expert,
           b_router_expert, w_up, w_down, ln2_g, ln2_b):
    b, t, d = x_prompt.shape
    s, nq, _ = x_sample.shape
    depth = w_in.shape[0]
    n_pool = cache_kidx.shape[1]
    cache_a = cache_kv_diff.reshape(depth, n_pool, PAGE_ROWS * 2 * A_HEADS, 2 * HEAD_DIM)
    cache_bt = jnp.transpose(cache_kv_dsa, (0, 1, 3, 4, 5, 2))
    cache_it = jnp.transpose(cache_kidx, (0, 1, 3, 2))
    n_pages = page_table.shape[1]
    past_len = n_pages * PAGE_ROWS
    alpha = (2 * depth) ** 0.25
    topk_p = min(DSA_TOPK_MAX, t // 4)
    topk_s = min(DSA_TOPK_MAX, (past_len + nq) // 4)
    assert d == D_MODEL and t % MOE_TILE == 0 and (s * nq) % ROW_TILE == 0 and n_pages % PAGES_PER_STEP == 0
    assert MOE_TILE % TB == 0 and B_HEADS == 2 * N_CHAINS and A_HEADS % N_CHAINS == 0
    assert cache_kidx.shape[2] == PAGE_ROWS and nq <= 8

    bias_p = _bias_tiles(rel_bias, _prompt_dist()).reshape(A_HEADS + B_HEADS, N_BIAS_TILES, TB, TB)
    bias_s = _bias_tiles(rel_bias, _sample_dist(nq, past_len))
    bias_sa = jnp.broadcast_to(bias_s[:A_HEADS, None], (A_HEADS, 2, nq, 2 * PAGE_ROWS)).reshape(
        A_HEADS * 2 * nq, 2 * PAGE_ROWS)
    bias_sb = bias_s[A_HEADS:].reshape(B_HEADS * nq, 2 * PAGE_ROWS)
    tri_p = (jnp.arange(TB)[:, None] <= jnp.arange(TB)[None, :]).astype(BF16)
    tri_s = tri_p[:PAGE_ROWS, :PAGE_ROWS]
    lane_a = jnp.arange(A_WIDTH)[None, :] // HEAD_DIM
    mask_a = (lane_a == (jnp.arange(A_HEADS * 2 * nq)[:, None] // nq)).astype(F32)
    lane_b = jnp.arange(B_WIDTH)[None, :] // HEAD_DIM
    mask_b = (lane_b == (jnp.arange(B_HEADS * nq)[:, None] // nq)).astype(F32)

    xp = x_prompt.reshape(b * t, d)
    xs = x_sample.reshape(s * nq, d)
    c_all = jnp.concatenate([c_prompt, c_sample], axis=0)
    c_all = jnp.pad(c_all, ((0, -(b + s) % 8), (0, 0)))
    outs = [[] for _ in range(6)]
    for l in range(depth):
        lam_init = 0.8 - 0.6 * math.exp(-0.3 * l)
        lam4 = jnp.stack([lambda_q1[l], lambda_k1[l], lambda_q2[l], lambda_k2[l]]).astype(F32)
        g_sub = subln_g[l].reshape(1, 2 * HEAD_DIM)
        w = w_in[l]
        w2 = jnp.concatenate([w[:, :C_KI + IDX_DIM], w[:, C_KI:C_KI + IDX_DIM],
                              w[:, C_KI + IDX_DIM:C_KI + IDX_DIM + IDX_HEADS],
                              jnp.zeros((d, LANES - IDX_HEADS), w.dtype),
                              w[:, C_KI + IDX_DIM + IDX_HEADS:]], axis=1).astype(BF16)
        wba, wbb, wout = w_branch_a[l].astype(BF16), w_branch_b[l].astype(BF16), w_out[l].astype(BF16)
        wup, wdn = w_up[l].astype(BF16), w_down[l].astype(BF16)
        wr = jnp.concatenate([w_router_expert[l], w_router_group[l],
                              jnp.zeros((d, LANES - N_EXPERTS - N_GROUPS), F32)], axis=1)
        br = jnp.concatenate([b_router_expert[l], b_router_group[l],
                              jnp.zeros((LANES - N_EXPERTS - N_GROUPS,), F32)]).reshape(1, LANES)
        ln1 = (ln1_g[l].reshape(1, d), ln1_b[l].reshape(1, d))
        ln2 = (ln2_g[l].reshape(1, d), ln2_b[l].reshape(1, d))

        ada = _ada(c_all, w_ada[l], b_ada[l])
        ada_p = ada[:b].reshape(b, 1, 6 * d)
        ada_s = jnp.broadcast_to(ada[b:b + s, None], (s, nq, 6 * d)).reshape(s * nq // ROW_TILE, ROW_TILE, 6 * d)

        p = _in_proj(xp, ada_p, w2, t)
        oa = _diff_attn_prompt(lam4, g_sub, p["qa"], p["kva"], bias_p[:A_HEADS], b, t, lam_init)
        ob = _dsa_prompt(p, bias_p[A_HEADS:], tri_p, b, t, topk_p)
        x1, h2, comb = _out_proj(oa, ob, p["sg"], xp, ada_p, wba, wbb, wout, *ln1, wr, br, t, alpha)
        xp = _moe(h2, x1, comb, ada_p, wup, wdn, *ln2, t, alpha)
        outs[0].append(p["ra"].reshape(b, t, 2, A_HEADS, 2 * HEAD_DIM))
        outs[1].append(p["rb"].reshape(b, t, 2, B_HEADS, HEAD_DIM))
        outs[2].append(p["ki"].reshape(b, t, IDX_DIM))

        q = _in_proj(xs, ada_s, w2, nq)
        q_stack = q["qi"].reshape(s, nq, IDX_HEADS, IDX_DIM).transpose(0, 2, 1, 3).reshape(
            s, IDX_HEADS * nq, IDX_DIM)
        w_stack = q["wi"][:, :IDX_HEADS].reshape(s, nq, IDX_HEADS).transpose(0, 2, 1).reshape(
            s, IDX_HEADS * nq, 1)
        scores = _sample_idx(page_table, q_stack, w_stack, q["ki"].reshape(s, nq, IDX_DIM), cache_it, l)
        sel = _sample_select(scores, tri_s, topk_s)
        oa_s, ob_s = _sample_attn(
            page_table, lam4, g_sub,
            q["qa"].astype(F32).reshape(s, nq, A_WIDTH), q["qb"].astype(F32).reshape(s, nq, B_WIDTH),
            mask_a, mask_b, bias_sa, bias_sb,
            q["ra"].reshape(s, nq, 2 * A_WIDTH), q["rb"].reshape(s, nq, 2 * B_WIDTH), sel,
            cache_a, cache_bt, l, lam_init)
        x1, h2, comb = _out_proj(oa_s.reshape(s * nq, A_WIDTH).astype(BF16),
                                 ob_s.reshape(s * nq, B_WIDTH).astype(BF16),
                                 q["sg"], xs, ada_s, wba, wbb, wout, *ln1, wr, br, nq, alpha)
        xs = _moe(h2, x1, comb, ada_s, wup, wdn, *ln2, nq, alpha)
        outs[3].append(q["ra"].reshape(s, nq, 2, A_HEADS, 2 * HEAD_DIM))
        outs[4].append(q["rb"].reshape(s, nq, 2, B_HEADS, HEAD_DIM))
        outs[5].append(q["ki"].reshape(s, nq, IDX_DIM))

    return (xp.reshape(b, t, d), xs.reshape(s, nq, d)) + tuple(jnp.stack(o, 0) for o in outs)
```

```python
import functools
import math

import jax
import jax.numpy as jnp
from jax import lax
from jax.experimental import pallas as pl
from jax.experimental.pallas import tpu as pltpu

D_MODEL = 1024
HEAD_DIM = 64
A_HEADS = 8
B_HEADS = 8
IDX_HEADS = 8
IDX_DIM = 64
DSA_TOPK_MAX = 256
N_BUCKETS = 32
MAX_DISTANCE = 128
N_GROUPS = 4
EXPERTS_PER_GROUP = 4
N_EXPERTS = N_GROUPS * EXPERTS_PER_GROUP
EXPERT_HIDDEN = 512
LN_EPS = 1e-5

LANES = 128
TB = 256
ROW_TILE = 256
MOE_ROWS = 256
PAGE_ROWS = 128
PAGES_PER_STEP = 8
VMEM_LIMIT = 56 * 1024 * 1024

A_WIDTH = A_HEADS * 2 * HEAD_DIM
B_WIDTH = B_HEADS * HEAD_DIM
I_WIDTH = IDX_HEADS * IDX_DIM
C_QA, C_KVA, C_QB, C_KVB, C_QI, C_KI, C_WI, C_G = 0, 1024, 3072, 3584, 4608, 5120, 5248, 5376
W2_WIDTH = C_G + 2 * D_MODEL
HC_WIDTH = D_MODEL + LANES
GROUP_LANE = N_EXPERTS + N_GROUPS

F32 = jnp.float32
BF16 = jnp.bfloat16
NEG_INF = float("-inf")
INT_MIN = -2 ** 31


def _cparams(sem):
    return pltpu.CompilerParams(dimension_semantics=sem, vmem_limit_bytes=VMEM_LIMIT)


def _dot(a, b):
    return jnp.dot(a, b, preferred_element_type=F32)


def _dot_nt(a, b):
    return lax.dot_general(a, b, (((1,), (1,)), ((), ())), preferred_element_type=F32)


def _split(a):
    hi = a.astype(BF16)
    lo = (a - hi.astype(F32)).astype(BF16)
    return hi, lo


def _dot3(a, b):
    a_hi, a_lo = _split(a)
    b_hi, b_lo = _split(b)
    return _dot(a_hi, b_hi) + _dot(a_lo, b_hi) + _dot(a_hi, b_lo)


def _sigmoid(x):
    return 1.0 / (1.0 + jnp.exp(-x))


def _layer_norm(u, g, b):
    mu = jnp.mean(u, axis=-1, keepdims=True)
    d = u - mu
    var = jnp.mean(d * d, axis=-1, keepdims=True)
    return d * lax.rsqrt(var + LN_EPS) * g + b


def _bias_kernel(tab_ref, dist_ref, out_ref):
    h = pl.program_id(0)
    d = dist_ref[...]
    n = jnp.maximum(d, 0)
    max_exact = N_BUCKETS // 2
    nf = jnp.maximum(n, 1).astype(F32)
    large = max_exact + (jnp.log(nf / max_exact) / math.log(MAX_DISTANCE / max_exact)
                         * (N_BUCKETS - max_exact)).astype(jnp.int32)
    large = jnp.minimum(large, N_BUCKETS - 1)
    bucket = jnp.where(n < max_exact, n, large)
    last = tab_ref[N_BUCKETS - 1, h]
    acc = jnp.zeros(d.shape, F32)
    for m in range(N_BUCKETS - 1):
        acc = jnp.where(bucket == m, tab_ref[m, h] - last, acc)
    out_ref[0] = jnp.where(d < 0, NEG_INF, acc)


def _bias_tiles(rel_bias, dist):
    n_heads = rel_bias.shape[1]
    r, c = dist.shape
    return pl.pallas_call(
        _bias_kernel,
        grid=(n_heads,),
        in_specs=[pl.BlockSpec(memory_space=pltpu.SMEM),
                  pl.BlockSpec((r, c), lambda h: (0, 0))],
        out_specs=pl.BlockSpec((1, r, c), lambda h: (h, 0, 0)),
        out_shape=jax.ShapeDtypeStruct((n_heads, r, c), F32),
        compiler_params=_cparams(("arbitrary",)),
        name="bias_tiles",
    )(rel_bias, dist)


def _ada_kernel(c_ref, w_ref, b_ref, o_ref):
    c = c_ref[...]
    o_ref[...] = _dot3(c * _sigmoid(c), w_ref[...]) + b_ref[...]


def _ada(c_all, w_ada, b_ada):
    r, d = c_all.shape
    n = w_ada.shape[1]
    tn = 512
    return pl.pallas_call(
        _ada_kernel,
        grid=(n // tn,),
        in_specs=[pl.BlockSpec((r, d), lambda j: (0, 0)),
                  pl.BlockSpec((d, tn), lambda j: (0, j)),
                  pl.BlockSpec((1, tn), lambda j: (0, j))],
        out_specs=pl.BlockSpec((r, tn), lambda j: (0, j)),
        out_shape=jax.ShapeDtypeStruct((r, n), F32),
        compiler_params=_cparams(("arbitrary",)),
        name="ada",
    )(c_all, w_ada, b_ada.reshape(1, n))


def _inproj_kernel(x_ref, sh_ref, sc_ref, w_ref, qa_ref, ra_ref, kva_ref, qb_ref, rb_ref, kvb_ref,
                   qi_ref, ki_ref, kk_ref, wi_ref, sg_ref):
    h = (x_ref[...] * (1.0 + sc_ref[0]) + sh_ref[0]).astype(BF16)
    q_scale = HEAD_DIM ** -0.5

    def mm(c0, n):
        return _dot(h, w_ref[:, c0:c0 + n])

    for c in range(0, A_WIDTH, 512):
        qa_ref[:, c:c + 512] = (mm(C_QA + c, 512) * q_scale).astype(BF16)
    for c in range(0, 2 * A_WIDTH, 512):
        a = mm(C_KVA + c, 512)
        ra_ref[:, c:c + 512] = a
        kva_ref[:, c:c + 512] = a.astype(BF16)
    qb_ref[...] = (mm(C_QB, 512) * q_scale).astype(BF16)
    for c in range(0, 2 * B_WIDTH, 512):
        a = mm(C_KVB + c, 512)
        rb_ref[:, c:c + 512] = a
        kvb_ref[:, c:c + 512] = a.astype(BF16)
    qi_ref[...] = (mm(C_QI, 512) * q_scale).astype(BF16)
    a = mm(C_KI, 2 * LANES)
    ki_ref[...] = a[:, :IDX_DIM]
    kk_ref[...] = a[:, :LANES].astype(BF16)
    wi_ref[...] = a[:, LANES:] * IDX_HEADS ** -0.5
    for c in range(0, 2 * D_MODEL, 512):
        sg_ref[:, c:c + 512] = _sigmoid(mm(C_G + c, 512))


def _row_vec_spec(arr, col, tm, rows_per_batch):
    if arr.shape[1] == 1:
        per = rows_per_batch // tm
        return pl.BlockSpec((1, 1, D_MODEL), lambda i, *_: (i // per, 0, col))
    return pl.BlockSpec((1, tm, D_MODEL), lambda i, *_: (i, 0, col))


def _in_proj(x, ada3, w2, rows_per_batch):
    n = x.shape[0]
    tm = ROW_TILE
    row = lambda w: pl.BlockSpec((tm, w), lambda i: (i, 0))
    outs = [("qa", A_WIDTH, BF16), ("ra", 2 * A_WIDTH, F32), ("kva", 2 * A_WIDTH, BF16),
            ("qb", B_WIDTH, BF16), ("rb", 2 * B_WIDTH, F32), ("kvb", 2 * B_WIDTH, BF16),
            ("qi", I_WIDTH, BF16), ("ki", IDX_DIM, F32), ("kk", LANES, BF16), ("wi", LANES, F32),
            ("sg", 2 * D_MODEL, F32)]
    res = pl.pallas_call(
        _inproj_kernel,
        grid=(n // tm,),
        in_specs=[row(D_MODEL),
                  _row_vec_spec(ada3, 0, tm, rows_per_batch),
                  _row_vec_spec(ada3, 1, tm, rows_per_batch),
                  pl.BlockSpec((D_MODEL, W2_WIDTH), lambda i: (0, 0), pipeline_mode=pl.Buffered(1))],
        out_specs=[row(w) for _, w, _ in outs],
        out_shape=[jax.ShapeDtypeStruct((n, w), dt) for _, w, dt in outs],
        compiler_params=_cparams(("arbitrary",)),
        name="in_proj",
    )(x, ada3, ada3, w2)
    return {name: r for (name, _, _), r in zip(outs, res)}


def _stack2(x):
    return jnp.concatenate([x, x], axis=0)


LOG2_E = 1.4426950408889634
N_CHAINS = 4
N_BIAS_TILES = 4


def _bias_index(tile_distance):
    return jnp.clip(tile_distance, -1, N_BIAS_TILES - 2) + 1


def _lane_fold(x, op):
    r = x[:, :LANES]
    for c in range(1, x.shape[1] // LANES):
        r = op(r, x[:, c * LANES:(c + 1) * LANES])
    return r


def _two_pass_attend(q_ref, k_ref, v_ref, nt, near_bias, every_bias, s_ref, st_ref):
    rows = 2 * TB
    st_ref[0] = jnp.full((N_CHAINS, rows, LANES), NEG_INF, F32)
    st_ref[1] = jnp.zeros((N_CHAINS, rows, LANES), F32)
    st_ref[2] = jnp.zeros((N_CHAINS, rows, LANES), F32)

    def tiles(jj):
        return [(2 * jj + u, jnp.minimum(2 * jj + u, nt - 1)) for u in range(2)]

    def rows_of(ref, c, pair):
        return jnp.concatenate([ref[pl.ds(pl.multiple_of(jc * TB, TB), TB), c * LANES:(c + 1) * LANES]
                                for _, jc in pair], axis=0)

    def first(near):
        def body(jj, _):
            pair = tiles(jj)
            every = None
            if every_bias is not None:
                every = _stack2(jnp.concatenate([every_bias(jc) for _, jc in pair], axis=1))
            for c in range(N_CHAINS):
                s = _dot_nt(_masked_pair(q_ref[:, c * LANES:(c + 1) * LANES]), rows_of(k_ref, c, pair))
                if near:
                    s = s + jnp.concatenate([near_bias(c, j) for j, _ in pair], axis=1)
                if every is not None:
                    s = s + every
                s = s * LOG2_E
                s_ref[jj, c] = s
                st_ref[0, c] = jnp.maximum(st_ref[0, c], _lane_fold(s, jnp.maximum))
            return 0
        return body

    n_steps = (nt + 1) // 2
    n_far = jnp.maximum(nt - 2, 0) // 2
    lax.fori_loop(0, n_far, first(False), 0)
    lax.fori_loop(n_far, n_steps, first(True), 0)
    for c in range(N_CHAINS):
        st_ref[0, c] = jnp.broadcast_to(jnp.max(st_ref[0, c], axis=-1, keepdims=True), (rows, LANES))

    def second(jj, _):
        pair = tiles(jj)
        for c in range(N_CHAINS):
            s = s_ref[jj, c]
            mb = st_ref[0, c]
            p = [jnp.exp2(s[:, i * LANES:(i + 1) * LANES] - mb) for i in range(2 * TB // LANES)]
            st_ref[1, c] += (p[0] + p[1]) + (p[2] + p[3])
            st_ref[2, c] += _dot(jnp.concatenate(p, axis=1).astype(BF16), rows_of(v_ref, c, pair))
        return 0

    lax.fori_loop(0, n_steps, second, 0)
    return lambda c: (st_ref[2, c], jnp.sum(st_ref[1, c], axis=-1, keepdims=True))


def _lambda_value(lam_ref, lam_init):
    a = jnp.sum(lam_ref[0:1, :] * lam_ref[1:2, :], axis=-1, keepdims=True)
    b = jnp.sum(lam_ref[2:3, :] * lam_ref[3:4, :], axis=-1, keepdims=True)
    return jnp.exp(a) - jnp.exp(b) + lam_init


def _sub_layer_norm(o, g, lam_init):
    o = o * lax.rsqrt(jnp.mean(o * o, axis=-1, keepdims=True) + LN_EPS)
    return o * g * (1.0 - lam_init)


def _masked_pair(q):
    lo = lax.broadcasted_iota(jnp.int32, q.shape, 1) < HEAD_DIM
    zero = jnp.zeros_like(q)
    return jnp.concatenate([jnp.where(lo, q, zero), jnp.where(lo, zero, q)], axis=0)


def _attend_scratch(nq):
    return [pltpu.VMEM(((nq + 1) // 2, N_CHAINS, 2 * TB, 2 * TB), F32),
            pltpu.VMEM((3, N_CHAINS, 2 * TB, LANES), F32)]


def _diff_attn_kernel(lam_ref, g_ref, q_ref, k_ref, v_ref, bias_ref, o_ref, s_ref, st_ref, *, lam_init):
    qi = pl.program_id(2)
    near = lambda c, j: _stack2(bias_ref[c, _bias_index(qi - j)])
    result = _two_pass_attend(q_ref, k_ref, v_ref, qi + 1, near, None, s_ref, st_ref)
    lam = _lambda_value(lam_ref, lam_init)
    for c in range(N_CHAINS):
        acc, l = result(c)
        o = acc / l
        o = o[:TB] - lam * o[TB:]
        o_ref[:, c * LANES:(c + 1) * LANES] = _sub_layer_norm(o, g_ref[...], lam_init).astype(BF16)


def _diff_attn_prompt(lam4, g_sub, qa, kva, bias_p, b, t, lam_init):
    nq = t // TB
    groups = A_HEADS // N_CHAINS
    width = N_CHAINS * LANES
    return pl.pallas_call(
        functools.partial(_diff_attn_kernel, lam_init=lam_init),
        grid=(b, groups, nq),
        in_specs=[pl.BlockSpec((4, HEAD_DIM), lambda bi, h, i: (0, 0)),
                  pl.BlockSpec((1, 2 * HEAD_DIM), lambda bi, h, i: (0, 0)),
                  pl.BlockSpec((TB, width), lambda bi, h, i: (bi * nq + i, h)),
                  pl.BlockSpec((t, width), lambda bi, h, i: (bi, h)),
                  pl.BlockSpec((t, width), lambda bi, h, i: (bi, groups + h)),
                  pl.BlockSpec((N_CHAINS, N_BIAS_TILES, TB, TB), lambda bi, h, i: (h, 0, 0, 0))],
        out_specs=pl.BlockSpec((TB, width), lambda bi, h, i: (bi * nq + i, h)),
        out_shape=jax.ShapeDtypeStruct((b * t, A_WIDTH), BF16),
        scratch_shapes=_attend_scratch(nq),
        compiler_params=_cparams(("arbitrary", "arbitrary", "arbitrary")),
        name="diff_attn_prompt",
    )(lam4, g_sub, qa, kva, kva, bias_p)


KEY_NEG_INF = INT_MIN + 0x7FFFFF


def _key_to_float(key):
    bits = jnp.where(key < 0, key ^ jnp.int32(0x7FFFFFFF), key)
    return jnp.where(key < KEY_NEG_INF, NEG_INF, lax.bitcast_convert_type(bits, F32))


def _topk_select(sc_ref, nt, topk, tri_ref):
    rows, tw = sc_ref.shape[1], sc_ref.shape[2]
    kf = float(topk)

    def count_ge(cf):
        def body(j, acc):
            g = jnp.where(sc_ref[j] >= cf, 1.0, 0.0)
            r = g[:, :LANES]
            for c in range(1, tw // LANES):
                r = r + g[:, c * LANES:(c + 1) * LANES]
            return acc + r
        acc = lax.fori_loop(0, nt, body, jnp.zeros((rows, LANES), F32))
        return jnp.sum(acc, axis=-1, keepdims=True)

    def search(p, t):
        cand = t + lax.shift_left(jnp.int32(1), jnp.int32(31) - p)
        c = count_ge(_key_to_float(cand))
        return jnp.where(c >= kf, cand, t)

    t = lax.fori_loop(0, 32, search, jnp.full((rows, 1), INT_MIN, jnp.int32))
    t_lo = _key_to_float(t)
    t_hi = _key_to_float(t + 1)
    need = kf - count_ge(t_hi)
    tri = tri_ref[...]

    def finish(j, c):
        s = sc_ref[j]
        gt = s >= t_hi
        eq = jnp.logical_and(s >= t_lo, jnp.logical_not(gt))
        e = jnp.where(eq, 1.0, 0.0)
        rank = _dot(e.astype(BF16), tri) + c
        sel = jnp.logical_or(gt, jnp.logical_and(eq, rank <= need))
        sel = jnp.logical_and(sel, s > NEG_INF)
        sc_ref[j] = jnp.where(sel, 0.0, NEG_INF)
        return c + jnp.sum(e, axis=-1, keepdims=True)

    lax.fori_loop(0, nt, finish, jnp.zeros((rows, 1), F32))


def _dsa_kernel(qi_ref, kk_ref, wi_ref, qb_ref, kb_ref, vb_ref, bias_ref, tri_ref, o_ref, sc_ref, s_ref,
                st_ref, *, topk):
    qblk = pl.program_id(1)
    nt = qblk + 1
    tq = qi_ref.shape[0]
    lo = lax.broadcasted_iota(jnp.int32, (tq, LANES), 1) < HEAD_DIM
    zero = jnp.zeros((tq, LANES), BF16)

    def halves(qp):
        return jnp.where(lo, qp, zero), jnp.where(lo, zero, qp)

    wi = wi_ref[...]

    def index_tile(j, _):
        off = pl.multiple_of(j * TB, TB)
        kk = kk_ref[pl.ds(off, TB), :]
        acc = jnp.zeros((tq, TB), F32)
        for m in range(IDX_HEADS // 2):
            q_lo, q_hi = halves(qi_ref[:, m * LANES:(m + 1) * LANES])
            acc = acc + wi[:, 2 * m:2 * m + 1] * jnp.maximum(_dot_nt(q_lo, kk), 0.0)
            acc = acc + wi[:, 2 * m + 1:2 * m + 2] * jnp.maximum(_dot_nt(q_hi, kk), 0.0)
        row = lax.broadcasted_iota(jnp.int32, (tq, TB), 0) + qblk * TB
        col = lax.broadcasted_iota(jnp.int32, (tq, TB), 1) + j * TB
        sc_ref[j] = jnp.where(col <= row, acc, NEG_INF)
        return 0

    lax.fori_loop(0, nt, index_tile, 0)
    _topk_select(sc_ref, nt, topk, tri_ref)

    def near(m, j):
        d = _bias_index(qblk - j)
        return jnp.concatenate([bias_ref[2 * m, d], bias_ref[2 * m + 1, d]], axis=0)

    result = _two_pass_attend(qb_ref, kb_ref, vb_ref, nt, near, lambda jc: sc_ref[jc], s_ref, st_ref)
    lo_out = lax.broadcasted_iota(jnp.int32, (tq, LANES), 1) < HEAD_DIM
    for m in range(N_CHAINS):
        acc, l = result(m)
        o = acc / l
        o_ref[:, m * LANES:(m + 1) * LANES] = jnp.where(lo_out, o[:tq], o[tq:]).astype(BF16)


def _dsa_prompt(p, bias_b, tri, b, t, topk):
    nq = t // TB
    return pl.pallas_call(
        functools.partial(_dsa_kernel, topk=topk),
        grid=(b, nq),
        in_specs=[pl.BlockSpec((TB, I_WIDTH), lambda bi, i: (bi * nq + i, 0)),
                  pl.BlockSpec((t, LANES), lambda bi, i: (bi, 0)),
                  pl.BlockSpec((TB, LANES), lambda bi, i: (bi * nq + i, 0)),
                  pl.BlockSpec((TB, B_WIDTH), lambda bi, i: (bi * nq + i, 0)),
                  pl.BlockSpec((t, B_WIDTH), lambda bi, i: (bi, 0)),
                  pl.BlockSpec((t, B_WIDTH), lambda bi, i: (bi, 1)),
                  pl.BlockSpec((B_HEADS, N_BIAS_TILES, TB, TB), lambda bi, i: (0, 0, 0, 0),
                               pipeline_mode=pl.Buffered(1)),
                  pl.BlockSpec((TB, TB), lambda bi, i: (0, 0))],
        out_specs=pl.BlockSpec((TB, B_WIDTH), lambda bi, i: (bi * nq + i, 0)),
        out_shape=jax.ShapeDtypeStruct((b * t, B_WIDTH), BF16),
        scratch_shapes=[pltpu.VMEM((nq, TB, TB), F32)] + _attend_scratch(nq),
        compiler_params=_cparams(("arbitrary", "arbitrary")),
        name="dsa_prompt",
    )(p["qi"], p["kk"], p["wi"], p["qb"], p["kvb"], p["kvb"], bias_b, tri)


def _sample_idx_kernel(pt_ref, q_ref, w_ref, kn_ref, *rest, n_pages):
    page_refs, o_ref = rest[:n_pages], rest[n_pages]
    q = q_ref[0]
    w = w_ref[0]
    nq = q.shape[0] // IDX_HEADS

    def combine(qk):
        rel = jnp.maximum(qk, 0.0) * w
        sc = rel[0:nq]
        for h in range(1, IDX_HEADS):
            sc = sc + rel[h * nq:(h + 1) * nq]
        return sc

    for k in range(n_pages):
        o_ref[k] = combine(_dot(q, page_refs[k][0, 0].astype(BF16)))
    new = jnp.concatenate([kn_ref[0], jnp.zeros((PAGE_ROWS - nq, IDX_DIM), F32)], axis=0).astype(BF16)
    row = lax.broadcasted_iota(jnp.int32, (nq, PAGE_ROWS), 0)
    col = lax.broadcasted_iota(jnp.int32, (nq, PAGE_ROWS), 1)
    o_ref[n_pages] = jnp.where(col <= row, combine(_dot_nt(q, new)), NEG_INF)


def _sample_idx(page_table, q_stack, w_stack, ki_new, cache_kidx_t, layer):
    s, n_pages = page_table.shape
    nq = ki_new.shape[1]
    hq = q_stack.shape[1]
    page_spec = lambda k: pl.BlockSpec((1, 1, IDX_DIM, PAGE_ROWS), lambda i, pt, k=k: (layer, pt[i, k], 0, 0))
    grid_spec = pltpu.PrefetchScalarGridSpec(
        num_scalar_prefetch=1,
        grid=(s,),
        in_specs=[pl.BlockSpec((1, hq, IDX_DIM), lambda i, pt: (i, 0, 0)),
                  pl.BlockSpec((1, hq, 1), lambda i, pt: (i, 0, 0)),
                  pl.BlockSpec((1, nq, IDX_DIM), lambda i, pt: (i, 0, 0))]
                 + [page_spec(k) for k in range(n_pages)],
        out_specs=pl.BlockSpec((n_pages + 1, nq, PAGE_ROWS), lambda i, pt: (0, i, 0)),
    )
    return pl.pallas_call(
        functools.partial(_sample_idx_kernel, n_pages=n_pages),
        grid_spec=grid_spec,
        out_shape=jax.ShapeDtypeStruct((n_pages + 1, s * nq, PAGE_ROWS), F32),
        compiler_params=_cparams(("arbitrary",)),
        name="sample_idx",
    )(page_table, q_stack, w_stack, ki_new, *([cache_kidx_t] * n_pages))


def _select_kernel(sc_ref, tri_ref, o_ref, *, topk):
    o_ref[...] = sc_ref[...]
    _topk_select(o_ref, o_ref.shape[0], topk, tri_ref)


def _sample_select(scores, tri, topk):
    nt, rows, tw = scores.shape
    tr = min(rows, 256)
    return pl.pallas_call(
        functools.partial(_select_kernel, topk=topk),
        grid=(rows // tr,),
        in_specs=[pl.BlockSpec((nt, tr, tw), lambda i: (0, i, 0)),
                  pl.BlockSpec((tw, tw), lambda i: (0, 0))],
        out_specs=pl.BlockSpec((nt, tr, tw), lambda i: (0, i, 0)),
        out_shape=jax.ShapeDtypeStruct(scores.shape, F32),
        compiler_params=_cparams(("arbitrary",)),
        name="sample_select",
    )(scores, tri)


def _sample_attn_kernel(pt_ref, lam_ref, g_ref, qa_ref, qb_ref, mska_ref, mskb_ref, biasa_ref, biasb_ref,
                        kvan_ref, kvbn_ref, selp_ref, seln_ref, *rest, n_chunks, lam_init):
    g_pages = PAGES_PER_STEP
    kva_refs, kvb_refs = rest[:g_pages], rest[g_pages:2 * g_pages]
    oa_ref, ob_ref = rest[2 * g_pages], rest[2 * g_pages + 1]
    qa_s, qb_s, ma_s, la_s, acca_s, mb_s, lb_s, accb_s = rest[2 * g_pages + 2:]
    c = pl.program_id(1)
    nq = qa_ref.shape[1]
    rows_a = A_HEADS * 2 * nq
    rows_b = B_HEADS * nq

    @pl.when(c == 0)
    def _():
        qa_s[...] = (jnp.concatenate([qa_ref[0]] * (2 * A_HEADS), axis=0) * mska_ref[...]).astype(BF16)
        qb_s[...] = (jnp.concatenate([qb_ref[0]] * B_HEADS, axis=0) * mskb_ref[...]).astype(BF16)
        ma_s[...] = jnp.full(ma_s.shape, NEG_INF, F32)
        mb_s[...] = jnp.full(mb_s.shape, NEG_INF, F32)
        la_s[...] = jnp.zeros(la_s.shape, F32)
        lb_s[...] = jnp.zeros(lb_s.shape, F32)
        acca_s[...] = jnp.zeros(acca_s.shape, F32)
        accb_s[...] = jnp.zeros(accb_s.shape, F32)

    def diag_a(r):
        return jnp.concatenate(
            [r[h * 2 * nq:(h + 1) * 2 * nq, h * LANES:(h + 1) * LANES] for h in range(A_HEADS)], axis=0)

    def diag_b(r):
        return jnp.concatenate(
            [r[m * 2 * nq:(m + 1) * 2 * nq, m * LANES:(m + 1) * LANES] for m in range(B_HEADS // 2)], axis=0)

    def update_a(s, values):
        mn = jnp.maximum(ma_s[...], jnp.max(s, axis=-1, keepdims=True))
        p = jnp.exp(s - mn)
        al = jnp.exp(ma_s[...] - mn)
        la_s[...] = al * la_s[...] + jnp.sum(p, axis=-1, keepdims=True)
        acca_s[...] = al * acca_s[...] + diag_a(values(p.astype(BF16)))
        ma_s[...] = mn

    def update_b(s, values):
        mn = jnp.maximum(mb_s[...], jnp.max(s, axis=-1, keepdims=True))
        ms = jnp.where(mn == NEG_INF, 0.0, mn)
        p = jnp.exp(s - ms)
        al = jnp.exp(mb_s[...] - ms)
        lb_s[...] = al * lb_s[...] + jnp.sum(p, axis=-1, keepdims=True)
        accb_s[...] = al * accb_s[...] + diag_b(values(p.astype(BF16)))
        mb_s[...] = mn

    def tile_rows(x, n):
        return jnp.concatenate([x] * n, axis=0)

    def gather_a(first):
        return jnp.concatenate(
            [jnp.concatenate([ref[0, 0, pl.ds(first + h, PAGE_ROWS, stride=2 * A_HEADS), :]
                              for h in range(A_HEADS)], axis=1) for ref in kva_refs], axis=0).astype(BF16)

    last = c == n_chunks - 1
    far = (g_pages - 1) * PAGE_ROWS
    bias_a = jnp.concatenate([jnp.zeros((rows_a, far), F32), jnp.where(last, biasa_ref[:, :PAGE_ROWS], 0.0)], axis=1)
    bias_b = jnp.concatenate([jnp.zeros((rows_b, far), F32), jnp.where(last, biasb_ref[:, :PAGE_ROWS], 0.0)], axis=1)

    ka = gather_a(0)
    va = gather_a(A_HEADS)
    update_a(_dot_nt(qa_s[...], ka) + bias_a, lambda p: _dot(p, va))

    kt = jnp.concatenate([ref[0, 0, 0].reshape(B_WIDTH, PAGE_ROWS) for ref in kvb_refs], axis=1).astype(BF16)
    vt = jnp.concatenate([ref[0, 0, 1].reshape(B_WIDTH, PAGE_ROWS) for ref in kvb_refs], axis=1).astype(BF16)
    sel = jnp.concatenate([tile_rows(selp_ref[g], B_HEADS) for g in range(g_pages)], axis=1)
    update_b(_dot(qb_s[...], kt) + bias_b + sel, lambda p: _dot_nt(p, vt))

    @pl.when(last)
    def _():
        pad = lambda x: jnp.concatenate([x, jnp.zeros((PAGE_ROWS - nq, x.shape[1]), F32)], axis=0).astype(BF16)
        kn, vn = pad(kvan_ref[0, :, :A_WIDTH]), pad(kvan_ref[0, :, A_WIDTH:])
        update_a(_dot_nt(qa_s[...], kn) + biasa_ref[:, PAGE_ROWS:], lambda p: _dot(p, vn))
        kn, vn = pad(kvbn_ref[0, :, :B_WIDTH]), pad(kvbn_ref[0, :, B_WIDTH:])
        update_b(_dot_nt(qb_s[...], kn) + biasb_ref[:, PAGE_ROWS:] + tile_rows(seln_ref[0], B_HEADS),
                 lambda p: _dot(p, vn))

        lam = _lambda_value(lam_ref, lam_init)
        oa = acca_s[...] / la_s[...]
        for h in range(A_HEADS):
            o = oa[h * 2 * nq:h * 2 * nq + nq] - lam * oa[h * 2 * nq + nq:(h + 1) * 2 * nq]
            oa_ref[0, :, h * LANES:(h + 1) * LANES] = _sub_layer_norm(o, g_ref[...], lam_init)
        ob = accb_s[...] / lb_s[...]
        lo = lax.broadcasted_iota(jnp.int32, (nq, LANES), 1) < HEAD_DIM
        for m in range(B_HEADS // 2):
            ob_ref[0, :, m * LANES:(m + 1) * LANES] = jnp.where(
                lo, ob[2 * m * nq:(2 * m + 1) * nq], ob[(2 * m + 1) * nq:(2 * m + 2) * nq])


def _sample_attn(page_table, lam4, g_sub, qa, qb, mask_a, mask_b, bias_a, bias_b, kva_new, kvb_new, sel,
                 cache_a, cache_bt, layer, lam_init):
    s, n_pages = page_table.shape
    nq = qa.shape[1]
    g_pages = PAGES_PER_STEP
    n_chunks = n_pages // g_pages
    rows_a, rows_b = A_HEADS * 2 * nq, B_HEADS * nq
    const = lambda shape: pl.BlockSpec(shape, lambda i, c, pt: (0,) * len(shape))
    seq = lambda shape: pl.BlockSpec(shape, lambda i, c, pt: (i,) + (0,) * (len(shape) - 1))
    page_a = lambda g: pl.BlockSpec((1, 1) + cache_a.shape[2:],
                                    lambda i, c, pt, g=g: (layer, pt[i, c * g_pages + g], 0, 0))
    page_b = lambda g: pl.BlockSpec((1, 1) + cache_bt.shape[2:],
                                    lambda i, c, pt, g=g: (layer, pt[i, c * g_pages + g], 0, 0, 0, 0))
    grid_spec = pltpu.PrefetchScalarGridSpec(
        num_scalar_prefetch=1,
        grid=(s, n_chunks),
        in_specs=[const((4, HEAD_DIM)), const((1, 2 * HEAD_DIM)),
                  seq((1, nq, A_WIDTH)), seq((1, nq, B_WIDTH)),
                  const((rows_a, A_WIDTH)), const((rows_b, B_WIDTH)),
                  const((rows_a, 2 * PAGE_ROWS)), const((rows_b, 2 * PAGE_ROWS)),
                  seq((1, nq, 2 * A_WIDTH)), seq((1, nq, 2 * B_WIDTH)),
                  pl.BlockSpec((g_pages, nq, PAGE_ROWS), lambda i, c, pt: (c, i, 0)),
                  pl.BlockSpec((1, nq, PAGE_ROWS), lambda i, c, pt: (n_pages, i, 0))]
                 + [page_a(g) for g in range(g_pages)]
                 + [page_b(g) for g in range(g_pages)],
        out_specs=[seq((1, nq, A_WIDTH)), seq((1, nq, B_WIDTH))],
        scratch_shapes=[pltpu.VMEM((rows_a, A_WIDTH), BF16), pltpu.VMEM((rows_b, B_WIDTH), BF16),
                        pltpu.VMEM((rows_a, 1), F32), pltpu.VMEM((rows_a, 1), F32),
                        pltpu.VMEM((rows_a, LANES), F32),
                        pltpu.VMEM((rows_b, 1), F32), pltpu.VMEM((rows_b, 1), F32),
                        pltpu.VMEM((rows_b, LANES), F32)],
    )
    return pl.pallas_call(
        functools.partial(_sample_attn_kernel, n_chunks=n_chunks, lam_init=lam_init),
        grid_spec=grid_spec,
        out_shape=[jax.ShapeDtypeStruct((s, nq, A_WIDTH), F32), jax.ShapeDtypeStruct((s, nq, B_WIDTH), F32)],
        compiler_params=_cparams(("arbitrary", "arbitrary")),
        name="sample_attn",
    )(page_table, lam4, g_sub, qa, qb, mask_a, mask_b, bias_a, bias_b, kva_new, kvb_new, sel, sel,
      *([cache_a] * g_pages), *([cache_bt] * g_pages))


def _outproj_kernel(oa_ref, ob_ref, sg_ref, x_ref, g1_ref, sh2_ref, sc2_ref, wba_ref, wbb_ref, wout_ref,
                    lng_ref, lnb_ref, wr_ref, br_ref, x1_ref, hc_ref, *, alpha):
    ya = _dot(oa_ref[...], wba_ref[...])
    yb = _dot(ob_ref[...], wbb_ref[...])
    t = sg_ref[:, :D_MODEL] * ya + sg_ref[:, D_MODEL:] * yb
    mix = _dot(t.astype(BF16), wout_ref[...])
    x1 = _layer_norm(alpha * x_ref[...] + g1_ref[0] * mix, lng_ref[...], lnb_ref[...])
    x1_ref[...] = x1
    h2 = x1 * (1.0 + sc2_ref[0]) + sh2_ref[0]
    hc_ref[:, :D_MODEL] = h2

    logits = _dot3(h2, wr_ref[...]) + br_ref[...]
    lane = lax.broadcasted_iota(jnp.int32, logits.shape, 1).astype(F32)
    big = float(LANES)
    is_group = jnp.logical_and(lane >= N_EXPERTS, lane < N_EXPERTS + N_GROUPS)
    lg = jnp.where(is_group, logits, NEG_INF)
    mg = jnp.max(lg, axis=-1, keepdims=True)
    g_sel = jnp.min(jnp.where(lg == mg, lane, big), axis=-1, keepdims=True) - N_EXPERTS
    p_g = 1.0 / jnp.sum(jnp.exp(lg - mg), axis=-1, keepdims=True)
    first = g_sel * EXPERTS_PER_GROUP
    in_group = jnp.logical_and(lane >= first, lane < first + EXPERTS_PER_GROUP)
    le = jnp.where(in_group, logits, NEG_INF)
    ex = jnp.exp(le - jnp.max(le, axis=-1, keepdims=True))
    pe = jnp.where(in_group, ex / jnp.sum(ex, axis=-1, keepdims=True), -1.0)
    v1 = jnp.max(pe, axis=-1, keepdims=True)
    i1 = jnp.min(jnp.where(pe == v1, lane, big), axis=-1, keepdims=True)
    pe2 = jnp.where(lane == i1, -1.0, pe)
    v2 = jnp.max(pe2, axis=-1, keepdims=True)
    i2 = jnp.min(jnp.where(pe2 == v2, lane, big), axis=-1, keepdims=True)
    tot = v1 + v2
    hc_ref[:, D_MODEL:] = (jnp.where(lane == i1, p_g * (v1 / tot), 0.0)
                           + jnp.where(lane == i2, p_g * (v2 / tot), 0.0)
                           + jnp.where(lane == GROUP_LANE, g_sel, 0.0))


def _out_proj(oa, ob, sg, x, ada3, wba, wbb, wout, lng, lnb, wr, br, rows_per_batch, alpha):
    n = x.shape[0]
    tm = ROW_TILE
    row = lambda w: pl.BlockSpec((tm, w), lambda i: (i, 0))
    full = lambda a: pl.BlockSpec(a.shape, lambda i: (0,) * a.ndim)
    return pl.pallas_call(
        functools.partial(_outproj_kernel, alpha=alpha),
        grid=(n // tm,),
        in_specs=[row(A_WIDTH), row(B_WIDTH), row(2 * D_MODEL), row(D_MODEL),
                  _row_vec_spec(ada3, 2, tm, rows_per_batch),
                  _row_vec_spec(ada3, 3, tm, rows_per_batch),
                  _row_vec_spec(ada3, 4, tm, rows_per_batch),
                  full(wba), full(wbb), full(wout), full(lng), full(lnb), full(wr), full(br)],
        out_specs=[row(D_MODEL), row(HC_WIDTH)],
        out_shape=[jax.ShapeDtypeStruct((n, D_MODEL), F32), jax.ShapeDtypeStruct((n, HC_WIDTH), F32)],
        compiler_params=_cparams(("arbitrary",)),
        name="out_proj",
    )(oa, ob, sg, x, ada3, ada3, ada3, wba, wbb, wout, lng, lnb, wr, br)


def _moe_route_kernel(hc_ref, tri_ref, upper_ref, slot_ref, tile_ref, cnt_ref, off_ref):
    phase, i = pl.program_id(0), pl.program_id(1)
    route = hc_ref[...]
    lane = lax.broadcasted_iota(jnp.int32, route.shape, 1)
    group = jnp.sum(jnp.where(lane == GROUP_LANE, route, 0.0), axis=-1, keepdims=True)
    onehot = jnp.where(lane.astype(F32) == group, 1.0, 0.0)
    col_sum = jnp.sum(onehot, axis=0, keepdims=True)

    @pl.when(jnp.logical_and(phase == 0, i == 0))
    def _():
        cnt_ref[...] = jnp.zeros(cnt_ref.shape, F32)

    @pl.when(phase == 0)
    def _():
        cnt_ref[0:1, :] += col_sum
        slot_ref[...] = jnp.zeros(slot_ref.shape, jnp.int32)

    @pl.when(jnp.logical_and(phase == 1, i == 0))
    def _():
        padded = jnp.ceil(cnt_ref[...] / MOE_ROWS) * MOE_ROWS
        start = _dot3(padded, upper_ref[...])
        off_ref[0] = start
        off_ref[1] = start + padded
        cnt_ref[...] = jnp.zeros(cnt_ref.shape, F32)
        sub = lax.broadcasted_iota(jnp.int32, cnt_ref.shape, 0)
        lane8 = lax.broadcasted_iota(jnp.int32, cnt_ref.shape, 1)
        lane1 = lax.broadcasted_iota(jnp.int32, (1, LANES), 1)
        tile_start = ((sub * LANES + lane8) * MOE_ROWS).astype(F32)
        ends = off_ref[1, 0:1, :]
        tile_group = jnp.zeros(cnt_ref.shape, F32)
        for g in range(N_GROUPS):
            end_g = jnp.sum(jnp.where(lane1 == g, ends, 0.0), axis=-1, keepdims=True)
            tile_group = tile_group + jnp.where(tile_start >= end_g, 1.0, 0.0)
        tile_ref[...] = tile_group.astype(jnp.int32)

    @pl.when(phase == 1)
    def _():
        earlier = _dot(tri_ref[...], onehot.astype(BF16))
        base = off_ref[0, 0:1, :] + cnt_ref[0:1, :]
        slot = jnp.sum(onehot * (base + earlier), axis=-1, keepdims=True)
        slot_ref[...] = jnp.broadcast_to(slot, route.shape).astype(jnp.int32)
        cnt_ref[0:1, :] += col_sum


def _moe_route(hc):
    n = hc.shape[0]
    tm = MOE_ROWS
    tri = (jnp.arange(tm)[:, None] > jnp.arange(tm)[None, :]).astype(BF16)
    upper = (jnp.arange(LANES)[:, None] < jnp.arange(LANES)[None, :]).astype(F32)
    return pl.pallas_call(
        _moe_route_kernel,
        grid=(2, n // tm),
        in_specs=[pl.BlockSpec((tm, LANES), lambda ph, i: (i, D_MODEL // LANES)),
                  pl.BlockSpec((tm, tm), lambda ph, i: (0, 0)),
                  pl.BlockSpec((LANES, LANES), lambda ph, i: (0, 0))],
        out_specs=[pl.BlockSpec((tm, LANES), lambda ph, i: (i * ph, 0)),
                   pl.BlockSpec((8, LANES), lambda ph, i: (0, 0))],
        out_shape=[jax.ShapeDtypeStruct((n, LANES), jnp.int32), jax.ShapeDtypeStruct((8, LANES), jnp.int32)],
        scratch_shapes=[pltpu.VMEM((8, LANES), F32), pltpu.VMEM((2, 8, LANES), F32)],
        compiler_params=_cparams(("arbitrary", "arbitrary")),
        name="moe_route",
    )(hc, tri, upper)


def _row_copies(n_rows, copy):
    lax.fori_loop(0, n_rows, lambda t, c: (copy(t).start(), c)[1], 0, unroll=8)
    lax.fori_loop(0, n_rows, lambda t, c: (copy(0).wait(), c)[1], 0, unroll=8)


def _moe_dispatch_kernel(slot_ref, hc_ref, init_ref, out_ref, sem):
    del init_ref
    copy = lambda t: pltpu.make_async_copy(hc_ref.at[pl.ds(t, 1)], out_ref.at[pl.ds(slot_ref[t], 1)], sem)
    _row_copies(hc_ref.shape[0], copy)


def _moe_dispatch(slot, hc, n_rows):
    n = hc.shape[0]
    tm = MOE_ROWS
    return pl.pallas_call(
        _moe_dispatch_kernel,
        grid=(n // tm,),
        in_specs=[pl.BlockSpec((tm,), lambda i: (i,), memory_space=pltpu.SMEM),
                  pl.BlockSpec((tm, HC_WIDTH), lambda i: (i, 0)),
                  pl.BlockSpec(memory_space=pl.ANY)],
        out_specs=pl.BlockSpec(memory_space=pl.ANY),
        out_shape=jax.ShapeDtypeStruct((n_rows, HC_WIDTH), F32),
        scratch_shapes=[pltpu.SemaphoreType.DMA(())],
        input_output_aliases={2: 0},
        compiler_params=_cparams(("arbitrary",)),
        name="moe_dispatch",
    )(slot, hc, jnp.zeros((n_rows, HC_WIDTH), F32))


def _moe_kernel(tile_ref, hc_ref, wup_ref, wdn_ref, o_ref, acc_ref):
    i, e = pl.program_id(0), pl.program_id(1)
    group = tile_ref[i]

    @pl.when(group < N_GROUPS)
    def _():
        @pl.when(e == 0)
        def _():
            acc_ref[...] = jnp.zeros(acc_ref.shape, F32)

        hid = _dot(hc_ref[:, :D_MODEL].astype(BF16), wup_ref[0])
        gate, up = hid[:, :EXPERT_HIDDEN], hid[:, EXPERT_HIDDEN:]
        act = (gate * _sigmoid(gate) * up).astype(BF16)
        y = _dot(act, wdn_ref[0])
        comb = hc_ref[:, D_MODEL:]
        lane = lax.broadcasted_iota(jnp.int32, comb.shape, 1)
        w = jnp.sum(jnp.where(lane == group * EXPERTS_PER_GROUP + e, comb, 0.0), axis=-1, keepdims=True)
        acc_ref[...] += w * y

        @pl.when(e == EXPERTS_PER_GROUP - 1)
        def _():
            o_ref[...] = acc_ref[...]

    @pl.when(jnp.logical_and(group >= N_GROUPS, e == 0))
    def _():
        o_ref[...] = jnp.zeros(o_ref.shape, F32)


def _moe_experts(tile_group, hc_sorted, wup, wdn):
    n_rows = hc_sorted.shape[0]
    tm = MOE_ROWS
    expert = lambda i, e, tg: (jnp.minimum(tg[i], N_GROUPS - 1) * EXPERTS_PER_GROUP + e, 0, 0)
    grid_spec = pltpu.PrefetchScalarGridSpec(
        num_scalar_prefetch=1,
        grid=(n_rows // tm, EXPERTS_PER_GROUP),
        in_specs=[pl.BlockSpec((tm, HC_WIDTH), lambda i, e, tg: (i, 0)),
                  pl.BlockSpec((1, D_MODEL, 2 * EXPERT_HIDDEN), expert),
                  pl.BlockSpec((1, EXPERT_HIDDEN, D_MODEL), expert)],
        out_specs=pl.BlockSpec((tm, D_MODEL), lambda i, e, tg: (i, 0)),
        scratch_shapes=[pltpu.VMEM((tm, D_MODEL), F32)],
    )
    return pl.pallas_call(
        _moe_kernel,
        grid_spec=grid_spec,
        out_shape=jax.ShapeDtypeStruct((n_rows, D_MODEL), F32),
        compiler_params=_cparams(("arbitrary", "arbitrary")),
        name="moe_experts",
    )(tile_group, hc_sorted, wup, wdn)


def _moe_combine_kernel(slot_ref, y_ref, x1_ref, g2_ref, lng_ref, lnb_ref, o_ref, buf_ref, sem, *, alpha):
    copy = lambda t: pltpu.make_async_copy(y_ref.at[pl.ds(slot_ref[t], 1)], buf_ref.at[pl.ds(t, 1)], sem)
    _row_copies(buf_ref.shape[0], copy)
    u = alpha * x1_ref[...] + g2_ref[0] * buf_ref[...]
    o_ref[...] = _layer_norm(u, lng_ref[...], lnb_ref[...])


def _moe_combine(slot, y_sorted, x1, ada3, lng, lnb, rows_per_batch, alpha):
    n = x1.shape[0]
    tm = MOE_ROWS
    full = lambda a: pl.BlockSpec(a.shape, lambda i: (0,) * a.ndim)
    return pl.pallas_call(
        functools.partial(_moe_combine_kernel, alpha=alpha),
        grid=(n // tm,),
        in_specs=[pl.BlockSpec((tm,), lambda i: (i,), memory_space=pltpu.SMEM),
                  pl.BlockSpec(memory_space=pl.ANY),
                  pl.BlockSpec((tm, D_MODEL), lambda i: (i, 0)),
                  _row_vec_spec(ada3, 5, tm, rows_per_batch),
                  full(lng), full(lnb)],
        out_specs=pl.BlockSpec((tm, D_MODEL), lambda i: (i, 0)),
        out_shape=jax.ShapeDtypeStruct((n, D_MODEL), F32),
        scratch_shapes=[pltpu.VMEM((tm, D_MODEL), F32), pltpu.SemaphoreType.DMA(())],
        compiler_params=_cparams(("arbitrary",)),
        name="moe_combine",
    )(slot, y_sorted, x1, ada3, lng, lnb)


def _moe(hc, x1, ada3, wup, wdn, lng, lnb, rows_per_batch, alpha):
    n = x1.shape[0]
    n_rows = n + N_GROUPS * MOE_ROWS
    slot_lanes, tile_table = _moe_route(hc)
    slot = slot_lanes[:, 0]
    tile_group = tile_table.reshape(-1)[:n_rows // MOE_ROWS]
    hc_sorted = _moe_dispatch(slot, hc, n_rows)
    y_sorted = _moe_experts(tile_group, hc_sorted, wup, wdn)
    return _moe_combine(slot, y_sorted, x1, ada3, lng, lnb, rows_per_batch, alpha)


def _prompt_dist():
    i = jnp.arange(TB, dtype=jnp.int32)[:, None]
    j = jnp.arange(TB, dtype=jnp.int32)[None, :]
    return jnp.concatenate([d * TB + i - j for d in range(-1, N_BIAS_TILES - 1)], axis=0)


def _sample_dist(nq, past_len):
    i = jnp.arange(nq, dtype=jnp.int32)[:, None]
    j = jnp.arange(PAGE_ROWS, dtype=jnp.int32)[None, :]
    last_page = past_len + i - (past_len - PAGE_ROWS + j)
    new = jnp.where(j < nq, i - j, -1)
    return jnp.concatenate([last_page, new], axis=1)


def kernel(x_prompt, x_sample, cache_kv_diff, cache_kv_dsa, cache_kidx, page_table, c_prompt, c_sample,
           rel_bias, w_ada, b_ada, w_in, lambda_q1, lambda_k1, lambda_q2, lambda_k2, subln_g, w_branch_a,
           w_branch_b, w_out, ln1_g, ln1_b, w_router_group, b_router_group, w_router_expert,
           b_router_expert, w_up, w_down, ln2_g, ln2_b):
    b, t, d = x_prompt.shape
    s, nq, _ = x_sample.shape
    depth = w_in.shape[0]
    n_pool = cache_kidx.shape[1]
    cache_a = cache_kv_diff.reshape(depth, n_pool, PAGE_ROWS * 2 * A_HEADS, 2 * HEAD_DIM)
    cache_bt = jnp.transpose(cache_kv_dsa, (0, 1, 3, 4, 5, 2))
    cache_it = jnp.transpose(cache_kidx, (0, 1, 3, 2))
    n_pages = page_table.shape[1]
    past_len = n_pages * PAGE_ROWS
    alpha = (2 * depth) ** 0.25
    topk_p = min(DSA_TOPK_MAX, t // 4)
    topk_s = min(DSA_TOPK_MAX, (past_len + nq) // 4)
    assert d == D_MODEL and t % TB == 0 and (s * nq) % ROW_TILE == 0 and n_pages % PAGES_PER_STEP == 0
    assert MOE_ROWS == ROW_TILE == TB and B_HEADS == 2 * N_CHAINS and A_HEADS % N_CHAINS == 0
    assert cache_kidx.shape[2] == PAGE_ROWS and nq <= 8

    bias_p = _bias_tiles(rel_bias, _prompt_dist()).reshape(A_HEADS + B_HEADS, N_BIAS_TILES, TB, TB)
    bias_s = _bias_tiles(rel_bias, _sample_dist(nq, past_len))
    bias_sa = jnp.broadcast_to(bias_s[:A_HEADS, None], (A_HEADS, 2, nq, 2 * PAGE_ROWS)).reshape(
        A_HEADS * 2 * nq, 2 * PAGE_ROWS)
    bias_sb = bias_s[A_HEADS:].reshape(B_HEADS * nq, 2 * PAGE_ROWS)
    tri_p = (jnp.arange(TB)[:, None] <= jnp.arange(TB)[None, :]).astype(BF16)
    tri_s = tri_p[:PAGE_ROWS, :PAGE_ROWS]
    lane_a = jnp.arange(A_WIDTH)[None, :] // HEAD_DIM
    mask_a = (lane_a == (jnp.arange(A_HEADS * 2 * nq)[:, None] // nq)).astype(F32)
    lane_b = jnp.arange(B_WIDTH)[None, :] // HEAD_DIM
    mask_b = (lane_b == (jnp.arange(B_HEADS * nq)[:, None] // nq)).astype(F32)

    xp = x_prompt.reshape(b * t, d)
    xs = x_sample.reshape(s * nq, d)
    c_all = jnp.concatenate([c_prompt, c_sample], axis=0)
    c_all = jnp.pad(c_all, ((0, -(b + s) % 8), (0, 0)))
    outs = [[] for _ in range(6)]
    for l in range(depth):
        lam_init = 0.8 - 0.6 * math.exp(-0.3 * l)
        lam4 = jnp.stack([lambda_q1[l], lambda_k1[l], lambda_q2[l], lambda_k2[l]]).astype(F32)
        g_sub = subln_g[l].reshape(1, 2 * HEAD_DIM)
        w = w_in[l]
        w2 = jnp.concatenate([w[:, :C_KI + IDX_DIM], w[:, C_KI:C_KI + IDX_DIM],
                              w[:, C_KI + IDX_DIM:C_KI + IDX_DIM + IDX_HEADS],
                              jnp.zeros((d, LANES - IDX_HEADS), w.dtype),
                              w[:, C_KI + IDX_DIM + IDX_HEADS:]], axis=1).astype(BF16)
        wba, wbb, wout = w_branch_a[l].astype(BF16), w_branch_b[l].astype(BF16), w_out[l].astype(BF16)
        wup, wdn = w_up[l].astype(BF16), w_down[l].astype(BF16)
        wr = jnp.concatenate([w_router_expert[l], w_router_group[l],
                              jnp.zeros((d, LANES - N_EXPERTS - N_GROUPS), F32)], axis=1)
        br = jnp.concatenate([b_router_expert[l], b_router_group[l],
                              jnp.zeros((LANES - N_EXPERTS - N_GROUPS,), F32)]).reshape(1, LANES)
        ln1 = (ln1_g[l].reshape(1, d), ln1_b[l].reshape(1, d))
        ln2 = (ln2_g[l].reshape(1, d), ln2_b[l].reshape(1, d))

        ada = _ada(c_all, w_ada[l], b_ada[l])
        ada_p = ada[:b].reshape(b, 1, 6 * d)
        ada_s = jnp.broadcast_to(ada[b:b + s, None], (s, nq, 6 * d)).reshape(s * nq // ROW_TILE, ROW_TILE, 6 * d)

        p = _in_proj(xp, ada_p, w2, t)
        oa = _diff_attn_prompt(lam4, g_sub, p["qa"], p["kva"], bias_p[:A_HEADS], b, t, lam_init)
        ob = _dsa_prompt(p, bias_p[A_HEADS:], tri_p, b, t, topk_p)
        x1, hc = _out_proj(oa, ob, p["sg"], xp, ada_p, wba, wbb, wout, *ln1, wr, br, t, alpha)
        xp = _moe(hc, x1, ada_p, wup, wdn, *ln2, t, alpha)
        outs[0].append(p["ra"].reshape(b, t, 2, A_HEADS, 2 * HEAD_DIM))
        outs[1].append(p["rb"].reshape(b, t, 2, B_HEADS, HEAD_DIM))
        outs[2].append(p["ki"].reshape(b, t, IDX_DIM))

        q = _in_proj(xs, ada_s, w2, nq)
        q_stack = q["qi"].reshape(s, nq, IDX_HEADS, IDX_DIM).transpose(0, 2, 1, 3).reshape(
            s, IDX_HEADS * nq, IDX_DIM)
        w_stack = q["wi"][:, :IDX_HEADS].reshape(s, nq, IDX_HEADS).transpose(0, 2, 1).reshape(
            s, IDX_HEADS * nq, 1)
        scores = _sample_idx(page_table, q_stack, w_stack, q["ki"].reshape(s, nq, IDX_DIM), cache_it, l)
        sel = _sample_select(scores, tri_s, topk_s)
        oa_s, ob_s = _sample_attn(
            page_table, lam4, g_sub,
            q["qa"].astype(F32).reshape(s, nq, A_WIDTH), q["qb"].astype(F32).reshape(s, nq, B_WIDTH),
            mask_a, mask_b, bias_sa, bias_sb,
            q["ra"].reshape(s, nq, 2 * A_WIDTH), q["rb"].reshape(s, nq, 2 * B_WIDTH), sel,
            cache_a, cache_bt, l, lam_init)
        x1, hc = _out_proj(oa_s.reshape(s * nq, A_WIDTH).astype(BF16),
                                 ob_s.reshape(s * nq, B_WIDTH).astype(BF16),
                                 q["sg"], xs, ada_s, wba, wbb, wout, *ln1, wr, br, nq, alpha)
        xs = _moe(hc, x1, ada_s, wup, wdn, *ln2, nq, alpha)
        outs[3].append(q["ra"].reshape(s, nq, 2, A_HEADS, 2 * HEAD_DIM))
        outs[4].append(q["rb"].reshape(s, nq, 2, B_HEADS, HEAD_DIM))
        outs[5].append(q["ki"].reshape(s, nq, IDX_DIM))

    return (xp.reshape(b, t, d), xs.reshape(s, nq, d)) + tuple(jnp.stack(o, 0) for o in outs)
```

```python
import functools
import math

import jax
import jax.numpy as jnp
from jax import lax
from jax.experimental import pallas as pl
from jax.experimental.pallas import tpu as pltpu

D_MODEL = 1024
HEAD_DIM = 64
A_HEADS = 8
B_HEADS = 8
IDX_HEADS = 8
IDX_DIM = 64
DSA_TOPK_MAX = 256
N_BUCKETS = 32
MAX_DISTANCE = 128
N_GROUPS = 4
EXPERTS_PER_GROUP = 4
N_EXPERTS = N_GROUPS * EXPERTS_PER_GROUP
EXPERT_HIDDEN = 512
LN_EPS = 1e-5

LANES = 128
TB = 256
ROW_TILE = 256
MOE_ROWS = 256
PAGE_ROWS = 128
PAGES_PER_STEP = 8
VMEM_LIMIT = 56 * 1024 * 1024

A_WIDTH = A_HEADS * 2 * HEAD_DIM
B_WIDTH = B_HEADS * HEAD_DIM
I_WIDTH = IDX_HEADS * IDX_DIM
C_QA, C_KVA, C_QB, C_KVB, C_QI, C_KI, C_WI, C_G = 0, 1024, 3072, 3584, 4608, 5120, 5248, 5376
W2_WIDTH = C_G + 2 * D_MODEL
HC_WIDTH = D_MODEL + LANES
GROUP_LANE = N_EXPERTS + N_GROUPS

F32 = jnp.float32
BF16 = jnp.bfloat16
NEG_INF = float("-inf")
INT_MIN = -2 ** 31


def _cparams(sem):
    return pltpu.CompilerParams(dimension_semantics=sem, vmem_limit_bytes=VMEM_LIMIT)


def _dot(a, b):
    return jnp.dot(a, b, preferred_element_type=F32)


def _dot_nt(a, b):
    return lax.dot_general(a, b, (((1,), (1,)), ((), ())), preferred_element_type=F32)


def _split(a):
    hi = a.astype(BF16)
    lo = (a - hi.astype(F32)).astype(BF16)
    return hi, lo


def _dot3(a, b):
    a_hi, a_lo = _split(a)
    b_hi, b_lo = _split(b)
    return _dot(a_hi, b_hi) + _dot(a_lo, b_hi) + _dot(a_hi, b_lo)


def _sigmoid(x):
    return 1.0 / (1.0 + jnp.exp(-x))


def _layer_norm(u, g, b):
    mu = jnp.mean(u, axis=-1, keepdims=True)
    d = u - mu
    var = jnp.mean(d * d, axis=-1, keepdims=True)
    return d * lax.rsqrt(var + LN_EPS) * g + b


def _bias_kernel(tab_ref, dist_ref, out_ref):
    h = pl.program_id(0)
    d = dist_ref[...]
    n = jnp.maximum(d, 0)
    max_exact = N_BUCKETS // 2
    nf = jnp.maximum(n, 1).astype(F32)
    large = max_exact + (jnp.log(nf / max_exact) / math.log(MAX_DISTANCE / max_exact)
                         * (N_BUCKETS - max_exact)).astype(jnp.int32)
    large = jnp.minimum(large, N_BUCKETS - 1)
    bucket = jnp.where(n < max_exact, n, large)
    last = tab_ref[N_BUCKETS - 1, h]
    acc = jnp.zeros(d.shape, F32)
    for m in range(N_BUCKETS - 1):
        acc = jnp.where(bucket == m, tab_ref[m, h] - last, acc)
    out_ref[0] = jnp.where(d < 0, NEG_INF, acc)


def _bias_tiles(rel_bias, dist):
    n_heads = rel_bias.shape[1]
    r, c = dist.shape
    return pl.pallas_call(
        _bias_kernel,
        grid=(n_heads,),
        in_specs=[pl.BlockSpec(memory_space=pltpu.SMEM),
                  pl.BlockSpec((r, c), lambda h: (0, 0))],
        out_specs=pl.BlockSpec((1, r, c), lambda h: (h, 0, 0)),
        out_shape=jax.ShapeDtypeStruct((n_heads, r, c), F32),
        compiler_params=_cparams(("arbitrary",)),
        name="bias_tiles",
    )(rel_bias, dist)


def _ada_kernel(c_ref, w_ref, b_ref, o_ref):
    c = c_ref[...]
    o_ref[...] = _dot3(c * _sigmoid(c), w_ref[...]) + b_ref[...]


def _ada(c_all, w_ada, b_ada):
    r, d = c_all.shape
    n = w_ada.shape[1]
    tn = 512
    return pl.pallas_call(
        _ada_kernel,
        grid=(n // tn,),
        in_specs=[pl.BlockSpec((r, d), lambda j: (0, 0)),
                  pl.BlockSpec((d, tn), lambda j: (0, j)),
                  pl.BlockSpec((1, tn), lambda j: (0, j))],
        out_specs=pl.BlockSpec((r, tn), lambda j: (0, j)),
        out_shape=jax.ShapeDtypeStruct((r, n), F32),
        compiler_params=_cparams(("arbitrary",)),
        name="ada",
    )(c_all, w_ada, b_ada.reshape(1, n))


def _inproj_kernel(x_ref, sh_ref, sc_ref, w_ref, qa_ref, ra_ref, kva_ref, qb_ref, rb_ref, kvb_ref,
                   qi_ref, ki_ref, kk_ref, wi_ref, sg_ref):
    h = (x_ref[...] * (1.0 + sc_ref[0]) + sh_ref[0]).astype(BF16)
    q_scale = HEAD_DIM ** -0.5

    def mm(c0, n):
        return _dot(h, w_ref[:, c0:c0 + n])

    for c in range(0, A_WIDTH, 512):
        qa_ref[:, c:c + 512] = (mm(C_QA + c, 512) * q_scale).astype(BF16)
    for c in range(0, 2 * A_WIDTH, 512):
        a = mm(C_KVA + c, 512)
        ra_ref[:, c:c + 512] = a
        kva_ref[:, c:c + 512] = a.astype(BF16)
    qb_ref[...] = (mm(C_QB, 512) * q_scale).astype(BF16)
    for c in range(0, 2 * B_WIDTH, 512):
        a = mm(C_KVB + c, 512)
        if len(rb_ref.shape) == 3:
            rb_ref[0, c:c + 512, :] = a.T
        else:
            rb_ref[:, c:c + 512] = a
        kvb_ref[:, c:c + 512] = a.astype(BF16)
    qi_ref[...] = (mm(C_QI, 512) * q_scale).astype(BF16)
    a = mm(C_KI, 2 * LANES)
    ki_ref[...] = a[:, :IDX_DIM]
    kk_ref[...] = a[:, :LANES].astype(BF16)
    wi_ref[...] = a[:, LANES:] * IDX_HEADS ** -0.5
    for c in range(0, 2 * D_MODEL, 512):
        sg_ref[:, c:c + 512] = _sigmoid(mm(C_G + c, 512))


def _row_vec_spec(arr, col, tm, rows_per_batch):
    if arr.shape[1] == 1:
        per = rows_per_batch // tm
        return pl.BlockSpec((1, 1, D_MODEL), lambda i, *_: (i // per, 0, col))
    return pl.BlockSpec((1, tm, D_MODEL), lambda i, *_: (i, 0, col))


def _in_proj(x, ada3, w2, rows_per_batch, rb_token_minor):
    n = x.shape[0]
    tm = ROW_TILE
    row = lambda w: pl.BlockSpec((tm, w), lambda i: (i, 0))
    outs = [("qa", A_WIDTH, BF16), ("ra", 2 * A_WIDTH, F32), ("kva", 2 * A_WIDTH, BF16),
            ("qb", B_WIDTH, BF16), ("rb", 2 * B_WIDTH, F32), ("kvb", 2 * B_WIDTH, BF16),
            ("qi", I_WIDTH, BF16), ("ki", IDX_DIM, F32), ("kk", LANES, BF16), ("wi", LANES, F32),
            ("sg", 2 * D_MODEL, F32)]
    out_specs = [row(w) for _, w, _ in outs]
    out_shape = [jax.ShapeDtypeStruct((n, w), dt) for _, w, dt in outs]
    if rb_token_minor:
        per = rows_per_batch // tm
        out_specs[4] = pl.BlockSpec((1, 2 * B_WIDTH, tm), lambda i: (i // per, 0, i % per))
        out_shape[4] = jax.ShapeDtypeStruct((n // rows_per_batch, 2 * B_WIDTH, rows_per_batch), F32)
    res = pl.pallas_call(
        _inproj_kernel,
        grid=(n // tm,),
        in_specs=[row(D_MODEL),
                  _row_vec_spec(ada3, 0, tm, rows_per_batch),
                  _row_vec_spec(ada3, 1, tm, rows_per_batch),
                  pl.BlockSpec((D_MODEL, W2_WIDTH), lambda i: (0, 0), pipeline_mode=pl.Buffered(1))],
        out_specs=out_specs,
        out_shape=out_shape,
        compiler_params=_cparams(("arbitrary",)),
        name="in_proj",
    )(x, ada3, ada3, w2)
    return {name: r for (name, _, _), r in zip(outs, res)}


def _stack2(x):
    return jnp.concatenate([x, x], axis=0)


LOG2_E = 1.4426950408889634
N_CHAINS = 4
N_BIAS_TILES = 4


def _bias_index(tile_distance):
    return jnp.clip(tile_distance, -1, N_BIAS_TILES - 2) + 1


def _lane_fold(x, op):
    r = x[:, :LANES]
    for c in range(1, x.shape[1] // LANES):
        r = op(r, x[:, c * LANES:(c + 1) * LANES])
    return r


def _two_pass_attend(q_ref, k_ref, v_ref, nt, near_bias, every_bias, s_ref, st_ref):
    rows = 2 * TB
    st_ref[0] = jnp.full((N_CHAINS, rows, LANES), NEG_INF, F32)
    st_ref[1] = jnp.zeros((N_CHAINS, rows, LANES), F32)
    st_ref[2] = jnp.zeros((N_CHAINS, rows, LANES), F32)

    def tiles(jj):
        return [(2 * jj + u, jnp.minimum(2 * jj + u, nt - 1)) for u in range(2)]

    def rows_of(ref, c, pair):
        return jnp.concatenate([ref[pl.ds(pl.multiple_of(jc * TB, TB), TB), c * LANES:(c + 1) * LANES]
                                for _, jc in pair], axis=0)

    def first(near):
        def body(jj, _):
            pair = tiles(jj)
            every = None
            if every_bias is not None:
                every = _stack2(jnp.concatenate([every_bias(jc) for _, jc in pair], axis=1))
            for c in range(N_CHAINS):
                s = _dot_nt(_masked_pair(q_ref[:, c * LANES:(c + 1) * LANES]), rows_of(k_ref, c, pair))
                if near:
                    s = s + jnp.concatenate([near_bias(c, j) for j, _ in pair], axis=1)
                if every is not None:
                    s = s + every
                s = s * LOG2_E
                s_ref[jj, c] = s
                st_ref[0, c] = jnp.maximum(st_ref[0, c], _lane_fold(s, jnp.maximum))
            return 0
        return body

    n_steps = (nt + 1) // 2
    n_far = jnp.maximum(nt - 2, 0) // 2
    lax.fori_loop(0, n_far, first(False), 0)
    lax.fori_loop(n_far, n_steps, first(True), 0)
    for c in range(N_CHAINS):
        st_ref[0, c] = jnp.broadcast_to(jnp.max(st_ref[0, c], axis=-1, keepdims=True), (rows, LANES))

    def second(jj, _):
        pair = tiles(jj)
        for c in range(N_CHAINS):
            s = s_ref[jj, c]
            mb = st_ref[0, c]
            p = [jnp.exp2(s[:, i * LANES:(i + 1) * LANES] - mb) for i in range(2 * TB // LANES)]
            st_ref[1, c] += (p[0] + p[1]) + (p[2] + p[3])
            st_ref[2, c] += _dot(jnp.concatenate(p, axis=1).astype(BF16), rows_of(v_ref, c, pair))
        return 0

    lax.fori_loop(0, n_steps, second, 0)
    return lambda c: (st_ref[2, c], jnp.sum(st_ref[1, c], axis=-1, keepdims=True))


def _lambda_value(lam_ref, lam_init):
    a = jnp.sum(lam_ref[0:1, :] * lam_ref[1:2, :], axis=-1, keepdims=True)
    b = jnp.sum(lam_ref[2:3, :] * lam_ref[3:4, :], axis=-1, keepdims=True)
    return jnp.exp(a) - jnp.exp(b) + lam_init


def _sub_layer_norm(o, g, lam_init):
    o = o * lax.rsqrt(jnp.mean(o * o, axis=-1, keepdims=True) + LN_EPS)
    return o * g * (1.0 - lam_init)


def _masked_pair(q):
    lo = lax.broadcasted_iota(jnp.int32, q.shape, 1) < HEAD_DIM
    zero = jnp.zeros_like(q)
    return jnp.concatenate([jnp.where(lo, q, zero), jnp.where(lo, zero, q)], axis=0)


def _attend_scratch(nq):
    return [pltpu.VMEM(((nq + 1) // 2, N_CHAINS, 2 * TB, 2 * TB), F32),
            pltpu.VMEM((3, N_CHAINS, 2 * TB, LANES), F32)]


def _diff_attn_kernel(lam_ref, g_ref, q_ref, k_ref, v_ref, bias_ref, o_ref, s_ref, st_ref, *, lam_init):
    qi = pl.program_id(2)
    near = lambda c, j: _stack2(bias_ref[c, _bias_index(qi - j)])
    result = _two_pass_attend(q_ref, k_ref, v_ref, qi + 1, near, None, s_ref, st_ref)
    lam = _lambda_value(lam_ref, lam_init)
    for c in range(N_CHAINS):
        acc, l = result(c)
        o = acc / l
        o = o[:TB] - lam * o[TB:]
        o_ref[:, c * LANES:(c + 1) * LANES] = _sub_layer_norm(o, g_ref[...], lam_init).astype(BF16)


def _diff_attn_prompt(lam4, g_sub, qa, kva, bias_p, b, t, lam_init):
    nq = t // TB
    groups = A_HEADS // N_CHAINS
    width = N_CHAINS * LANES
    return pl.pallas_call(
        functools.partial(_diff_attn_kernel, lam_init=lam_init),
        grid=(b, groups, nq),
        in_specs=[pl.BlockSpec((4, HEAD_DIM), lambda bi, h, i: (0, 0)),
                  pl.BlockSpec((1, 2 * HEAD_DIM), lambda bi, h, i: (0, 0)),
                  pl.BlockSpec((TB, width), lambda bi, h, i: (bi * nq + i, h)),
                  pl.BlockSpec((t, width), lambda bi, h, i: (bi, h)),
                  pl.BlockSpec((t, width), lambda bi, h, i: (bi, groups + h)),
                  pl.BlockSpec((N_CHAINS, N_BIAS_TILES, TB, TB), lambda bi, h, i: (h, 0, 0, 0))],
        out_specs=pl.BlockSpec((TB, width), lambda bi, h, i: (bi * nq + i, h)),
        out_shape=jax.ShapeDtypeStruct((b * t, A_WIDTH), BF16),
        scratch_shapes=_attend_scratch(nq),
        compiler_params=_cparams(("arbitrary", "arbitrary", "arbitrary")),
        name="diff_attn_prompt",
    )(lam4, g_sub, qa, kva, kva, bias_p)


KEY_NEG_INF = INT_MIN + 0x7FFFFF


def _key_to_float(key):
    bits = jnp.where(key < 0, key ^ jnp.int32(0x7FFFFFFF), key)
    return jnp.where(key < KEY_NEG_INF, NEG_INF, lax.bitcast_convert_type(bits, F32))


SELECT_ROW_GROUPS = 2


def _topk_select(sc_ref, nt, topk, tri_ref):
    rows, tw = sc_ref.shape[1], sc_ref.shape[2]
    rg = rows // SELECT_ROW_GROUPS
    groups = [slice(g * rg, (g + 1) * rg) for g in range(SELECT_ROW_GROUPS)]
    kf = float(topk)

    def count_ge(rs, cf):
        acc = None
        for j in range(nt):
            r = _lane_fold(jnp.where(sc_ref[j, rs, :] >= cf, 1.0, 0.0), jnp.add)
            acc = r if acc is None else acc + r
        return jnp.sum(acc, axis=-1, keepdims=True)

    def search(p, ts):
        inc = lax.shift_left(jnp.int32(1), jnp.int32(31) - p)
        out = []
        for rs, t in zip(groups, ts):
            cand = t + inc
            out.append(jnp.where(count_ge(rs, _key_to_float(cand)) >= kf, cand, t))
        return tuple(out)

    t0 = jnp.full((rg, 1), INT_MIN, jnp.int32)
    ts = lax.fori_loop(0, 32, search, (t0,) * SELECT_ROW_GROUPS)
    tri = tri_ref[...]
    for rs, t in zip(groups, ts):
        t_lo = _key_to_float(t)
        t_hi = _key_to_float(t + 1)
        need = kf - count_ge(rs, t_hi)
        c = jnp.zeros((rg, 1), F32)
        for j in range(nt):
            s = sc_ref[j, rs, :]
            gt = s >= t_hi
            eq = jnp.logical_and(s >= t_lo, jnp.logical_not(gt))
            e = jnp.where(eq, 1.0, 0.0)
            rank = _dot(e.astype(BF16), tri) + c
            sel = jnp.logical_or(gt, jnp.logical_and(eq, rank <= need))
            sel = jnp.logical_and(sel, s > NEG_INF)
            sc_ref[j, rs, :] = jnp.where(sel, 0.0, NEG_INF)
            c = c + jnp.sum(e, axis=-1, keepdims=True)


def _dsa_kernel(qi_ref, kk_ref, wi_ref, qb_ref, kb_ref, vb_ref, bias_ref, tri_ref, o_ref, sc_ref, s_ref,
                st_ref, *, topk):
    qblk = pl.program_id(1)
    nt = qblk + 1
    tq = qi_ref.shape[0]
    lo = lax.broadcasted_iota(jnp.int32, (tq, LANES), 1) < HEAD_DIM
    zero = jnp.zeros((tq, LANES), BF16)

    def halves(qp):
        return jnp.where(lo, qp, zero), jnp.where(lo, zero, qp)

    wi = wi_ref[...]

    def index_tile(j, _):
        off = pl.multiple_of(j * TB, TB)
        kk = kk_ref[pl.ds(off, TB), :]
        acc = jnp.zeros((tq, TB), F32)
        for m in range(IDX_HEADS // 2):
            q_lo, q_hi = halves(qi_ref[:, m * LANES:(m + 1) * LANES])
            acc = acc + wi[:, 2 * m:2 * m + 1] * jnp.maximum(_dot_nt(q_lo, kk), 0.0)
            acc = acc + wi[:, 2 * m + 1:2 * m + 2] * jnp.maximum(_dot_nt(q_hi, kk), 0.0)
        row = lax.broadcasted_iota(jnp.int32, (tq, TB), 0) + qblk * TB
        col = lax.broadcasted_iota(jnp.int32, (tq, TB), 1) + j * TB
        sc_ref[j] = jnp.where(col <= row, acc, NEG_INF)
        return 0

    lax.fori_loop(0, nt, index_tile, 0)
    for n_tiles in range(1, sc_ref.shape[0] + 1):
        pl.when(nt == n_tiles)(functools.partial(_topk_select, sc_ref, n_tiles, topk, tri_ref))

    def near(m, j):
        d = _bias_index(qblk - j)
        return jnp.concatenate([bias_ref[2 * m, d], bias_ref[2 * m + 1, d]], axis=0)

    result = _two_pass_attend(qb_ref, kb_ref, vb_ref, nt, near, lambda jc: sc_ref[jc], s_ref, st_ref)
    lo_out = lax.broadcasted_iota(jnp.int32, (tq, LANES), 1) < HEAD_DIM
    for m in range(N_CHAINS):
        acc, l = result(m)
        o = acc / l
        o_ref[:, m * LANES:(m + 1) * LANES] = jnp.where(lo_out, o[:tq], o[tq:]).astype(BF16)


def _dsa_prompt(p, bias_b, tri, b, t, topk):
    nq = t // TB
    return pl.pallas_call(
        functools.partial(_dsa_kernel, topk=topk),
        grid=(b, nq),
        in_specs=[pl.BlockSpec((TB, I_WIDTH), lambda bi, i: (bi * nq + i, 0)),
                  pl.BlockSpec((t, LANES), lambda bi, i: (bi, 0)),
                  pl.BlockSpec((TB, LANES), lambda bi, i: (bi * nq + i, 0)),
                  pl.BlockSpec((TB, B_WIDTH), lambda bi, i: (bi * nq + i, 0)),
                  pl.BlockSpec((t, B_WIDTH), lambda bi, i: (bi, 0)),
                  pl.BlockSpec((t, B_WIDTH), lambda bi, i: (bi, 1)),
                  pl.BlockSpec((B_HEADS, N_BIAS_TILES, TB, TB), lambda bi, i: (0, 0, 0, 0),
                               pipeline_mode=pl.Buffered(1)),
                  pl.BlockSpec((TB, TB), lambda bi, i: (0, 0))],
        out_specs=pl.BlockSpec((TB, B_WIDTH), lambda bi, i: (bi * nq + i, 0)),
        out_shape=jax.ShapeDtypeStruct((b * t, B_WIDTH), BF16),
        scratch_shapes=[pltpu.VMEM((nq, TB, TB), F32)] + _attend_scratch(nq),
        compiler_params=_cparams(("arbitrary", "arbitrary")),
        name="dsa_prompt",
    )(p["qi"], p["kk"], p["wi"], p["qb"], p["kvb"], p["kvb"], bias_b, tri)


def _sample_idx_kernel(pt_ref, q_ref, w_ref, kn_ref, *rest, n_pages):
    page_refs, o_ref = rest[:n_pages], rest[n_pages]
    q = q_ref[0]
    w = w_ref[0]
    nq = q.shape[0] // IDX_HEADS

    def combine(qk):
        rel = jnp.maximum(qk, 0.0) * w
        sc = rel[0:nq]
        for h in range(1, IDX_HEADS):
            sc = sc + rel[h * nq:(h + 1) * nq]
        return sc

    for k in range(n_pages):
        o_ref[k] = combine(_dot(q, page_refs[k][0, 0].astype(BF16)))
    new = jnp.concatenate([kn_ref[0], jnp.zeros((PAGE_ROWS - nq, IDX_DIM), F32)], axis=0).astype(BF16)
    row = lax.broadcasted_iota(jnp.int32, (nq, PAGE_ROWS), 0)
    col = lax.broadcasted_iota(jnp.int32, (nq, PAGE_ROWS), 1)
    o_ref[n_pages] = jnp.where(col <= row, combine(_dot_nt(q, new)), NEG_INF)


def _sample_idx(page_table, q_stack, w_stack, ki_new, cache_kidx_t, layer):
    s, n_pages = page_table.shape
    nq = ki_new.shape[1]
    hq = q_stack.shape[1]
    page_spec = lambda k: pl.BlockSpec((1, 1, IDX_DIM, PAGE_ROWS), lambda i, pt, k=k: (layer, pt[i, k], 0, 0))
    grid_spec = pltpu.PrefetchScalarGridSpec(
        num_scalar_prefetch=1,
        grid=(s,),
        in_specs=[pl.BlockSpec((1, hq, IDX_DIM), lambda i, pt: (i, 0, 0)),
                  pl.BlockSpec((1, hq, 1), lambda i, pt: (i, 0, 0)),
                  pl.BlockSpec((1, nq, IDX_DIM), lambda i, pt: (i, 0, 0))]
                 + [page_spec(k) for k in range(n_pages)],
        out_specs=pl.BlockSpec((n_pages + 1, nq, PAGE_ROWS), lambda i, pt: (0, i, 0)),
    )
    return pl.pallas_call(
        functools.partial(_sample_idx_kernel, n_pages=n_pages),
        grid_spec=grid_spec,
        out_shape=jax.ShapeDtypeStruct((n_pages + 1, s * nq, PAGE_ROWS), F32),
        compiler_params=_cparams(("arbitrary",)),
        name="sample_idx",
    )(page_table, q_stack, w_stack, ki_new, *([cache_kidx_t] * n_pages))


def _select_kernel(sc_ref, tri_ref, o_ref, *, topk):
    o_ref[...] = sc_ref[...]
    _topk_select(o_ref, o_ref.shape[0], topk, tri_ref)


def _sample_select(scores, tri, topk):
    nt, rows, tw = scores.shape
    tr = min(rows, 256)
    return pl.pallas_call(
        functools.partial(_select_kernel, topk=topk),
        grid=(rows // tr,),
        in_specs=[pl.BlockSpec((nt, tr, tw), lambda i: (0, i, 0)),
                  pl.BlockSpec((tw, tw), lambda i: (0, 0))],
        out_specs=pl.BlockSpec((nt, tr, tw), lambda i: (0, i, 0)),
        out_shape=jax.ShapeDtypeStruct(scores.shape, F32),
        compiler_params=_cparams(("arbitrary",)),
        name="sample_select",
    )(scores, tri)


def _sample_attn_kernel(pt_ref, lam_ref, g_ref, qa_ref, qb_ref, mska_ref, mskb_ref, biasa_ref, biasb_ref,
                        kvan_ref, kvbn_ref, selp_ref, seln_ref, *rest, n_chunks, lam_init):
    g_pages = PAGES_PER_STEP
    kva_refs, kvb_refs = rest[:g_pages], rest[g_pages:2 * g_pages]
    oa_ref, ob_ref = rest[2 * g_pages], rest[2 * g_pages + 1]
    qa_s, qb_s, ma_s, la_s, acca_s, mb_s, lb_s, accb_s = rest[2 * g_pages + 2:]
    c = pl.program_id(1)
    nq = qa_ref.shape[1]
    rows_a = A_HEADS * 2 * nq
    rows_b = B_HEADS * nq

    @pl.when(c == 0)
    def _():
        qa_s[...] = (jnp.concatenate([qa_ref[0]] * (2 * A_HEADS), axis=0) * mska_ref[...]).astype(BF16)
        qb_s[...] = (jnp.concatenate([qb_ref[0]] * B_HEADS, axis=0) * mskb_ref[...]).astype(BF16)
        ma_s[...] = jnp.full(ma_s.shape, NEG_INF, F32)
        mb_s[...] = jnp.full(mb_s.shape, NEG_INF, F32)
        la_s[...] = jnp.zeros(la_s.shape, F32)
        lb_s[...] = jnp.zeros(lb_s.shape, F32)
        acca_s[...] = jnp.zeros(acca_s.shape, F32)
        accb_s[...] = jnp.zeros(accb_s.shape, F32)

    def diag_a(r):
        return jnp.concatenate(
            [r[h * 2 * nq:(h + 1) * 2 * nq, h * LANES:(h + 1) * LANES] for h in range(A_HEADS)], axis=0)

    def diag_b(r):
        return jnp.concatenate(
            [r[m * 2 * nq:(m + 1) * 2 * nq, m * LANES:(m + 1) * LANES] for m in range(B_HEADS // 2)], axis=0)

    def update_a(s, values):
        mn = jnp.maximum(ma_s[...], jnp.max(s, axis=-1, keepdims=True))
        p = jnp.exp(s - mn)
        al = jnp.exp(ma_s[...] - mn)
        la_s[...] = al * la_s[...] + jnp.sum(p, axis=-1, keepdims=True)
        acca_s[...] = al * acca_s[...] + diag_a(values(p.astype(BF16)))
        ma_s[...] = mn

    def update_b(s, values):
        mn = jnp.maximum(mb_s[...], jnp.max(s, axis=-1, keepdims=True))
        ms = jnp.where(mn == NEG_INF, 0.0, mn)
        p = jnp.exp(s - ms)
        al = jnp.exp(mb_s[...] - ms)
        lb_s[...] = al * lb_s[...] + jnp.sum(p, axis=-1, keepdims=True)
        accb_s[...] = al * accb_s[...] + diag_b(values(p.astype(BF16)))
        mb_s[...] = mn

    def tile_rows(x, n):
        return jnp.concatenate([x] * n, axis=0)

    def gather_a(first):
        return jnp.concatenate(
            [jnp.concatenate([ref[0, 0, pl.ds(first + h, PAGE_ROWS, stride=2 * A_HEADS), :]
                              for h in range(A_HEADS)], axis=1) for ref in kva_refs], axis=0).astype(BF16)

    last = c == n_chunks - 1
    far = (g_pages - 1) * PAGE_ROWS
    bias_a = jnp.concatenate([jnp.zeros((rows_a, far), F32), jnp.where(last, biasa_ref[:, :PAGE_ROWS], 0.0)], axis=1)
    bias_b = jnp.concatenate([jnp.zeros((rows_b, far), F32), jnp.where(last, biasb_ref[:, :PAGE_ROWS], 0.0)], axis=1)

    ka = gather_a(0)
    va = gather_a(A_HEADS)
    update_a(_dot_nt(qa_s[...], ka) + bias_a, lambda p: _dot(p, va))

    kt = jnp.concatenate([ref[0, 0, 0].reshape(B_WIDTH, PAGE_ROWS) for ref in kvb_refs], axis=1).astype(BF16)
    vt = jnp.concatenate([ref[0, 0, 1].reshape(B_WIDTH, PAGE_ROWS) for ref in kvb_refs], axis=1).astype(BF16)
    sel = jnp.concatenate([tile_rows(selp_ref[g], B_HEADS) for g in range(g_pages)], axis=1)
    update_b(_dot(qb_s[...], kt) + bias_b + sel, lambda p: _dot_nt(p, vt))

    @pl.when(last)
    def _():
        pad = lambda x: jnp.concatenate([x, jnp.zeros((PAGE_ROWS - nq, x.shape[1]), F32)], axis=0).astype(BF16)
        kn, vn = pad(kvan_ref[0, :, :A_WIDTH]), pad(kvan_ref[0, :, A_WIDTH:])
        update_a(_dot_nt(qa_s[...], kn) + biasa_ref[:, PAGE_ROWS:], lambda p: _dot(p, vn))
        kn, vn = pad(kvbn_ref[0, :, :B_WIDTH]), pad(kvbn_ref[0, :, B_WIDTH:])
        update_b(_dot_nt(qb_s[...], kn) + biasb_ref[:, PAGE_ROWS:] + tile_rows(seln_ref[0], B_HEADS),
                 lambda p: _dot(p, vn))

        lam = _lambda_value(lam_ref, lam_init)
        oa = acca_s[...] / la_s[...]
        for h in range(A_HEADS):
            o = oa[h * 2 * nq:h * 2 * nq + nq] - lam * oa[h * 2 * nq + nq:(h + 1) * 2 * nq]
            oa_ref[0, :, h * LANES:(h + 1) * LANES] = _sub_layer_norm(o, g_ref[...], lam_init)
        ob = accb_s[...] / lb_s[...]
        lo = lax.broadcasted_iota(jnp.int32, (nq, LANES), 1) < HEAD_DIM
        for m in range(B_HEADS // 2):
            ob_ref[0, :, m * LANES:(m + 1) * LANES] = jnp.where(
                lo, ob[2 * m * nq:(2 * m + 1) * nq], ob[(2 * m + 1) * nq:(2 * m + 2) * nq])


def _sample_attn(page_table, lam4, g_sub, qa, qb, mask_a, mask_b, bias_a, bias_b, kva_new, kvb_new, sel,
                 cache_a, cache_bt, layer, lam_init):
    s, n_pages = page_table.shape
    nq = qa.shape[1]
    g_pages = PAGES_PER_STEP
    n_chunks = n_pages // g_pages
    rows_a, rows_b = A_HEADS * 2 * nq, B_HEADS * nq
    const = lambda shape: pl.BlockSpec(shape, lambda i, c, pt: (0,) * len(shape))
    seq = lambda shape: pl.BlockSpec(shape, lambda i, c, pt: (i,) + (0,) * (len(shape) - 1))
    page_a = lambda g: pl.BlockSpec((1, 1) + cache_a.shape[2:],
                                    lambda i, c, pt, g=g: (layer, pt[i, c * g_pages + g], 0, 0))
    page_b = lambda g: pl.BlockSpec((1, 1) + cache_bt.shape[2:],
                                    lambda i, c, pt, g=g: (layer, pt[i, c * g_pages + g], 0, 0, 0, 0))
    grid_spec = pltpu.PrefetchScalarGridSpec(
        num_scalar_prefetch=1,
        grid=(s, n_chunks),
        in_specs=[const((4, HEAD_DIM)), const((1, 2 * HEAD_DIM)),
                  seq((1, nq, A_WIDTH)), seq((1, nq, B_WIDTH)),
                  const((rows_a, A_WIDTH)), const((rows_b, B_WIDTH)),
                  const((rows_a, 2 * PAGE_ROWS)), const((rows_b, 2 * PAGE_ROWS)),
                  seq((1, nq, 2 * A_WIDTH)), seq((1, nq, 2 * B_WIDTH)),
                  pl.BlockSpec((g_pages, nq, PAGE_ROWS), lambda i, c, pt: (c, i, 0)),
                  pl.BlockSpec((1, nq, PAGE_ROWS), lambda i, c, pt: (n_pages, i, 0))]
                 + [page_a(g) for g in range(g_pages)]
                 + [page_b(g) for g in range(g_pages)],
        out_specs=[seq((1, nq, A_WIDTH)), seq((1, nq, B_WIDTH))],
        scratch_shapes=[pltpu.VMEM((rows_a, A_WIDTH), BF16), pltpu.VMEM((rows_b, B_WIDTH), BF16),
                        pltpu.VMEM((rows_a, 1), F32), pltpu.VMEM((rows_a, 1), F32),
                        pltpu.VMEM((rows_a, LANES), F32),
                        pltpu.VMEM((rows_b, 1), F32), pltpu.VMEM((rows_b, 1), F32),
                        pltpu.VMEM((rows_b, LANES), F32)],
    )
    return pl.pallas_call(
        functools.partial(_sample_attn_kernel, n_chunks=n_chunks, lam_init=lam_init),
        grid_spec=grid_spec,
        out_shape=[jax.ShapeDtypeStruct((s, nq, A_WIDTH), F32), jax.ShapeDtypeStruct((s, nq, B_WIDTH), F32)],
        compiler_params=_cparams(("arbitrary", "arbitrary")),
        name="sample_attn",
    )(page_table, lam4, g_sub, qa, qb, mask_a, mask_b, bias_a, bias_b, kva_new, kvb_new, sel, sel,
      *([cache_a] * g_pages), *([cache_bt] * g_pages))


def _outproj_kernel(oa_ref, ob_ref, sg_ref, x_ref, g1_ref, sh2_ref, sc2_ref, wba_ref, wbb_ref, wout_ref,
                    lng_ref, lnb_ref, wr_ref, br_ref, x1_ref, hc_ref, *, alpha):
    ya = _dot(oa_ref[...], wba_ref[...])
    yb = _dot(ob_ref[...], wbb_ref[...])
    t = sg_ref[:, :D_MODEL] * ya + sg_ref[:, D_MODEL:] * yb
    mix = _dot(t.astype(BF16), wout_ref[...])
    x1 = _layer_norm(alpha * x_ref[...] + g1_ref[0] * mix, lng_ref[...], lnb_ref[...])
    x1_ref[...] = x1
    h2 = x1 * (1.0 + sc2_ref[0]) + sh2_ref[0]
    hc_ref[:, :D_MODEL] = h2

    logits = _dot3(h2, wr_ref[...]) + br_ref[...]
    lane = lax.broadcasted_iota(jnp.int32, logits.shape, 1).astype(F32)
    big = float(LANES)
    is_group = jnp.logical_and(lane >= N_EXPERTS, lane < N_EXPERTS + N_GROUPS)
    lg = jnp.where(is_group, logits, NEG_INF)
    mg = jnp.max(lg, axis=-1, keepdims=True)
    g_sel = jnp.min(jnp.where(lg == mg, lane, big), axis=-1, keepdims=True) - N_EXPERTS
    p_g = 1.0 / jnp.sum(jnp.exp(lg - mg), axis=-1, keepdims=True)
    first = g_sel * EXPERTS_PER_GROUP
    in_group = jnp.logical_and(lane >= first, lane < first + EXPERTS_PER_GROUP)
    le = jnp.where(in_group, logits, NEG_INF)
    ex = jnp.exp(le - jnp.max(le, axis=-1, keepdims=True))
    pe = jnp.where(in_group, ex / jnp.sum(ex, axis=-1, keepdims=True), -1.0)
    v1 = jnp.max(pe, axis=-1, keepdims=True)
    i1 = jnp.min(jnp.where(pe == v1, lane, big), axis=-1, keepdims=True)
    pe2 = jnp.where(lane == i1, -1.0, pe)
    v2 = jnp.max(pe2, axis=-1, keepdims=True)
    i2 = jnp.min(jnp.where(pe2 == v2, lane, big), axis=-1, keepdims=True)
    tot = v1 + v2
    hc_ref[:, D_MODEL:] = (jnp.where(lane == i1, p_g * (v1 / tot), 0.0)
                           + jnp.where(lane == i2, p_g * (v2 / tot), 0.0)
                           + jnp.where(lane == GROUP_LANE, g_sel, 0.0))


def _out_proj(oa, ob, sg, x, ada3, wba, wbb, wout, lng, lnb, wr, br, rows_per_batch, alpha):
    n = x.shape[0]
    tm = ROW_TILE
    row = lambda w: pl.BlockSpec((tm, w), lambda i: (i, 0))
    full = lambda a: pl.BlockSpec(a.shape, lambda i: (0,) * a.ndim)
    return pl.pallas_call(
        functools.partial(_outproj_kernel, alpha=alpha),
        grid=(n // tm,),
        in_specs=[row(A_WIDTH), row(B_WIDTH), row(2 * D_MODEL), row(D_MODEL),
                  _row_vec_spec(ada3, 2, tm, rows_per_batch),
                  _row_vec_spec(ada3, 3, tm, rows_per_batch),
                  _row_vec_spec(ada3, 4, tm, rows_per_batch),
                  full(wba), full(wbb), full(wout), full(lng), full(lnb), full(wr), full(br)],
        out_specs=[row(D_MODEL), row(HC_WIDTH)],
        out_shape=[jax.ShapeDtypeStruct((n, D_MODEL), F32), jax.ShapeDtypeStruct((n, HC_WIDTH), F32)],
        compiler_params=_cparams(("arbitrary",)),
        name="out_proj",
    )(oa, ob, sg, x, ada3, ada3, ada3, wba, wbb, wout, lng, lnb, wr, br)


def _moe_route_kernel(hc_ref, tri_ref, upper_ref, slot_ref, tile_ref, cnt_ref, off_ref):
    phase, i = pl.program_id(0), pl.program_id(1)
    route = hc_ref[...]
    lane = lax.broadcasted_iota(jnp.int32, route.shape, 1)
    group = jnp.sum(jnp.where(lane == GROUP_LANE, route, 0.0), axis=-1, keepdims=True)
    onehot = jnp.where(lane.astype(F32) == group, 1.0, 0.0)
    col_sum = jnp.sum(onehot, axis=0, keepdims=True)

    @pl.when(jnp.logical_and(phase == 0, i == 0))
    def _():
        cnt_ref[...] = jnp.zeros(cnt_ref.shape, F32)

    @pl.when(phase == 0)
    def _():
        cnt_ref[0:1, :] += col_sum
        slot_ref[...] = jnp.zeros(slot_ref.shape, jnp.int32)

    @pl.when(jnp.logical_and(phase == 1, i == 0))
    def _():
        padded = jnp.ceil(cnt_ref[...] / MOE_ROWS) * MOE_ROWS
        start = _dot3(padded, upper_ref[...])
        off_ref[0] = start
        off_ref[1] = start + padded
        cnt_ref[...] = jnp.zeros(cnt_ref.shape, F32)
        sub = lax.broadcasted_iota(jnp.int32, cnt_ref.shape, 0)
        lane8 = lax.broadcasted_iota(jnp.int32, cnt_ref.shape, 1)
        lane1 = lax.broadcasted_iota(jnp.int32, (1, LANES), 1)
        tile_start = ((sub * LANES + lane8) * MOE_ROWS).astype(F32)
        ends = off_ref[1, 0:1, :]
        tile_group = jnp.zeros(cnt_ref.shape, F32)
        for g in range(N_GROUPS):
            end_g = jnp.sum(jnp.where(lane1 == g, ends, 0.0), axis=-1, keepdims=True)
            tile_group = tile_group + jnp.where(tile_start >= end_g, 1.0, 0.0)
        tile_ref[...] = tile_group.astype(jnp.int32)

    @pl.when(phase == 1)
    def _():
        earlier = _dot(tri_ref[...], onehot.astype(BF16))
        base = off_ref[0, 0:1, :] + cnt_ref[0:1, :]
        slot = jnp.sum(onehot * (base + earlier), axis=-1, keepdims=True)
        slot_ref[...] = jnp.broadcast_to(slot, route.shape).astype(jnp.int32)
        cnt_ref[0:1, :] += col_sum


def _moe_route(hc):
    n = hc.shape[0]
    tm = MOE_ROWS
    tri = (jnp.arange(tm)[:, None] > jnp.arange(tm)[None, :]).astype(BF16)
    upper = (jnp.arange(LANES)[:, None] < jnp.arange(LANES)[None, :]).astype(F32)
    return pl.pallas_call(
        _moe_route_kernel,
        grid=(2, n // tm),
        in_specs=[pl.BlockSpec((tm, LANES), lambda ph, i: (i, D_MODEL // LANES)),
                  pl.BlockSpec((tm, tm), lambda ph, i: (0, 0)),
                  pl.BlockSpec((LANES, LANES), lambda ph, i: (0, 0))],
        out_specs=[pl.BlockSpec((tm, LANES), lambda ph, i: (i * ph, 0)),
                   pl.BlockSpec((8, LANES), lambda ph, i: (0, 0))],
        out_shape=[jax.ShapeDtypeStruct((n, LANES), jnp.int32), jax.ShapeDtypeStruct((8, LANES), jnp.int32)],
        scratch_shapes=[pltpu.VMEM((8, LANES), F32), pltpu.VMEM((2, 8, LANES), F32)],
        compiler_params=_cparams(("arbitrary", "arbitrary")),
        name="moe_route",
    )(hc, tri, upper)


def _row_copies(n_rows, copy):
    lax.fori_loop(0, n_rows, lambda t, c: (copy(t).start(), c)[1], 0, unroll=8)
    lax.fori_loop(0, n_rows, lambda t, c: (copy(0).wait(), c)[1], 0, unroll=8)


def _moe_dispatch_kernel(slot_ref, hc_ref, init_ref, out_ref, sem):
    del init_ref
    copy = lambda t: pltpu.make_async_copy(hc_ref.at[pl.ds(t, 1)], out_ref.at[pl.ds(slot_ref[t], 1)], sem)
    _row_copies(hc_ref.shape[0], copy)


def _moe_dispatch(slot, hc, n_rows):
    n = hc.shape[0]
    tm = MOE_ROWS
    return pl.pallas_call(
        _moe_dispatch_kernel,
        grid=(n // tm,),
        in_specs=[pl.BlockSpec((tm,), lambda i: (i,), memory_space=pltpu.SMEM),
                  pl.BlockSpec((tm, HC_WIDTH), lambda i: (i, 0)),
                  pl.BlockSpec(memory_space=pl.ANY)],
        out_specs=pl.BlockSpec(memory_space=pl.ANY),
        out_shape=jax.ShapeDtypeStruct((n_rows, HC_WIDTH), F32),
        scratch_shapes=[pltpu.SemaphoreType.DMA(())],
        input_output_aliases={2: 0},
        compiler_params=_cparams(("arbitrary",)),
        name="moe_dispatch",
    )(slot, hc, jnp.zeros((n_rows, HC_WIDTH), F32))


def _moe_kernel(tile_ref, hc_ref, wup_ref, wdn_ref, o_ref):
    group = tile_ref[pl.program_id(0)]

    @pl.when(group < N_GROUPS)
    def _():
        h = hc_ref[:, :D_MODEL].astype(BF16)
        comb = hc_ref[:, D_MODEL:]
        lane = lax.broadcasted_iota(jnp.int32, comb.shape, 1)
        acc = jnp.zeros(o_ref.shape, F32)
        for e in range(EXPERTS_PER_GROUP):
            hid = _dot(h, wup_ref[e])
            gate, up = hid[:, :EXPERT_HIDDEN], hid[:, EXPERT_HIDDEN:]
            act = (gate * _sigmoid(gate) * up).astype(BF16)
            w = jnp.sum(jnp.where(lane == group * EXPERTS_PER_GROUP + e, comb, 0.0), axis=-1, keepdims=True)
            acc = acc + w * _dot(act, wdn_ref[e])
        o_ref[...] = acc

    @pl.when(group >= N_GROUPS)
    def _():
        o_ref[...] = jnp.zeros(o_ref.shape, F32)


def _moe_experts(tile_group, hc_sorted, wup, wdn):
    n_rows = hc_sorted.shape[0]
    tm = MOE_ROWS
    group = lambda i, tg: (jnp.minimum(tg[i], N_GROUPS - 1), 0, 0)
    grid_spec = pltpu.PrefetchScalarGridSpec(
        num_scalar_prefetch=1,
        grid=(n_rows // tm,),
        in_specs=[pl.BlockSpec((tm, HC_WIDTH), lambda i, tg: (i, 0)),
                  pl.BlockSpec((EXPERTS_PER_GROUP, D_MODEL, 2 * EXPERT_HIDDEN), group),
                  pl.BlockSpec((EXPERTS_PER_GROUP, EXPERT_HIDDEN, D_MODEL), group)],
        out_specs=pl.BlockSpec((tm, D_MODEL), lambda i, tg: (i, 0)),
    )
    return pl.pallas_call(
        _moe_kernel,
        grid_spec=grid_spec,
        out_shape=jax.ShapeDtypeStruct((n_rows, D_MODEL), F32),
        compiler_params=_cparams(("arbitrary",)),
        name="moe_experts",
    )(tile_group, hc_sorted, wup, wdn)


def _moe_combine_kernel(slot_ref, y_ref, x1_ref, g2_ref, lng_ref, lnb_ref, o_ref, buf_ref, sem, *, alpha):
    copy = lambda t: pltpu.make_async_copy(y_ref.at[pl.ds(slot_ref[t], 1)], buf_ref.at[pl.ds(t, 1)], sem)
    _row_copies(buf_ref.shape[0], copy)
    u = alpha * x1_ref[...] + g2_ref[0] * buf_ref[...]
    o_ref[...] = _layer_norm(u, lng_ref[...], lnb_ref[...])


def _moe_combine(slot, y_sorted, x1, ada3, lng, lnb, rows_per_batch, alpha):
    n = x1.shape[0]
    tm = MOE_ROWS
    full = lambda a: pl.BlockSpec(a.shape, lambda i: (0,) * a.ndim)
    return pl.pallas_call(
        functools.partial(_moe_combine_kernel, alpha=alpha),
        grid=(n // tm,),
        in_specs=[pl.BlockSpec((tm,), lambda i: (i,), memory_space=pltpu.SMEM),
                  pl.BlockSpec(memory_space=pl.ANY),
                  pl.BlockSpec((tm, D_MODEL), lambda i: (i, 0)),
                  _row_vec_spec(ada3, 5, tm, rows_per_batch),
                  full(lng), full(lnb)],
        out_specs=pl.BlockSpec((tm, D_MODEL), lambda i: (i, 0)),
        out_shape=jax.ShapeDtypeStruct((n, D_MODEL), F32),
        scratch_shapes=[pltpu.VMEM((tm, D_MODEL), F32), pltpu.SemaphoreType.DMA(())],
        compiler_params=_cparams(("arbitrary",)),
        name="moe_combine",
    )(slot, y_sorted, x1, ada3, lng, lnb)


def _moe(hc, x1, ada3, wup, wdn, lng, lnb, rows_per_batch, alpha):
    n = x1.shape[0]
    n_rows = n + N_GROUPS * MOE_ROWS
    slot_lanes, tile_table = _moe_route(hc)
    slot = slot_lanes[:, 0]
    tile_group = tile_table.reshape(-1)[:n_rows // MOE_ROWS]
    hc_sorted = _moe_dispatch(slot, hc, n_rows)
    y_sorted = _moe_experts(tile_group, hc_sorted, wup, wdn)
    return _moe_combine(slot, y_sorted, x1, ada3, lng, lnb, rows_per_batch, alpha)


def _prompt_dist():
    i = jnp.arange(TB, dtype=jnp.int32)[:, None]
    j = jnp.arange(TB, dtype=jnp.int32)[None, :]
    return jnp.concatenate([d * TB + i - j for d in range(-1, N_BIAS_TILES - 1)], axis=0)


def _sample_dist(nq, past_len):
    i = jnp.arange(nq, dtype=jnp.int32)[:, None]
    j = jnp.arange(PAGE_ROWS, dtype=jnp.int32)[None, :]
    last_page = past_len + i - (past_len - PAGE_ROWS + j)
    new = jnp.where(j < nq, i - j, -1)
    return jnp.concatenate([last_page, new], axis=1)


def kernel(x_prompt, x_sample, cache_kv_diff, cache_kv_dsa, cache_kidx, page_table, c_prompt, c_sample,
           rel_bias, w_ada, b_ada, w_in, lambda_q1, lambda_k1, lambda_q2, lambda_k2, subln_g, w_branch_a,
           w_branch_b, w_out, ln1_g, ln1_b, w_router_group, b_router_group, w_router_expert,
           b_router_expert, w_up, w_down, ln2_g, ln2_b):
    b, t, d = x_prompt.shape
    s, nq, _ = x_sample.shape
    depth = w_in.shape[0]
    n_pool = cache_kidx.shape[1]
    cache_a = cache_kv_diff.reshape(depth, n_pool, PAGE_ROWS * 2 * A_HEADS, 2 * HEAD_DIM)
    cache_bt = jnp.transpose(cache_kv_dsa, (0, 1, 3, 4, 5, 2))
    cache_it = jnp.transpose(cache_kidx, (0, 1, 3, 2))
    n_pages = page_table.shape[1]
    past_len = n_pages * PAGE_ROWS
    alpha = (2 * depth) ** 0.25
    topk_p = min(DSA_TOPK_MAX, t // 4)
    topk_s = min(DSA_TOPK_MAX, (past_len + nq) // 4)
    assert d == D_MODEL and t % TB == 0 and (s * nq) % ROW_TILE == 0 and n_pages % PAGES_PER_STEP == 0
    assert MOE_ROWS == ROW_TILE == TB and B_HEADS == 2 * N_CHAINS and A_HEADS % N_CHAINS == 0
    assert cache_kidx.shape[2] == PAGE_ROWS and nq <= 8

    bias_p = _bias_tiles(rel_bias, _prompt_dist()).reshape(A_HEADS + B_HEADS, N_BIAS_TILES, TB, TB)
    bias_s = _bias_tiles(rel_bias, _sample_dist(nq, past_len))
    bias_sa = jnp.broadcast_to(bias_s[:A_HEADS, None], (A_HEADS, 2, nq, 2 * PAGE_ROWS)).reshape(
        A_HEADS * 2 * nq, 2 * PAGE_ROWS)
    bias_sb = bias_s[A_HEADS:].reshape(B_HEADS * nq, 2 * PAGE_ROWS)
    tri_p = (jnp.arange(TB)[:, None] <= jnp.arange(TB)[None, :]).astype(BF16)
    tri_s = tri_p[:PAGE_ROWS, :PAGE_ROWS]
    lane_a = jnp.arange(A_WIDTH)[None, :] // HEAD_DIM
    mask_a = (lane_a == (jnp.arange(A_HEADS * 2 * nq)[:, None] // nq)).astype(F32)
    lane_b = jnp.arange(B_WIDTH)[None, :] // HEAD_DIM
    mask_b = (lane_b == (jnp.arange(B_HEADS * nq)[:, None] // nq)).astype(F32)

    xp = x_prompt.reshape(b * t, d)
    xs = x_sample.reshape(s * nq, d)
    c_all = jnp.concatenate([c_prompt, c_sample], axis=0)
    c_all = jnp.pad(c_all, ((0, -(b + s) % 8), (0, 0)))
    outs = [[] for _ in range(6)]
    for l in range(depth):
        lam_init = 0.8 - 0.6 * math.exp(-0.3 * l)
        lam4 = jnp.stack([lambda_q1[l], lambda_k1[l], lambda_q2[l], lambda_k2[l]]).astype(F32)
        g_sub = subln_g[l].reshape(1, 2 * HEAD_DIM)
        w = w_in[l]
        w2 = jnp.concatenate([w[:, :C_KI + IDX_DIM], w[:, C_KI:C_KI + IDX_DIM],
                              w[:, C_KI + IDX_DIM:C_KI + IDX_DIM + IDX_HEADS],
                              jnp.zeros((d, LANES - IDX_HEADS), w.dtype),
                              w[:, C_KI + IDX_DIM + IDX_HEADS:]], axis=1).astype(BF16)
        wba, wbb, wout = w_branch_a[l].astype(BF16), w_branch_b[l].astype(BF16), w_out[l].astype(BF16)
        wup, wdn = w_up[l].astype(BF16), w_down[l].astype(BF16)
        wr = jnp.concatenate([w_router_expert[l], w_router_group[l],
                              jnp.zeros((d, LANES - N_EXPERTS - N_GROUPS), F32)], axis=1)
        br = jnp.concatenate([b_router_expert[l], b_router_group[l],
                              jnp.zeros((LANES - N_EXPERTS - N_GROUPS,), F32)]).reshape(1, LANES)
        ln1 = (ln1_g[l].reshape(1, d), ln1_b[l].reshape(1, d))
        ln2 = (ln2_g[l].reshape(1, d), ln2_b[l].reshape(1, d))

        ada = _ada(c_all, w_ada[l], b_ada[l])
        ada_p = ada[:b].reshape(b, 1, 6 * d)
        ada_s = jnp.broadcast_to(ada[b:b + s, None], (s, nq, 6 * d)).reshape(s * nq // ROW_TILE, ROW_TILE, 6 * d)

        p = _in_proj(xp, ada_p, w2, t, True)
        oa = _diff_attn_prompt(lam4, g_sub, p["qa"], p["kva"], bias_p[:A_HEADS], b, t, lam_init)
        ob = _dsa_prompt(p, bias_p[A_HEADS:], tri_p, b, t, topk_p)
        x1, hc = _out_proj(oa, ob, p["sg"], xp, ada_p, wba, wbb, wout, *ln1, wr, br, t, alpha)
        xp = _moe(hc, x1, ada_p, wup, wdn, *ln2, t, alpha)
        outs[0].append(p["ra"].reshape(b, t, 2, A_HEADS, 2 * HEAD_DIM))
        outs[1].append(p["rb"].reshape(b, 2, B_HEADS, HEAD_DIM, t).transpose(0, 4, 1, 2, 3))
        outs[2].append(p["ki"].reshape(b, t, IDX_DIM))

        q = _in_proj(xs, ada_s, w2, nq, False)
        q_stack = q["qi"].reshape(s, nq, IDX_HEADS, IDX_DIM).transpose(0, 2, 1, 3).reshape(
            s, IDX_HEADS * nq, IDX_DIM)
        w_stack = q["wi"][:, :IDX_HEADS].reshape(s, nq, IDX_HEADS).transpose(0, 2, 1).reshape(
            s, IDX_HEADS * nq, 1)
        scores = _sample_idx(page_table, q_stack, w_stack, q["ki"].reshape(s, nq, IDX_DIM), cache_it, l)
        sel = _sample_select(scores, tri_s, topk_s)
        oa_s, ob_s = _sample_attn(
            page_table, lam4, g_sub,
            q["qa"].astype(F32).reshape(s, nq, A_WIDTH), q["qb"].astype(F32).reshape(s, nq, B_WIDTH),
            mask_a, mask_b, bias_sa, bias_sb,
            q["ra"].reshape(s, nq, 2 * A_WIDTH), q["rb"].reshape(s, nq, 2 * B_WIDTH), sel,
            cache_a, cache_bt, l, lam_init)
        x1, hc = _out_proj(oa_s.reshape(s * nq, A_WIDTH).astype(BF16),
                                 ob_s.reshape(s * nq, B_WIDTH).astype(BF16),
                                 q["sg"], xs, ada_s, wba, wbb, wout, *ln1, wr, br, nq, alpha)
        xs = _moe(hc, x1, ada_s, wup, wdn, *ln2, nq, alpha)
        outs[3].append(q["ra"].reshape(s, nq, 2, A_HEADS, 2 * HEAD_DIM))
        outs[4].append(q["rb"].reshape(s, nq, 2, B_HEADS, HEAD_DIM))
        outs[5].append(q["ki"].reshape(s, nq, IDX_DIM))

    return (xp.reshape(b, t, d), xs.reshape(s, nq, d)) + tuple(jnp.stack(o, 0) for o in outs)
```

```python
import functools
import math

import jax
import jax.numpy as jnp
from jax import lax
from jax.experimental import pallas as pl
from jax.experimental.pallas import tpu as pltpu

D_MODEL = 1024
HEAD_DIM = 64
A_HEADS = 8
B_HEADS = 8
IDX_HEADS = 8
IDX_DIM = 64
DSA_TOPK_MAX = 256
N_BUCKETS = 32
MAX_DISTANCE = 128
N_GROUPS = 4
EXPERTS_PER_GROUP = 4
N_EXPERTS = N_GROUPS * EXPERTS_PER_GROUP
EXPERT_HIDDEN = 512
LN_EPS = 1e-5

LANES = 128
TB = 256
ROW_TILE = 256
MOE_ROWS = 256
PAGE_ROWS = 128
PAGES_PER_STEP = 8
VMEM_LIMIT = 56 * 1024 * 1024

A_WIDTH = A_HEADS * 2 * HEAD_DIM
B_WIDTH = B_HEADS * HEAD_DIM
I_WIDTH = IDX_HEADS * IDX_DIM
C_QA, C_KVA, C_QB, C_KVB, C_QI, C_KI, C_WI, C_G = 0, 1024, 3072, 3584, 4608, 5120, 5248, 5376
W2_WIDTH = C_G + 2 * D_MODEL
HC_WIDTH = D_MODEL + LANES
GROUP_LANE = N_EXPERTS + N_GROUPS

F32 = jnp.float32
BF16 = jnp.bfloat16
NEG_INF = float("-inf")
INT_MIN = -2 ** 31


def _cparams(sem):
    return pltpu.CompilerParams(dimension_semantics=sem, vmem_limit_bytes=VMEM_LIMIT)


def _dot(a, b):
    return jnp.dot(a, b, preferred_element_type=F32)


def _dot_nt(a, b):
    return lax.dot_general(a, b, (((1,), (1,)), ((), ())), preferred_element_type=F32)


def _split(a):
    hi = a.astype(BF16)
    lo = (a - hi.astype(F32)).astype(BF16)
    return hi, lo


def _dot3(a, b):
    a_hi, a_lo = _split(a)
    b_hi, b_lo = _split(b)
    return _dot(a_hi, b_hi) + _dot(a_lo, b_hi) + _dot(a_hi, b_lo)


def _sigmoid(x):
    return 1.0 / (1.0 + jnp.exp(-x))


def _layer_norm(u, g, b):
    mu = jnp.mean(u, axis=-1, keepdims=True)
    d = u - mu
    var = jnp.mean(d * d, axis=-1, keepdims=True)
    return d * lax.rsqrt(var + LN_EPS) * g + b


def _bias_kernel(tab_ref, dist_ref, out_ref):
    h = pl.program_id(0)
    d = dist_ref[...]
    n = jnp.maximum(d, 0)
    max_exact = N_BUCKETS // 2
    nf = jnp.maximum(n, 1).astype(F32)
    large = max_exact + (jnp.log(nf / max_exact) / math.log(MAX_DISTANCE / max_exact)
                         * (N_BUCKETS - max_exact)).astype(jnp.int32)
    large = jnp.minimum(large, N_BUCKETS - 1)
    bucket = jnp.where(n < max_exact, n, large)
    last = tab_ref[N_BUCKETS - 1, h]
    acc = jnp.zeros(d.shape, F32)
    for m in range(N_BUCKETS - 1):
        acc = jnp.where(bucket == m, tab_ref[m, h] - last, acc)
    out_ref[0] = jnp.where(d < 0, NEG_INF, acc)


def _bias_tiles(rel_bias, dist):
    n_heads = rel_bias.shape[1]
    r, c = dist.shape
    return pl.pallas_call(
        _bias_kernel,
        grid=(n_heads,),
        in_specs=[pl.BlockSpec(memory_space=pltpu.SMEM),
                  pl.BlockSpec((r, c), lambda h: (0, 0))],
        out_specs=pl.BlockSpec((1, r, c), lambda h: (h, 0, 0)),
        out_shape=jax.ShapeDtypeStruct((n_heads, r, c), F32),
        compiler_params=_cparams(("arbitrary",)),
        name="bias_tiles",
    )(rel_bias, dist)


def _ada_kernel(c_ref, w_ref, b_ref, o_ref):
    c = c_ref[...]
    o_ref[...] = _dot3(c * _sigmoid(c), w_ref[...]) + b_ref[...]


def _ada(c_all, w_ada, b_ada):
    r, d = c_all.shape
    n = w_ada.shape[1]
    tn = 512
    return pl.pallas_call(
        _ada_kernel,
        grid=(n // tn,),
        in_specs=[pl.BlockSpec((r, d), lambda j: (0, 0)),
                  pl.BlockSpec((d, tn), lambda j: (0, j)),
                  pl.BlockSpec((1, tn), lambda j: (0, j))],
        out_specs=pl.BlockSpec((r, tn), lambda j: (0, j)),
        out_shape=jax.ShapeDtypeStruct((r, n), F32),
        compiler_params=_cparams(("arbitrary",)),
        name="ada",
    )(c_all, w_ada, b_ada.reshape(1, n))


def _inproj_kernel(x_ref, sh_ref, sc_ref, w_ref, qa_ref, ra_ref, kva_ref, qb_ref, rb_ref, kvb_ref,
                   qi_ref, ki_ref, kk_ref, wi_ref, sg_ref):
    h = (x_ref[...] * (1.0 + sc_ref[0]) + sh_ref[0]).astype(BF16)
    q_scale = HEAD_DIM ** -0.5

    def mm(c0, n):
        return _dot(h, w_ref[:, c0:c0 + n])

    for c in range(0, A_WIDTH, 512):
        qa_ref[:, c:c + 512] = (mm(C_QA + c, 512) * q_scale).astype(BF16)
    for c in range(0, 2 * A_WIDTH, 512):
        a = mm(C_KVA + c, 512)
        ra_ref[:, c:c + 512] = a
        kva_ref[:, c:c + 512] = a.astype(BF16)
    qb_ref[...] = (mm(C_QB, 512) * q_scale).astype(BF16)
    for c in range(0, 2 * B_WIDTH, 512):
        a = mm(C_KVB + c, 512)
        if len(rb_ref.shape) == 3:
            rb_ref[0, c:c + 512, :] = a.T
        else:
            rb_ref[:, c:c + 512] = a
        kvb_ref[:, c:c + 512] = a.astype(BF16)
    qi_ref[...] = (mm(C_QI, 512) * q_scale).astype(BF16)
    a = mm(C_KI, 2 * LANES)
    ki_ref[...] = a[:, :IDX_DIM]
    kk_ref[...] = a[:, :LANES].astype(BF16)
    wi_ref[...] = a[:, LANES:] * IDX_HEADS ** -0.5
    for c in range(0, 2 * D_MODEL, 512):
        sg_ref[:, c:c + 512] = _sigmoid(mm(C_G + c, 512))


def _row_vec_spec(arr, col, tm, rows_per_batch):
    if arr.shape[1] == 1:
        per = rows_per_batch // tm
        return pl.BlockSpec((1, 1, D_MODEL), lambda i, *_: (i // per, 0, col))
    return pl.BlockSpec((1, tm, D_MODEL), lambda i, *_: (i, 0, col))


def _in_proj(x, ada3, w2, rows_per_batch, rb_token_minor):
    n = x.shape[0]
    tm = ROW_TILE
    row = lambda w: pl.BlockSpec((tm, w), lambda i: (i, 0))
    outs = [("qa", A_WIDTH, BF16), ("ra", 2 * A_WIDTH, F32), ("kva", 2 * A_WIDTH, BF16),
            ("qb", B_WIDTH, BF16), ("rb", 2 * B_WIDTH, F32), ("kvb", 2 * B_WIDTH, BF16),
            ("qi", I_WIDTH, BF16), ("ki", IDX_DIM, F32), ("kk", LANES, BF16), ("wi", LANES, F32),
            ("sg", 2 * D_MODEL, F32)]
    out_specs = [row(w) for _, w, _ in outs]
    out_shape = [jax.ShapeDtypeStruct((n, w), dt) for _, w, dt in outs]
    if rb_token_minor:
        per = rows_per_batch // tm
        out_specs[4] = pl.BlockSpec((1, 2 * B_WIDTH, tm), lambda i: (i // per, 0, i % per))
        out_shape[4] = jax.ShapeDtypeStruct((n // rows_per_batch, 2 * B_WIDTH, rows_per_batch), F32)
    res = pl.pallas_call(
        _inproj_kernel,
        grid=(n // tm,),
        in_specs=[row(D_MODEL),
                  _row_vec_spec(ada3, 0, tm, rows_per_batch),
                  _row_vec_spec(ada3, 1, tm, rows_per_batch),
                  pl.BlockSpec((D_MODEL, W2_WIDTH), lambda i: (0, 0), pipeline_mode=pl.Buffered(1))],
        out_specs=out_specs,
        out_shape=out_shape,
        compiler_params=_cparams(("arbitrary",)),
        name="in_proj",
    )(x, ada3, ada3, w2)
    return {name: r for (name, _, _), r in zip(outs, res)}


def _stack2(x):
    return jnp.concatenate([x, x], axis=0)


LOG2_E = 1.4426950408889634
N_CHAINS = 4
N_BIAS_TILES = 4


def _bias_index(tile_distance):
    return jnp.clip(tile_distance, -1, N_BIAS_TILES - 2) + 1


def _lane_fold(x, op):
    r = x[:, :LANES]
    for c in range(1, x.shape[1] // LANES):
        r = op(r, x[:, c * LANES:(c + 1) * LANES])
    return r


def _two_pass_attend(q_ref, k_ref, v_ref, nt, near_bias, every_bias, s_ref, st_ref):
    rows = 2 * TB
    st_ref[0] = jnp.full((N_CHAINS, rows, LANES), NEG_INF, F32)
    st_ref[1] = jnp.zeros((N_CHAINS, rows, LANES), F32)
    st_ref[2] = jnp.zeros((N_CHAINS, rows, LANES), F32)

    def tiles(jj):
        return [(2 * jj + u, jnp.minimum(2 * jj + u, nt - 1)) for u in range(2)]

    def rows_of(ref, c, pair):
        return jnp.concatenate([ref[pl.ds(pl.multiple_of(jc * TB, TB), TB), c * LANES:(c + 1) * LANES]
                                for _, jc in pair], axis=0)

    def first(near):
        def body(jj, _):
            pair = tiles(jj)
            every = None
            if every_bias is not None:
                every = _stack2(jnp.concatenate([every_bias(jc) for _, jc in pair], axis=1))
            for c in range(N_CHAINS):
                s = _dot_nt(_masked_pair(q_ref[:, c * LANES:(c + 1) * LANES]), rows_of(k_ref, c, pair))
                if near:
                    s = s + jnp.concatenate([near_bias(c, j) for j, _ in pair], axis=1)
                if every is not None:
                    s = s + every
                s = s * LOG2_E
                s_ref[jj, c] = s
                st_ref[0, c] = jnp.maximum(st_ref[0, c], _lane_fold(s, jnp.maximum))
            return 0
        return body

    n_steps = (nt + 1) // 2
    n_far = jnp.maximum(nt - 2, 0) // 2
    lax.fori_loop(0, n_far, first(False), 0)
    lax.fori_loop(n_far, n_steps, first(True), 0)
    for c in range(N_CHAINS):
        st_ref[0, c] = jnp.broadcast_to(jnp.max(st_ref[0, c], axis=-1, keepdims=True), (rows, LANES))

    def second(jj, _):
        pair = tiles(jj)
        for c in range(N_CHAINS):
            s = s_ref[jj, c]
            mb = st_ref[0, c]
            p = [jnp.exp2(s[:, i * LANES:(i + 1) * LANES] - mb) for i in range(2 * TB // LANES)]
            st_ref[1, c] += (p[0] + p[1]) + (p[2] + p[3])
            st_ref[2, c] += _dot(jnp.concatenate(p, axis=1).astype(BF16), rows_of(v_ref, c, pair))
        return 0

    lax.fori_loop(0, n_steps, second, 0)
    return lambda c: (st_ref[2, c], jnp.sum(st_ref[1, c], axis=-1, keepdims=True))


def _lambda_value(lam_ref, lam_init):
    a = jnp.sum(lam_ref[0:1, :] * lam_ref[1:2, :], axis=-1, keepdims=True)
    b = jnp.sum(lam_ref[2:3, :] * lam_ref[3:4, :], axis=-1, keepdims=True)
    return jnp.exp(a) - jnp.exp(b) + lam_init


def _sub_layer_norm(o, g, lam_init):
    o = o * lax.rsqrt(jnp.mean(o * o, axis=-1, keepdims=True) + LN_EPS)
    return o * g * (1.0 - lam_init)


def _masked_pair(q):
    lo = lax.broadcasted_iota(jnp.int32, q.shape, 1) < HEAD_DIM
    zero = jnp.zeros_like(q)
    return jnp.concatenate([jnp.where(lo, q, zero), jnp.where(lo, zero, q)], axis=0)


def _attend_scratch(nq):
    return [pltpu.VMEM(((nq + 1) // 2, N_CHAINS, 2 * TB, 2 * TB), F32),
            pltpu.VMEM((3, N_CHAINS, 2 * TB, LANES), F32)]


def _diff_attn_kernel(lam_ref, g_ref, q_ref, k_ref, v_ref, bias_ref, o_ref, s_ref, st_ref, *, lam_init):
    qi = pl.program_id(2)
    near = lambda c, j: _stack2(bias_ref[c, _bias_index(qi - j)])
    result = _two_pass_attend(q_ref, k_ref, v_ref, qi + 1, near, None, s_ref, st_ref)
    lam = _lambda_value(lam_ref, lam_init)
    for c in range(N_CHAINS):
        acc, l = result(c)
        o = acc / l
        o = o[:TB] - lam * o[TB:]
        o_ref[:, c * LANES:(c + 1) * LANES] = _sub_layer_norm(o, g_ref[...], lam_init).astype(BF16)


def _diff_attn_prompt(lam4, g_sub, qa, kva, bias_p, b, t, lam_init):
    nq = t // TB
    groups = A_HEADS // N_CHAINS
    width = N_CHAINS * LANES
    return pl.pallas_call(
        functools.partial(_diff_attn_kernel, lam_init=lam_init),
        grid=(b, groups, nq),
        in_specs=[pl.BlockSpec((4, HEAD_DIM), lambda bi, h, i: (0, 0)),
                  pl.BlockSpec((1, 2 * HEAD_DIM), lambda bi, h, i: (0, 0)),
                  pl.BlockSpec((TB, width), lambda bi, h, i: (bi * nq + i, h)),
                  pl.BlockSpec((t, width), lambda bi, h, i: (bi, h)),
                  pl.BlockSpec((t, width), lambda bi, h, i: (bi, groups + h)),
                  pl.BlockSpec((N_CHAINS, N_BIAS_TILES, TB, TB), lambda bi, h, i: (h, 0, 0, 0))],
        out_specs=pl.BlockSpec((TB, width), lambda bi, h, i: (bi * nq + i, h)),
        out_shape=jax.ShapeDtypeStruct((b * t, A_WIDTH), BF16),
        scratch_shapes=_attend_scratch(nq),
        compiler_params=_cparams(("arbitrary", "arbitrary", "arbitrary")),
        name="diff_attn_prompt",
    )(lam4, g_sub, qa, kva, kva, bias_p)


KEY_NEG_INF = INT_MIN + 0x7FFFFF


def _key_to_float(key):
    bits = jnp.where(key < 0, key ^ jnp.int32(0x7FFFFFFF), key)
    return jnp.where(key < KEY_NEG_INF, NEG_INF, lax.bitcast_convert_type(bits, F32))


SELECT_ROW_GROUPS = 4


def _topk_select(sc_ref, nt, topk, tri_ref):
    rows, tw = sc_ref.shape[1], sc_ref.shape[2]
    rg = rows // SELECT_ROW_GROUPS
    groups = [slice(g * rg, (g + 1) * rg) for g in range(SELECT_ROW_GROUPS)]
    kf = float(topk)

    def count_ge(rs, cf):
        acc = None
        for j in range(nt):
            r = _lane_fold(jnp.where(sc_ref[j, rs, :] >= cf, 1.0, 0.0), jnp.add)
            acc = r if acc is None else acc + r
        return jnp.sum(acc, axis=-1, keepdims=True)

    def search(p, ts):
        inc = lax.shift_left(jnp.int32(1), jnp.int32(31) - p)
        out = []
        for rs, t in zip(groups, ts):
            cand = t + inc
            out.append(jnp.where(count_ge(rs, _key_to_float(cand)) >= kf, cand, t))
        return tuple(out)

    t0 = jnp.full((rg, 1), INT_MIN, jnp.int32)
    ts = lax.fori_loop(0, 32, search, (t0,) * SELECT_ROW_GROUPS)
    tri = tri_ref[...]
    for rs, t in zip(groups, ts):
        t_lo = _key_to_float(t)
        t_hi = _key_to_float(t + 1)
        need = kf - count_ge(rs, t_hi)
        c = jnp.zeros((rg, 1), F32)
        for j in range(nt):
            s = sc_ref[j, rs, :]
            gt = s >= t_hi
            eq = jnp.logical_and(s >= t_lo, jnp.logical_not(gt))
            e = jnp.where(eq, 1.0, 0.0)
            rank = _dot(e.astype(BF16), tri) + c
            sel = jnp.logical_or(gt, jnp.logical_and(eq, rank <= need))
            sel = jnp.logical_and(sel, s > NEG_INF)
            sc_ref[j, rs, :] = jnp.where(sel, 0.0, NEG_INF)
            c = c + jnp.sum(e, axis=-1, keepdims=True)


def _dsa_kernel(qi_ref, kk_ref, wi_ref, qb_ref, kb_ref, vb_ref, bias_ref, tri_ref, o_ref, sc_ref, s_ref,
                st_ref, *, topk):
    qblk = pl.program_id(1)
    nt = qblk + 1
    tq = qi_ref.shape[0]
    lo = lax.broadcasted_iota(jnp.int32, (tq, LANES), 1) < HEAD_DIM
    zero = jnp.zeros((tq, LANES), BF16)

    def halves(qp):
        return jnp.where(lo, qp, zero), jnp.where(lo, zero, qp)

    wi = wi_ref[...]

    def index_tile(j, _):
        off = pl.multiple_of(j * TB, TB)
        kk = kk_ref[pl.ds(off, TB), :]
        acc = jnp.zeros((tq, TB), F32)
        for m in range(IDX_HEADS // 2):
            q_lo, q_hi = halves(qi_ref[:, m * LANES:(m + 1) * LANES])
            acc = acc + wi[:, 2 * m:2 * m + 1] * jnp.maximum(_dot_nt(q_lo, kk), 0.0)
            acc = acc + wi[:, 2 * m + 1:2 * m + 2] * jnp.maximum(_dot_nt(q_hi, kk), 0.0)
        row = lax.broadcasted_iota(jnp.int32, (tq, TB), 0) + qblk * TB
        col = lax.broadcasted_iota(jnp.int32, (tq, TB), 1) + j * TB
        sc_ref[j] = jnp.where(col <= row, acc, NEG_INF)
        return 0

    lax.fori_loop(0, nt, index_tile, 0)
    for n_tiles in range(1, sc_ref.shape[0] + 1):
        pl.when(nt == n_tiles)(functools.partial(_topk_select, sc_ref, n_tiles, topk, tri_ref))

    def near(m, j):
        d = _bias_index(qblk - j)
        return jnp.concatenate([bias_ref[2 * m, d], bias_ref[2 * m + 1, d]], axis=0)

    result = _two_pass_attend(qb_ref, kb_ref, vb_ref, nt, near, lambda jc: sc_ref[jc], s_ref, st_ref)
    lo_out = lax.broadcasted_iota(jnp.int32, (tq, LANES), 1) < HEAD_DIM
    for m in range(N_CHAINS):
        acc, l = result(m)
        o = acc / l
        o_ref[:, m * LANES:(m + 1) * LANES] = jnp.where(lo_out, o[:tq], o[tq:]).astype(BF16)


def _dsa_prompt(p, bias_b, tri, b, t, topk):
    nq = t // TB
    return pl.pallas_call(
        functools.partial(_dsa_kernel, topk=topk),
        grid=(b, nq),
        in_specs=[pl.BlockSpec((TB, I_WIDTH), lambda bi, i: (bi * nq + i, 0)),
                  pl.BlockSpec((t, LANES), lambda bi, i: (bi, 0)),
                  pl.BlockSpec((TB, LANES), lambda bi, i: (bi * nq + i, 0)),
                  pl.BlockSpec((TB, B_WIDTH), lambda bi, i: (bi * nq + i, 0)),
                  pl.BlockSpec((t, B_WIDTH), lambda bi, i: (bi, 0)),
                  pl.BlockSpec((t, B_WIDTH), lambda bi, i: (bi, 1)),
                  pl.BlockSpec((B_HEADS, N_BIAS_TILES, TB, TB), lambda bi, i: (0, 0, 0, 0),
                               pipeline_mode=pl.Buffered(1)),
                  pl.BlockSpec((TB, TB), lambda bi, i: (0, 0))],
        out_specs=pl.BlockSpec((TB, B_WIDTH), lambda bi, i: (bi * nq + i, 0)),
        out_shape=jax.ShapeDtypeStruct((b * t, B_WIDTH), BF16),
        scratch_shapes=[pltpu.VMEM((nq, TB, TB), F32)] + _attend_scratch(nq),
        compiler_params=_cparams(("arbitrary", "arbitrary")),
        name="dsa_prompt",
    )(p["qi"], p["kk"], p["wi"], p["qb"], p["kvb"], p["kvb"], bias_b, tri)


def _sample_idx_kernel(pt_ref, q_ref, w_ref, kn_ref, *rest, n_pages):
    page_refs, o_ref = rest[:n_pages], rest[n_pages]
    q = q_ref[0]
    w = w_ref[0]
    nq = q.shape[0] // IDX_HEADS

    def combine(qk):
        rel = jnp.maximum(qk, 0.0) * w
        sc = rel[0:nq]
        for h in range(1, IDX_HEADS):
            sc = sc + rel[h * nq:(h + 1) * nq]
        return sc

    for k in range(n_pages):
        o_ref[k] = combine(_dot(q, page_refs[k][0, 0].astype(BF16)))
    new = jnp.concatenate([kn_ref[0], jnp.zeros((PAGE_ROWS - nq, IDX_DIM), F32)], axis=0).astype(BF16)
    row = lax.broadcasted_iota(jnp.int32, (nq, PAGE_ROWS), 0)
    col = lax.broadcasted_iota(jnp.int32, (nq, PAGE_ROWS), 1)
    o_ref[n_pages] = jnp.where(col <= row, combine(_dot_nt(q, new)), NEG_INF)


def _sample_idx(page_table, q_stack, w_stack, ki_new, cache_kidx_t, layer):
    s, n_pages = page_table.shape
    nq = ki_new.shape[1]
    hq = q_stack.shape[1]
    page_spec = lambda k: pl.BlockSpec((1, 1, IDX_DIM, PAGE_ROWS), lambda i, pt, k=k: (layer, pt[i, k], 0, 0))
    grid_spec = pltpu.PrefetchScalarGridSpec(
        num_scalar_prefetch=1,
        grid=(s,),
        in_specs=[pl.BlockSpec((1, hq, IDX_DIM), lambda i, pt: (i, 0, 0)),
                  pl.BlockSpec((1, hq, 1), lambda i, pt: (i, 0, 0)),
                  pl.BlockSpec((1, nq, IDX_DIM), lambda i, pt: (i, 0, 0))]
                 + [page_spec(k) for k in range(n_pages)],
        out_specs=pl.BlockSpec((n_pages + 1, nq, PAGE_ROWS), lambda i, pt: (0, i, 0)),
    )
    return pl.pallas_call(
        functools.partial(_sample_idx_kernel, n_pages=n_pages),
        grid_spec=grid_spec,
        out_shape=jax.ShapeDtypeStruct((n_pages + 1, s * nq, PAGE_ROWS), F32),
        compiler_params=_cparams(("arbitrary",)),
        name="sample_idx",
    )(page_table, q_stack, w_stack, ki_new, *([cache_kidx_t] * n_pages))


def _select_kernel(sc_ref, tri_ref, o_ref, *, topk):
    o_ref[...] = sc_ref[...]
    _topk_select(o_ref, o_ref.shape[0], topk, tri_ref)


def _sample_select(scores, tri, topk):
    nt, rows, tw = scores.shape
    tr = min(rows, 256)
    return pl.pallas_call(
        functools.partial(_select_kernel, topk=topk),
        grid=(rows // tr,),
        in_specs=[pl.BlockSpec((nt, tr, tw), lambda i: (0, i, 0)),
                  pl.BlockSpec((tw, tw), lambda i: (0, 0))],
        out_specs=pl.BlockSpec((nt, tr, tw), lambda i: (0, i, 0)),
        out_shape=jax.ShapeDtypeStruct(scores.shape, F32),
        compiler_params=_cparams(("arbitrary",)),
        name="sample_select",
    )(scores, tri)


def _sample_attn_kernel(pt_ref, lam_ref, g_ref, qa_ref, qb_ref, mska_ref, mskb_ref, biasa_ref, biasb_ref,
                        mext_ref, bext_ref, kvan_ref, kvbn_ref, selp_ref, seln_ref, *rest, n_chunks, lam_init):
    g_pages = PAGES_PER_STEP
    kva_refs, kvb_refs = rest[:g_pages], rest[g_pages:2 * g_pages]
    oa_ref, ob_ref = rest[2 * g_pages], rest[2 * g_pages + 1]
    qa_s, qn_s, qb_s, ma_s, la_s, acca_s, mb_s, lb_s, accb_s = rest[2 * g_pages + 2:]
    c = pl.program_id(1)
    nq = qa_ref.shape[1]
    rows_a = A_HEADS * 2 * nq
    rows_b = B_HEADS * nq

    @pl.when(c == 0)
    def _():
        qa_s[...] = (jnp.concatenate([qa_ref[0]] * (2 * A_HEADS), axis=0) * mska_ref[...]).astype(BF16)
        qn_s[...] = jnp.concatenate([_masked_pair(qa_ref[0, :, h * LANES:(h + 1) * LANES])
                                     for h in range(A_HEADS)], axis=0).astype(BF16)
        qb_s[...] = (jnp.concatenate([qb_ref[0]] * B_HEADS, axis=0) * mskb_ref[...]).astype(BF16)
        ma_s[...] = jnp.full(ma_s.shape, NEG_INF, F32)
        mb_s[...] = jnp.full(mb_s.shape, NEG_INF, F32)
        la_s[...] = jnp.zeros(la_s.shape, F32)
        lb_s[...] = jnp.zeros(lb_s.shape, F32)
        acca_s[...] = jnp.zeros(acca_s.shape, F32)
        accb_s[...] = jnp.zeros(accb_s.shape, F32)

    def diag_a(r):
        return jnp.concatenate(
            [r[h * 2 * nq:(h + 1) * 2 * nq, h * LANES:(h + 1) * LANES] for h in range(A_HEADS)], axis=0)

    def diag_b(r):
        return jnp.concatenate(
            [r[m * 2 * nq:(m + 1) * 2 * nq, m * LANES:(m + 1) * LANES] for m in range(B_HEADS // 2)], axis=0)

    def update_a(s, values):
        mn = jnp.maximum(ma_s[...], jnp.max(s, axis=-1, keepdims=True))
        p = jnp.exp(s - mn)
        al = jnp.exp(ma_s[...] - mn)
        la_s[...] = al * la_s[...] + jnp.sum(p, axis=-1, keepdims=True)
        acca_s[...] = al * acca_s[...] + values(p.astype(BF16))
        ma_s[...] = mn

    def update_b(s, values):
        mn = jnp.maximum(mb_s[...], jnp.max(s, axis=-1, keepdims=True))
        ms = jnp.where(mn == NEG_INF, 0.0, mn)
        p = jnp.exp(s - ms)
        al = jnp.exp(mb_s[...] - ms)
        lb_s[...] = al * lb_s[...] + jnp.sum(p, axis=-1, keepdims=True)
        accb_s[...] = al * accb_s[...] + diag_b(values(p.astype(BF16)))
        mb_s[...] = mn

    def tile_rows(x, n):
        return jnp.concatenate([x] * n, axis=0)

    last = c == n_chunks - 1
    far = (g_pages - 1) * PAGE_ROWS
    bias_b = jnp.concatenate([jnp.zeros((rows_b, far), F32), jnp.where(last, biasb_ref[:, :PAGE_ROWS], 0.0)], axis=1)

    ext = A_HEADS * PAGE_ROWS
    page_keys = lambda ref, kv: ref[0, 0, :, kv].reshape(ext, 2 * HEAD_DIM).astype(BF16)
    qn = qn_s[...]
    s_pages = [_dot_nt(qn, page_keys(ref, 0)) for ref in kva_refs]
    s_pages = ([sp + mext_ref[...] for sp in s_pages[:-1]]
               + [s_pages[-1] + jnp.where(last, bext_ref[...], mext_ref[...])])

    def page_values(p):
        out = None
        for g, ref in enumerate(kva_refs):
            part = _dot(p[:, g * ext:(g + 1) * ext], page_keys(ref, 1))
            out = part if out is None else out + part
        return out

    update_a(jnp.concatenate(s_pages, axis=1), page_values)

    kt = jnp.concatenate([ref[0, 0, 0].reshape(B_WIDTH, PAGE_ROWS) for ref in kvb_refs], axis=1).astype(BF16)
    vt = jnp.concatenate([ref[0, 0, 1].reshape(B_WIDTH, PAGE_ROWS) for ref in kvb_refs], axis=1).astype(BF16)
    sel = jnp.concatenate([tile_rows(selp_ref[g], B_HEADS) for g in range(g_pages)], axis=1)
    update_b(_dot(qb_s[...], kt) + bias_b + sel, lambda p: _dot_nt(p, vt))

    @pl.when(last)
    def _():
        pad = lambda x: jnp.concatenate([x, jnp.zeros((PAGE_ROWS - nq, x.shape[1]), F32)], axis=0).astype(BF16)
        kn, vn = pad(kvan_ref[0, :, :A_WIDTH]), pad(kvan_ref[0, :, A_WIDTH:])
        update_a(_dot_nt(qa_s[...], kn) + biasa_ref[...], lambda p: diag_a(_dot(p, vn)))
        kn, vn = pad(kvbn_ref[0, :, :B_WIDTH]), pad(kvbn_ref[0, :, B_WIDTH:])
        update_b(_dot_nt(qb_s[...], kn) + biasb_ref[:, PAGE_ROWS:] + tile_rows(seln_ref[0], B_HEADS),
                 lambda p: _dot(p, vn))

        lam = _lambda_value(lam_ref, lam_init)
        oa = acca_s[...] / la_s[...]
        for h in range(A_HEADS):
            o = oa[h * 2 * nq:h * 2 * nq + nq] - lam * oa[h * 2 * nq + nq:(h + 1) * 2 * nq]
            oa_ref[0, :, h * LANES:(h + 1) * LANES] = _sub_layer_norm(o, g_ref[...], lam_init)
        ob = accb_s[...] / lb_s[...]
        lo = lax.broadcasted_iota(jnp.int32, (nq, LANES), 1) < HEAD_DIM
        for m in range(B_HEADS // 2):
            ob_ref[0, :, m * LANES:(m + 1) * LANES] = jnp.where(
                lo, ob[2 * m * nq:(2 * m + 1) * nq], ob[(2 * m + 1) * nq:(2 * m + 2) * nq])


def _sample_attn(page_table, lam4, g_sub, qa, qb, mask_a, mask_b, bias_a, bias_b, kva_new, kvb_new, sel,
                 cache_a, cache_bt, layer, lam_init):
    s, n_pages = page_table.shape
    nq = qa.shape[1]
    g_pages = PAGES_PER_STEP
    n_chunks = n_pages // g_pages
    rows_a, rows_b = A_HEADS * 2 * nq, B_HEADS * nq
    ext = A_HEADS * PAGE_ROWS
    key_head = jnp.arange(ext)[None, :] % A_HEADS
    row_head = jnp.arange(rows_a)[:, None] // (2 * nq)
    mask_ext = jnp.where(key_head == row_head, 0.0, NEG_INF).astype(F32)
    bias_ext = jnp.repeat(bias_a[:, :PAGE_ROWS], A_HEADS, axis=1) + mask_ext
    const = lambda shape: pl.BlockSpec(shape, lambda i, c, pt: (0,) * len(shape))
    seq = lambda shape: pl.BlockSpec(shape, lambda i, c, pt: (i,) + (0,) * (len(shape) - 1))
    page_a = lambda g: pl.BlockSpec((1, 1) + cache_a.shape[2:],
                                    lambda i, c, pt, g=g: (layer, pt[i, c * g_pages + g], 0, 0, 0, 0))
    page_b = lambda g: pl.BlockSpec((1, 1) + cache_bt.shape[2:],
                                    lambda i, c, pt, g=g: (layer, pt[i, c * g_pages + g], 0, 0, 0, 0))
    grid_spec = pltpu.PrefetchScalarGridSpec(
        num_scalar_prefetch=1,
        grid=(s, n_chunks),
        in_specs=[const((4, HEAD_DIM)), const((1, 2 * HEAD_DIM)),
                  seq((1, nq, A_WIDTH)), seq((1, nq, B_WIDTH)),
                  const((rows_a, A_WIDTH)), const((rows_b, B_WIDTH)),
                  const((rows_a, PAGE_ROWS)), const((rows_b, 2 * PAGE_ROWS)),
                  const((rows_a, ext)), const((rows_a, ext)),
                  seq((1, nq, 2 * A_WIDTH)), seq((1, nq, 2 * B_WIDTH)),
                  pl.BlockSpec((g_pages, nq, PAGE_ROWS), lambda i, c, pt: (c, i, 0)),
                  pl.BlockSpec((1, nq, PAGE_ROWS), lambda i, c, pt: (n_pages, i, 0))]
                 + [page_a(g) for g in range(g_pages)]
                 + [page_b(g) for g in range(g_pages)],
        out_specs=[seq((1, nq, A_WIDTH)), seq((1, nq, B_WIDTH))],
        scratch_shapes=[pltpu.VMEM((rows_a, A_WIDTH), BF16), pltpu.VMEM((rows_a, 2 * HEAD_DIM), BF16),
                        pltpu.VMEM((rows_b, B_WIDTH), BF16),
                        pltpu.VMEM((rows_a, 1), F32), pltpu.VMEM((rows_a, 1), F32),
                        pltpu.VMEM((rows_a, LANES), F32),
                        pltpu.VMEM((rows_b, 1), F32), pltpu.VMEM((rows_b, 1), F32),
                        pltpu.VMEM((rows_b, LANES), F32)],
    )
    return pl.pallas_call(
        functools.partial(_sample_attn_kernel, n_chunks=n_chunks, lam_init=lam_init),
        grid_spec=grid_spec,
        out_shape=[jax.ShapeDtypeStruct((s, nq, A_WIDTH), F32), jax.ShapeDtypeStruct((s, nq, B_WIDTH), F32)],
        compiler_params=_cparams(("arbitrary", "arbitrary")),
        name="sample_attn",
    )(page_table, lam4, g_sub, qa, qb, mask_a, mask_b, bias_a[:, PAGE_ROWS:], bias_b, mask_ext, bias_ext,
      kva_new, kvb_new, sel, sel, *([cache_a] * g_pages), *([cache_bt] * g_pages))


def _outproj_kernel(oa_ref, ob_ref, sg_ref, x_ref, g1_ref, sh2_ref, sc2_ref, wba_ref, wbb_ref, wout_ref,
                    lng_ref, lnb_ref, wr_ref, br_ref, x1_ref, hc_ref, *, alpha):
    ya = _dot(oa_ref[...], wba_ref[...])
    yb = _dot(ob_ref[...], wbb_ref[...])
    t = sg_ref[:, :D_MODEL] * ya + sg_ref[:, D_MODEL:] * yb
    mix = _dot(t.astype(BF16), wout_ref[...])
    x1 = _layer_norm(alpha * x_ref[...] + g1_ref[0] * mix, lng_ref[...], lnb_ref[...])
    x1_ref[...] = x1
    h2 = x1 * (1.0 + sc2_ref[0]) + sh2_ref[0]
    hc_ref[:, :D_MODEL] = h2

    logits = _dot3(h2, wr_ref[...]) + br_ref[...]
    lane = lax.broadcasted_iota(jnp.int32, logits.shape, 1).astype(F32)
    big = float(LANES)
    is_group = jnp.logical_and(lane >= N_EXPERTS, lane < N_EXPERTS + N_GROUPS)
    lg = jnp.where(is_group, logits, NEG_INF)
    mg = jnp.max(lg, axis=-1, keepdims=True)
    g_sel = jnp.min(jnp.where(lg == mg, lane, big), axis=-1, keepdims=True) - N_EXPERTS
    p_g = 1.0 / jnp.sum(jnp.exp(lg - mg), axis=-1, keepdims=True)
    first = g_sel * EXPERTS_PER_GROUP
    in_group = jnp.logical_and(lane >= first, lane < first + EXPERTS_PER_GROUP)
    le = jnp.where(in_group, logits, NEG_INF)
    ex = jnp.exp(le - jnp.max(le, axis=-1, keepdims=True))
    pe = jnp.where(in_group, ex / jnp.sum(ex, axis=-1, keepdims=True), -1.0)
    v1 = jnp.max(pe, axis=-1, keepdims=True)
    i1 = jnp.min(jnp.where(pe == v1, lane, big), axis=-1, keepdims=True)
    pe2 = jnp.where(lane == i1, -1.0, pe)
    v2 = jnp.max(pe2, axis=-1, keepdims=True)
    i2 = jnp.min(jnp.where(pe2 == v2, lane, big), axis=-1, keepdims=True)
    tot = v1 + v2
    hc_ref[:, D_MODEL:] = (jnp.where(lane == i1, p_g * (v1 / tot), 0.0)
                           + jnp.where(lane == i2, p_g * (v2 / tot), 0.0)
                           + jnp.where(lane == GROUP_LANE, g_sel, 0.0))


def _out_proj(oa, ob, sg, x, ada3, wba, wbb, wout, lng, lnb, wr, br, rows_per_batch, alpha):
    n = x.shape[0]
    tm = ROW_TILE
    row = lambda w: pl.BlockSpec((tm, w), lambda i: (i, 0))
    full = lambda a: pl.BlockSpec(a.shape, lambda i: (0,) * a.ndim)
    return pl.pallas_call(
        functools.partial(_outproj_kernel, alpha=alpha),
        grid=(n // tm,),
        in_specs=[row(A_WIDTH), row(B_WIDTH), row(2 * D_MODEL), row(D_MODEL),
                  _row_vec_spec(ada3, 2, tm, rows_per_batch),
                  _row_vec_spec(ada3, 3, tm, rows_per_batch),
                  _row_vec_spec(ada3, 4, tm, rows_per_batch),
                  full(wba), full(wbb), full(wout), full(lng), full(lnb), full(wr), full(br)],
        out_specs=[row(D_MODEL), row(HC_WIDTH)],
        out_shape=[jax.ShapeDtypeStruct((n, D_MODEL), F32), jax.ShapeDtypeStruct((n, HC_WIDTH), F32)],
        compiler_params=_cparams(("arbitrary",)),
        name="out_proj",
    )(oa, ob, sg, x, ada3, ada3, ada3, wba, wbb, wout, lng, lnb, wr, br)


def _moe_route_kernel(hc_ref, tri_ref, upper_ref, slot_ref, tile_ref, cnt_ref, off_ref):
    phase, i = pl.program_id(0), pl.program_id(1)
    route = hc_ref[...]
    lane = lax.broadcasted_iota(jnp.int32, route.shape, 1)
    group = jnp.sum(jnp.where(lane == GROUP_LANE, route, 0.0), axis=-1, keepdims=True)
    onehot = jnp.where(lane.astype(F32) == group, 1.0, 0.0)
    col_sum = jnp.sum(onehot, axis=0, keepdims=True)

    @pl.when(jnp.logical_and(phase == 0, i == 0))
    def _():
        cnt_ref[...] = jnp.zeros(cnt_ref.shape, F32)

    @pl.when(phase == 0)
    def _():
        cnt_ref[0:1, :] += col_sum
        slot_ref[...] = jnp.zeros(slot_ref.shape, jnp.int32)

    @pl.when(jnp.logical_and(phase == 1, i == 0))
    def _():
        padded = jnp.ceil(cnt_ref[...] / MOE_ROWS) * MOE_ROWS
        start = _dot3(padded, upper_ref[...])
        off_ref[0] = start
        off_ref[1] = start + padded
        cnt_ref[...] = jnp.zeros(cnt_ref.shape, F32)
        sub = lax.broadcasted_iota(jnp.int32, cnt_ref.shape, 0)
        lane8 = lax.broadcasted_iota(jnp.int32, cnt_ref.shape, 1)
        lane1 = lax.broadcasted_iota(jnp.int32, (1, LANES), 1)
        tile_start = ((sub * LANES + lane8) * MOE_ROWS).astype(F32)
        ends = off_ref[1, 0:1, :]
        tile_group = jnp.zeros(cnt_ref.shape, F32)
        for g in range(N_GROUPS):
            end_g = jnp.sum(jnp.where(lane1 == g, ends, 0.0), axis=-1, keepdims=True)
            tile_group = tile_group + jnp.where(tile_start >= end_g, 1.0, 0.0)
        tile_ref[...] = tile_group.astype(jnp.int32)

    @pl.when(phase == 1)
    def _():
        earlier = _dot(tri_ref[...], onehot.astype(BF16))
        base = off_ref[0, 0:1, :] + cnt_ref[0:1, :]
        slot = jnp.sum(onehot * (base + earlier), axis=-1, keepdims=True)
        slot_ref[...] = jnp.broadcast_to(slot, route.shape).astype(jnp.int32)
        cnt_ref[0:1, :] += col_sum


def _moe_route(hc):
    n = hc.shape[0]
    tm = MOE_ROWS
    tri = (jnp.arange(tm)[:, None] > jnp.arange(tm)[None, :]).astype(BF16)
    upper = (jnp.arange(LANES)[:, None] < jnp.arange(LANES)[None, :]).astype(F32)
    return pl.pallas_call(
        _moe_route_kernel,
        grid=(2, n // tm),
        in_specs=[pl.BlockSpec((tm, LANES), lambda ph, i: (i, D_MODEL // LANES)),
                  pl.BlockSpec((tm, tm), lambda ph, i: (0, 0)),
                  pl.BlockSpec((LANES, LANES), lambda ph, i: (0, 0))],
        out_specs=[pl.BlockSpec((tm, LANES), lambda ph, i: (i * ph, 0)),
                   pl.BlockSpec((8, LANES), lambda ph, i: (0, 0))],
        out_shape=[jax.ShapeDtypeStruct((n, LANES), jnp.int32), jax.ShapeDtypeStruct((8, LANES), jnp.int32)],
        scratch_shapes=[pltpu.VMEM((8, LANES), F32), pltpu.VMEM((2, 8, LANES), F32)],
        compiler_params=_cparams(("arbitrary", "arbitrary")),
        name="moe_route",
    )(hc, tri, upper)


def _row_copies(n_rows, copy):
    def start_pair(i, c):
        copy(2 * i).start(priority=0)
        copy(2 * i + 1).start(priority=1)
        return c
    lax.fori_loop(0, n_rows // 2, start_pair, 0, unroll=4)
    lax.fori_loop(0, n_rows, lambda t, c: (copy(0).wait(), c)[1], 0, unroll=8)


def _moe_dispatch_kernel(slot_ref, hc_ref, init_ref, out_ref, sem):
    del init_ref
    copy = lambda t: pltpu.make_async_copy(hc_ref.at[pl.ds(t, 1)], out_ref.at[pl.ds(slot_ref[t], 1)], sem)
    _row_copies(hc_ref.shape[0], copy)


def _moe_dispatch(slot, hc, n_rows):
    n = hc.shape[0]
    tm = MOE_ROWS
    return pl.pallas_call(
        _moe_dispatch_kernel,
        grid=(n // tm,),
        in_specs=[pl.BlockSpec((tm,), lambda i: (i,), memory_space=pltpu.SMEM),
                  pl.BlockSpec((tm, HC_WIDTH), lambda i: (i, 0)),
                  pl.BlockSpec(memory_space=pl.ANY)],
        out_specs=pl.BlockSpec(memory_space=pl.ANY),
        out_shape=jax.ShapeDtypeStruct((n_rows, HC_WIDTH), F32),
        scratch_shapes=[pltpu.SemaphoreType.DMA(())],
        input_output_aliases={2: 0},
        compiler_params=_cparams(("arbitrary",)),
        name="moe_dispatch",
    )(slot, hc, jnp.zeros((n_rows, HC_WIDTH), F32))


def _moe_kernel(tile_ref, hc_ref, wup_ref, wdn_ref, o_ref):
    group = tile_ref[pl.program_id(0)]

    @pl.when(group < N_GROUPS)
    def _():
        h = hc_ref[:, :D_MODEL].astype(BF16)
        comb = hc_ref[:, D_MODEL:]
        lane = lax.broadcasted_iota(jnp.int32, comb.shape, 1)
        acc = jnp.zeros(o_ref.shape, F32)
        for e in range(EXPERTS_PER_GROUP):
            hid = _dot(h, wup_ref[e])
            gate, up = hid[:, :EXPERT_HIDDEN], hid[:, EXPERT_HIDDEN:]
            act = (gate * _sigmoid(gate) * up).astype(BF16)
            w = jnp.sum(jnp.where(lane == group * EXPERTS_PER_GROUP + e, comb, 0.0), axis=-1, keepdims=True)
            acc = acc + w * _dot(act, wdn_ref[e])
        o_ref[...] = acc

    @pl.when(group >= N_GROUPS)
    def _():
        o_ref[...] = jnp.zeros(o_ref.shape, F32)


def _moe_experts(tile_group, hc_sorted, wup, wdn):
    n_rows = hc_sorted.shape[0]
    tm = MOE_ROWS
    group = lambda i, tg: (jnp.minimum(tg[i], N_GROUPS - 1), 0, 0)
    grid_spec = pltpu.PrefetchScalarGridSpec(
        num_scalar_prefetch=1,
        grid=(n_rows // tm,),
        in_specs=[pl.BlockSpec((tm, HC_WIDTH), lambda i, tg: (i, 0)),
                  pl.BlockSpec((EXPERTS_PER_GROUP, D_MODEL, 2 * EXPERT_HIDDEN), group),
                  pl.BlockSpec((EXPERTS_PER_GROUP, EXPERT_HIDDEN, D_MODEL), group)],
        out_specs=pl.BlockSpec((tm, D_MODEL), lambda i, tg: (i, 0)),
    )
    return pl.pallas_call(
        _moe_kernel,
        grid_spec=grid_spec,
        out_shape=jax.ShapeDtypeStruct((n_rows, D_MODEL), F32),
        compiler_params=_cparams(("arbitrary",)),
        name="moe_experts",
    )(tile_group, hc_sorted, wup, wdn)


def _moe_combine_kernel(slot_ref, y_ref, x1_ref, g2_ref, lng_ref, lnb_ref, o_ref, buf_ref, sem, *, alpha):
    copy = lambda t: pltpu.make_async_copy(y_ref.at[pl.ds(slot_ref[t], 1)], buf_ref.at[pl.ds(t, 1)], sem)
    _row_copies(buf_ref.shape[0], copy)
    u = alpha * x1_ref[...] + g2_ref[0] * buf_ref[...]
    o_ref[...] = _layer_norm(u, lng_ref[...], lnb_ref[...])


def _moe_combine(slot, y_sorted, x1, ada3, lng, lnb, rows_per_batch, alpha):
    n = x1.shape[0]
    tm = MOE_ROWS
    full = lambda a: pl.BlockSpec(a.shape, lambda i: (0,) * a.ndim)
    return pl.pallas_call(
        functools.partial(_moe_combine_kernel, alpha=alpha),
        grid=(n // tm,),
        in_specs=[pl.BlockSpec((tm,), lambda i: (i,), memory_space=pltpu.SMEM),
                  pl.BlockSpec(memory_space=pl.ANY),
                  pl.BlockSpec((tm, D_MODEL), lambda i: (i, 0)),
                  _row_vec_spec(ada3, 5, tm, rows_per_batch),
                  full(lng), full(lnb)],
        out_specs=pl.BlockSpec((tm, D_MODEL), lambda i: (i, 0)),
        out_shape=jax.ShapeDtypeStruct((n, D_MODEL), F32),
        scratch_shapes=[pltpu.VMEM((tm, D_MODEL), F32), pltpu.SemaphoreType.DMA(())],
        compiler_params=_cparams(("arbitrary",)),
        name="moe_combine",
    )(slot, y_sorted, x1, ada3, lng, lnb)


def _moe(hc, x1, ada3, wup, wdn, lng, lnb, rows_per_batch, alpha):
    n = x1.shape[0]
    n_rows = n + N_GROUPS * MOE_ROWS
    slot_lanes, tile_table = _moe_route(hc)
    slot = slot_lanes[:, 0]
    tile_group = tile_table.reshape(-1)[:n_rows // MOE_ROWS]
    hc_sorted = _moe_dispatch(slot, hc, n_rows)
    y_sorted = _moe_experts(tile_group, hc_sorted, wup, wdn)
    return _moe_combine(slot, y_sorted, x1, ada3, lng, lnb, rows_per_batch, alpha)


def _prompt_dist():
    i = jnp.arange(TB, dtype=jnp.int32)[:, None]
    j = jnp.arange(TB, dtype=jnp.int32)[None, :]
    return jnp.concatenate([d * TB + i - j for d in range(-1, N_BIAS_TILES - 1)], axis=0)


def _sample_dist(nq, past_len):
    i = jnp.arange(nq, dtype=jnp.int32)[:, None]
    j = jnp.arange(PAGE_ROWS, dtype=jnp.int32)[None, :]
    last_page = past_len + i - (past_len - PAGE_ROWS + j)
    new = jnp.where(j < nq, i - j, -1)
    return jnp.concatenate([last_page, new], axis=1)


def kernel(x_prompt, x_sample, cache_kv_diff, cache_kv_dsa, cache_kidx, page_table, c_prompt, c_sample,
           rel_bias, w_ada, b_ada, w_in, lambda_q1, lambda_k1, lambda_q2, lambda_k2, subln_g, w_branch_a,
           w_branch_b, w_out, ln1_g, ln1_b, w_router_group, b_router_group, w_router_expert,
           b_router_expert, w_up, w_down, ln2_g, ln2_b):
    b, t, d = x_prompt.shape
    s, nq, _ = x_sample.shape
    depth = w_in.shape[0]
    n_pool = cache_kidx.shape[1]
    cache_a = cache_kv_diff
    cache_bt = jnp.transpose(cache_kv_dsa, (0, 1, 3, 4, 5, 2))
    cache_it = jnp.transpose(cache_kidx, (0, 1, 3, 2))
    n_pages = page_table.shape[1]
    past_len = n_pages * PAGE_ROWS
    alpha = (2 * depth) ** 0.25
    topk_p = min(DSA_TOPK_MAX, t // 4)
    topk_s = min(DSA_TOPK_MAX, (past_len + nq) // 4)
    assert d == D_MODEL and t % TB == 0 and (s * nq) % ROW_TILE == 0 and n_pages % PAGES_PER_STEP == 0
    assert MOE_ROWS == ROW_TILE == TB and B_HEADS == 2 * N_CHAINS and A_HEADS % N_CHAINS == 0
    assert cache_kidx.shape[2] == PAGE_ROWS and nq <= 8

    bias_p = _bias_tiles(rel_bias, _prompt_dist()).reshape(A_HEADS + B_HEADS, N_BIAS_TILES, TB, TB)
    bias_s = _bias_tiles(rel_bias, _sample_dist(nq, past_len))
    bias_sa = jnp.broadcast_to(bias_s[:A_HEADS, None], (A_HEADS, 2, nq, 2 * PAGE_ROWS)).reshape(
        A_HEADS * 2 * nq, 2 * PAGE_ROWS)
    bias_sb = bias_s[A_HEADS:].reshape(B_HEADS * nq, 2 * PAGE_ROWS)
    tri_p = (jnp.arange(TB)[:, None] <= jnp.arange(TB)[None, :]).astype(BF16)
    tri_s = tri_p[:PAGE_ROWS, :PAGE_ROWS]
    lane_a = jnp.arange(A_WIDTH)[None, :] // HEAD_DIM
    mask_a = (lane_a == (jnp.arange(A_HEADS * 2 * nq)[:, None] // nq)).astype(F32)
    lane_b = jnp.arange(B_WIDTH)[None, :] // HEAD_DIM
    mask_b = (lane_b == (jnp.arange(B_HEADS * nq)[:, None] // nq)).astype(F32)

    xp = x_prompt.reshape(b * t, d)
    xs = x_sample.reshape(s * nq, d)
    c_all = jnp.concatenate([c_prompt, c_sample], axis=0)
    c_all = jnp.pad(c_all, ((0, -(b + s) % 8), (0, 0)))
    outs = [[] for _ in range(6)]
    for l in range(depth):
        lam_init = 0.8 - 0.6 * math.exp(-0.3 * l)
        lam4 = jnp.stack([lambda_q1[l], lambda_k1[l], lambda_q2[l], lambda_k2[l]]).astype(F32)
        g_sub = subln_g[l].reshape(1, 2 * HEAD_DIM)
        w = w_in[l]
        w2 = jnp.concatenate([w[:, :C_KI + IDX_DIM], w[:, C_KI:C_KI + IDX_DIM],
                              w[:, C_KI + IDX_DIM:C_KI + IDX_DIM + IDX_HEADS],
                              jnp.zeros((d, LANES - IDX_HEADS), w.dtype),
                              w[:, C_KI + IDX_DIM + IDX_HEADS:]], axis=1).astype(BF16)
        wba, wbb, wout = w_branch_a[l].astype(BF16), w_branch_b[l].astype(BF16), w_out[l].astype(BF16)
        wup, wdn = w_up[l].astype(BF16), w_down[l].astype(BF16)
        wr = jnp.concatenate([w_router_expert[l], w_router_group[l],
                              jnp.zeros((d, LANES - N_EXPERTS - N_GROUPS), F32)], axis=1)
        br = jnp.concatenate([b_router_expert[l], b_router_group[l],
                              jnp.zeros((LANES - N_EXPERTS - N_GROUPS,), F32)]).reshape(1, LANES)
        ln1 = (ln1_g[l].reshape(1, d), ln1_b[l].reshape(1, d))
        ln2 = (ln2_g[l].reshape(1, d), ln2_b[l].reshape(1, d))

        ada = _ada(c_all, w_ada[l], b_ada[l])
        ada_p = ada[:b].reshape(b, 1, 6 * d)
        ada_s = jnp.broadcast_to(ada[b:b + s, None], (s, nq, 6 * d)).reshape(s * nq // ROW_TILE, ROW_TILE, 6 * d)

        p = _in_proj(xp, ada_p, w2, t, True)
        oa = _diff_attn_prompt(lam4, g_sub, p["qa"], p["kva"], bias_p[:A_HEADS], b, t, lam_init)
        ob = _dsa_prompt(p, bias_p[A_HEADS:], tri_p, b, t, topk_p)
        x1, hc = _out_proj(oa, ob, p["sg"], xp, ada_p, wba, wbb, wout, *ln1, wr, br, t, alpha)
        xp = _moe(hc, x1, ada_p, wup, wdn, *ln2, t, alpha)
        outs[0].append(p["ra"].reshape(b, t, 2, A_HEADS, 2 * HEAD_DIM))
        outs[1].append(p["rb"].reshape(b, 2, B_HEADS, HEAD_DIM, t).transpose(0, 4, 1, 2, 3))
        outs[2].append(p["ki"].reshape(b, t, IDX_DIM))

        q = _in_proj(xs, ada_s, w2, nq, False)
        q_stack = q["qi"].reshape(s, nq, IDX_HEADS, IDX_DIM).transpose(0, 2, 1, 3).reshape(
            s, IDX_HEADS * nq, IDX_DIM)
        w_stack = q["wi"][:, :IDX_HEADS].reshape(s, nq, IDX_HEADS).transpose(0, 2, 1).reshape(
            s, IDX_HEADS * nq, 1)
        scores = _sample_idx(page_table, q_stack, w_stack, q["ki"].reshape(s, nq, IDX_DIM), cache_it, l)
        sel = _sample_select(scores, tri_s, topk_s)
        oa_s, ob_s = _sample_attn(
            page_table, lam4, g_sub,
            q["qa"].astype(F32).reshape(s, nq, A_WIDTH), q["qb"].astype(F32).reshape(s, nq, B_WIDTH),
            mask_a, mask_b, bias_sa, bias_sb,
            q["ra"].reshape(s, nq, 2 * A_WIDTH), q["rb"].reshape(s, nq, 2 * B_WIDTH), sel,
            cache_a, cache_bt, l, lam_init)
        x1, hc = _out_proj(oa_s.reshape(s * nq, A_WIDTH).astype(BF16),
                                 ob_s.reshape(s * nq, B_WIDTH).astype(BF16),
                                 q["sg"], xs, ada_s, wba, wbb, wout, *ln1, wr, br, nq, alpha)
        xs = _moe(hc, x1, ada_s, wup, wdn, *ln2, nq, alpha)
        outs[3].append(q["ra"].reshape(s, nq, 2, A_HEADS, 2 * HEAD_DIM))
        outs[4].append(q["rb"].reshape(s, nq, 2, B_HEADS, HEAD_DIM))
        outs[5].append(q["ki"].reshape(s, nq, IDX_DIM))

    return (xp.reshape(b, t, d), xs.reshape(s, nq, d)) + tuple(jnp.stack(o, 0) for o in outs)
```

```python
import functools
import math

import jax
import jax.numpy as jnp
from jax import lax
from jax.experimental import pallas as pl
from jax.experimental.pallas import tpu as pltpu

D_MODEL = 1024
HEAD_DIM = 64
A_HEADS = 8
B_HEADS = 8
IDX_HEADS = 8
IDX_DIM = 64
DSA_TOPK_MAX = 256
N_BUCKETS = 32
MAX_DISTANCE = 128
N_GROUPS = 4
EXPERTS_PER_GROUP = 4
N_EXPERTS = N_GROUPS * EXPERTS_PER_GROUP
EXPERT_HIDDEN = 512
LN_EPS = 1e-5

LANES = 128
TB = 256
ROW_TILE = 256
MOE_ROWS = 256
ROUTE_ROWS = 512
PAGE_ROWS = 128
PAGES_PER_STEP = 8
VMEM_LIMIT = 56 * 1024 * 1024

A_WIDTH = A_HEADS * 2 * HEAD_DIM
B_WIDTH = B_HEADS * HEAD_DIM
I_WIDTH = IDX_HEADS * IDX_DIM
C_QA, C_KVA, C_QB, C_KVB, C_QI, C_KI, C_WI, C_G = 0, 1024, 3072, 3584, 4608, 5120, 5248, 5376
W2_WIDTH = C_G + 2 * D_MODEL
HC_WIDTH = D_MODEL + LANES
GROUP_LANE = N_EXPERTS + N_GROUPS

F32 = jnp.float32
BF16 = jnp.bfloat16
NEG_INF = float("-inf")
INT_MIN = -2 ** 31


def _cparams(sem):
    return pltpu.CompilerParams(dimension_semantics=sem, vmem_limit_bytes=VMEM_LIMIT)


def _dot(a, b):
    return jnp.dot(a, b, preferred_element_type=F32)


def _dot_nt(a, b):
    return lax.dot_general(a, b, (((1,), (1,)), ((), ())), preferred_element_type=F32)


def _split(a):
    hi = a.astype(BF16)
    lo = (a - hi.astype(F32)).astype(BF16)
    return hi, lo


def _dot3(a, b):
    a_hi, a_lo = _split(a)
    b_hi, b_lo = _split(b)
    return _dot(a_hi, b_hi) + _dot(a_lo, b_hi) + _dot(a_hi, b_lo)


def _sigmoid(x):
    return 1.0 / (1.0 + jnp.exp(-x))


def _layer_norm(u, g, b):
    mu = jnp.mean(u, axis=-1, keepdims=True)
    d = u - mu
    var = jnp.mean(d * d, axis=-1, keepdims=True)
    return d * lax.rsqrt(var + LN_EPS) * g + b


def _bias_kernel(tab_ref, dist_ref, out_ref):
    h = pl.program_id(0)
    d = dist_ref[...]
    n = jnp.maximum(d, 0)
    max_exact = N_BUCKETS // 2
    nf = jnp.maximum(n, 1).astype(F32)
    large = max_exact + (jnp.log(nf / max_exact) / math.log(MAX_DISTANCE / max_exact)
                         * (N_BUCKETS - max_exact)).astype(jnp.int32)
    large = jnp.minimum(large, N_BUCKETS - 1)
    bucket = jnp.where(n < max_exact, n, large)
    last = tab_ref[N_BUCKETS - 1, h]
    acc = jnp.zeros(d.shape, F32)
    for m in range(N_BUCKETS - 1):
        acc = jnp.where(bucket == m, tab_ref[m, h] - last, acc)
    out_ref[0] = jnp.where(d < 0, NEG_INF, acc)


def _bias_tiles(rel_bias, dist):
    n_heads = rel_bias.shape[1]
    r, c = dist.shape
    return pl.pallas_call(
        _bias_kernel,
        grid=(n_heads,),
        in_specs=[pl.BlockSpec(memory_space=pltpu.SMEM),
                  pl.BlockSpec((r, c), lambda h: (0, 0))],
        out_specs=pl.BlockSpec((1, r, c), lambda h: (h, 0, 0)),
        out_shape=jax.ShapeDtypeStruct((n_heads, r, c), F32),
        compiler_params=_cparams(("arbitrary",)),
        name="bias_tiles",
    )(rel_bias, dist)


def _ada_kernel(c_ref, w_ref, b_ref, o_ref):
    c = c_ref[...]
    o_ref[...] = _dot3(c * _sigmoid(c), w_ref[...]) + b_ref[...]


def _ada(c_all, w_ada, b_ada):
    r, d = c_all.shape
    n = w_ada.shape[1]
    tn = 512
    return pl.pallas_call(
        _ada_kernel,
        grid=(n // tn,),
        in_specs=[pl.BlockSpec((r, d), lambda j: (0, 0)),
                  pl.BlockSpec((d, tn), lambda j: (0, j)),
                  pl.BlockSpec((1, tn), lambda j: (0, j))],
        out_specs=pl.BlockSpec((r, tn), lambda j: (0, j)),
        out_shape=jax.ShapeDtypeStruct((r, n), F32),
        compiler_params=_cparams(("arbitrary",)),
        name="ada",
    )(c_all, w_ada, b_ada.reshape(1, n))


def _inproj_kernel(x_ref, sh_ref, sc_ref, w_ref, qa_ref, ra_ref, kva_ref, qb_ref, rb_ref, kvb_ref,
                   qi_ref, ki_ref, kk_ref, wi_ref, sg_ref):
    h = (x_ref[...] * (1.0 + sc_ref[0]) + sh_ref[0]).astype(BF16)
    q_scale = HEAD_DIM ** -0.5

    def mm(c0, n):
        return _dot(h, w_ref[:, c0:c0 + n])

    for c in range(0, A_WIDTH, 512):
        qa_ref[:, c:c + 512] = (mm(C_QA + c, 512) * q_scale).astype(BF16)
    for c in range(0, 2 * A_WIDTH, 512):
        a = mm(C_KVA + c, 512)
        ra_ref[:, c:c + 512] = a
        kva_ref[:, c:c + 512] = a.astype(BF16)
    qb_ref[...] = (mm(C_QB, 512) * q_scale).astype(BF16)
    for c in range(0, 2 * B_WIDTH, 512):
        a = mm(C_KVB + c, 512)
        if len(rb_ref.shape) == 3:
            rb_ref[0, c:c + 512, :] = a.T
        else:
            rb_ref[:, c:c + 512] = a
        kvb_ref[:, c:c + 512] = a.astype(BF16)
    qi_ref[...] = (mm(C_QI, 512) * q_scale).astype(BF16)
    a = mm(C_KI, 2 * LANES)
    ki_ref[...] = a[:, :IDX_DIM]
    kk_ref[...] = a[:, :LANES].astype(BF16)
    wi_ref[...] = a[:, LANES:] * IDX_HEADS ** -0.5
    for c in range(0, 2 * D_MODEL, 512):
        sg_ref[:, c:c + 512] = _sigmoid(mm(C_G + c, 512))


def _row_vec_spec(arr, col, tm, rows_per_batch):
    if arr.shape[1] == 1:
        per = rows_per_batch // tm
        return pl.BlockSpec((1, 1, D_MODEL), lambda i, *_: (i // per, 0, col))
    return pl.BlockSpec((1, tm, D_MODEL), lambda i, *_: (i, 0, col))


def _in_proj(x, ada3, w2, rows_per_batch, rb_token_minor):
    n = x.shape[0]
    tm = ROW_TILE
    row = lambda w: pl.BlockSpec((tm, w), lambda i: (i, 0))
    outs = [("qa", A_WIDTH, BF16), ("ra", 2 * A_WIDTH, F32), ("kva", 2 * A_WIDTH, BF16),
            ("qb", B_WIDTH, BF16), ("rb", 2 * B_WIDTH, F32), ("kvb", 2 * B_WIDTH, BF16),
            ("qi", I_WIDTH, BF16), ("ki", IDX_DIM, F32), ("kk", LANES, BF16), ("wi", LANES, F32),
            ("sg", 2 * D_MODEL, F32)]
    out_specs = [row(w) for _, w, _ in outs]
    out_shape = [jax.ShapeDtypeStruct((n, w), dt) for _, w, dt in outs]
    if rb_token_minor:
        per = rows_per_batch // tm
        out_specs[4] = pl.BlockSpec((1, 2 * B_WIDTH, tm), lambda i: (i // per, 0, i % per))
        out_shape[4] = jax.ShapeDtypeStruct((n // rows_per_batch, 2 * B_WIDTH, rows_per_batch), F32)
    res = pl.pallas_call(
        _inproj_kernel,
        grid=(n // tm,),
        in_specs=[row(D_MODEL),
                  _row_vec_spec(ada3, 0, tm, rows_per_batch),
                  _row_vec_spec(ada3, 1, tm, rows_per_batch),
                  pl.BlockSpec((D_MODEL, W2_WIDTH), lambda i: (0, 0), pipeline_mode=pl.Buffered(1))],
        out_specs=out_specs,
        out_shape=out_shape,
        compiler_params=_cparams(("arbitrary",)),
        name="in_proj",
    )(x, ada3, ada3, w2)
    return {name: r for (name, _, _), r in zip(outs, res)}


def _stack2(x):
    return jnp.concatenate([x, x], axis=0)


LOG2_E = 1.4426950408889634
N_CHAINS = 4
N_BIAS_TILES = 3


def _bias_index(tile_distance):
    return jnp.minimum(tile_distance, N_BIAS_TILES - 1)


def _lane_fold(x, op):
    r = x[:, :LANES]
    for c in range(1, x.shape[1] // LANES):
        r = op(r, x[:, c * LANES:(c + 1) * LANES])
    return r


def _two_pass_attend(q_ref, k_ref, v_ref, nt, near_bias, every_bias, s_ref, st_ref):
    rows = 2 * TB
    st_ref[0] = jnp.full((N_CHAINS, rows, LANES), NEG_INF, F32)
    st_ref[1] = jnp.zeros((N_CHAINS, rows, LANES), F32)
    st_ref[2] = jnp.zeros((N_CHAINS, rows, LANES), F32)

    def rows_of(ref, c, tiles):
        return jnp.concatenate([ref[pl.ds(pl.multiple_of(j * TB, TB), TB), c * LANES:(c + 1) * LANES]
                                for j in tiles], axis=0)

    def scores(jj, tiles, near):
        width = len(tiles) * TB
        every = None
        if every_bias is not None:
            every = _stack2(jnp.concatenate([every_bias(j) for j in tiles], axis=1))
        for c in range(N_CHAINS):
            s = _dot_nt(_masked_pair(q_ref[:, c * LANES:(c + 1) * LANES]), rows_of(k_ref, c, tiles))
            if near:
                s = s + jnp.concatenate([near_bias(c, j) for j in tiles], axis=1)
            if every is not None:
                s = s + every
            s = s * LOG2_E
            s_ref[jj, c, :, :width] = s
            st_ref[0, c] = jnp.maximum(st_ref[0, c], _lane_fold(s, jnp.maximum))

    def values(jj, tiles):
        width = len(tiles) * TB
        for c in range(N_CHAINS):
            s = s_ref[jj, c, :, :width]
            mb = st_ref[0, c]
            p = [jnp.exp2(s[:, i * LANES:(i + 1) * LANES] - mb) for i in range(width // LANES)]
            st_ref[1, c] += _lane_fold(jnp.concatenate(p, axis=1), jnp.add)
            st_ref[2, c] += _dot(jnp.concatenate(p, axis=1).astype(BF16), rows_of(v_ref, c, tiles))

    pair = lambda jj: [2 * jj, 2 * jj + 1]
    n_pairs = nt // 2
    n_far = jnp.maximum(n_pairs - 1, 0)
    odd = nt % 2 == 1
    loop = lambda lo, hi, fn: lax.fori_loop(lo, hi, lambda jj, c: (fn(jj), c)[1], 0)
    loop(0, n_far, lambda jj: scores(jj, pair(jj), False))
    loop(n_far, n_pairs, lambda jj: scores(jj, pair(jj), True))
    pl.when(odd)(lambda: scores(n_pairs, [nt - 1], True))
    for c in range(N_CHAINS):
        st_ref[0, c] = jnp.broadcast_to(jnp.max(st_ref[0, c], axis=-1, keepdims=True), (rows, LANES))
    loop(0, n_pairs, lambda jj: values(jj, pair(jj)))
    pl.when(odd)(lambda: values(n_pairs, [nt - 1]))
    return lambda c: (st_ref[2, c], jnp.sum(st_ref[1, c], axis=-1, keepdims=True))


def _lambda_value(lam_ref, lam_init):
    a = jnp.sum(lam_ref[0:1, :] * lam_ref[1:2, :], axis=-1, keepdims=True)
    b = jnp.sum(lam_ref[2:3, :] * lam_ref[3:4, :], axis=-1, keepdims=True)
    return jnp.exp(a) - jnp.exp(b) + lam_init


def _sub_layer_norm(o, g, lam_init):
    o = o * lax.rsqrt(jnp.mean(o * o, axis=-1, keepdims=True) + LN_EPS)
    return o * g * (1.0 - lam_init)


def _masked_pair(q):
    lo = lax.broadcasted_iota(jnp.int32, q.shape, 1) < HEAD_DIM
    zero = jnp.zeros_like(q)
    return jnp.concatenate([jnp.where(lo, q, zero), jnp.where(lo, zero, q)], axis=0)


def _attend_scratch(nq):
    return [pltpu.VMEM(((nq + 1) // 2, N_CHAINS, 2 * TB, 2 * TB), F32),
            pltpu.VMEM((3, N_CHAINS, 2 * TB, LANES), F32)]


def _diff_attn_kernel(lam_ref, g_ref, q_ref, k_ref, v_ref, bias_ref, o_ref, s_ref, st_ref, *, lam_init):
    qi = pl.program_id(2)
    near = lambda c, j: _stack2(bias_ref[c, _bias_index(qi - j)])
    result = _two_pass_attend(q_ref, k_ref, v_ref, qi + 1, near, None, s_ref, st_ref)
    lam = _lambda_value(lam_ref, lam_init)
    for c in range(N_CHAINS):
        acc, l = result(c)
        o = acc / l
        o = o[:TB] - lam * o[TB:]
        o_ref[:, c * LANES:(c + 1) * LANES] = _sub_layer_norm(o, g_ref[...], lam_init).astype(BF16)


def _diff_attn_prompt(lam4, g_sub, qa, kva, bias_p, b, t, lam_init):
    nq = t // TB
    groups = A_HEADS // N_CHAINS
    width = N_CHAINS * LANES
    return pl.pallas_call(
        functools.partial(_diff_attn_kernel, lam_init=lam_init),
        grid=(b, groups, nq),
        in_specs=[pl.BlockSpec((4, HEAD_DIM), lambda bi, h, i: (0, 0)),
                  pl.BlockSpec((1, 2 * HEAD_DIM), lambda bi, h, i: (0, 0)),
                  pl.BlockSpec((TB, width), lambda bi, h, i: (bi * nq + i, h)),
                  pl.BlockSpec((t, width), lambda bi, h, i: (bi, h)),
                  pl.BlockSpec((t, width), lambda bi, h, i: (bi, groups + h)),
                  pl.BlockSpec((N_CHAINS, N_BIAS_TILES, TB, TB), lambda bi, h, i: (h, 0, 0, 0))],
        out_specs=pl.BlockSpec((TB, width), lambda bi, h, i: (bi * nq + i, h)),
        out_shape=jax.ShapeDtypeStruct((b * t, A_WIDTH), BF16),
        scratch_shapes=_attend_scratch(nq),
        compiler_params=_cparams(("arbitrary", "arbitrary", "arbitrary")),
        name="diff_attn_prompt",
    )(lam4, g_sub, qa, kva, kva, bias_p)


KEY_NEG_INF = INT_MIN + 0x7FFFFF


def _key_to_float(key):
    bits = jnp.where(key < 0, key ^ jnp.int32(0x7FFFFFFF), key)
    return jnp.where(key < KEY_NEG_INF, NEG_INF, lax.bitcast_convert_type(bits, F32))


SELECT_ROW_GROUPS = 4


def _topk_select(sc_ref, nt, topk, tri_ref):
    rows, tw = sc_ref.shape[1], sc_ref.shape[2]
    rg = rows // SELECT_ROW_GROUPS
    groups = [slice(g * rg, (g + 1) * rg) for g in range(SELECT_ROW_GROUPS)]
    kf = float(topk)

    def count_ge(rs, cf):
        acc = None
        for j in range(nt):
            r = _lane_fold(jnp.where(sc_ref[j, rs, :] >= cf, 1.0, 0.0), jnp.add)
            acc = r if acc is None else acc + r
        return jnp.sum(acc, axis=-1, keepdims=True)

    def search(p, ts):
        inc = lax.shift_left(jnp.int32(1), jnp.int32(31) - p)
        out = []
        for rs, t in zip(groups, ts):
            cand = t + inc
            out.append(jnp.where(count_ge(rs, _key_to_float(cand)) >= kf, cand, t))
        return tuple(out)

    t0 = jnp.full((rg, 1), INT_MIN, jnp.int32)
    ts = lax.fori_loop(0, 32, search, (t0,) * SELECT_ROW_GROUPS)
    t_los = [_key_to_float(t) for t in ts]
    n_tied = 0.0
    for rs, t, t_lo in zip(groups, ts, t_los):
        exact = jnp.logical_or(count_ge(rs, t_lo) == kf, t == KEY_NEG_INF)
        n_tied = n_tied + jnp.sum(jnp.where(exact, 0.0, 1.0))

    @pl.when(n_tied == 0.0)
    def _():
        for rs, t_lo in zip(groups, t_los):
            for j in range(nt):
                s = sc_ref[j, rs, :]
                sel = jnp.logical_and(s >= t_lo, s > NEG_INF)
                sc_ref[j, rs, :] = jnp.where(sel, 0.0, NEG_INF)

    @pl.when(n_tied > 0.0)
    def _():
        tri = tri_ref[...]
        for rs, t, t_lo in zip(groups, ts, t_los):
            t_hi = _key_to_float(t + 1)
            need = kf - count_ge(rs, t_hi)
            c = jnp.zeros((rg, 1), F32)
            for j in range(nt):
                s = sc_ref[j, rs, :]
                gt = s >= t_hi
                eq = jnp.logical_and(s >= t_lo, jnp.logical_not(gt))
                e = jnp.where(eq, 1.0, 0.0)
                rank = _dot(e.astype(BF16), tri) + c
                sel = jnp.logical_or(gt, jnp.logical_and(eq, rank <= need))
                sel = jnp.logical_and(sel, s > NEG_INF)
                sc_ref[j, rs, :] = jnp.where(sel, 0.0, NEG_INF)
                c = c + jnp.sum(e, axis=-1, keepdims=True)


def _dsa_kernel(qi_ref, kk_ref, wi_ref, qb_ref, kb_ref, vb_ref, bias_ref, tri_ref, o_ref, sc_ref, s_ref,
                st_ref, *, topk):
    qblk = pl.program_id(1)
    nt = qblk + 1
    tq = qi_ref.shape[0]
    lo = lax.broadcasted_iota(jnp.int32, (tq, LANES), 1) < HEAD_DIM
    zero = jnp.zeros((tq, LANES), BF16)

    def halves(qp):
        return jnp.where(lo, qp, zero), jnp.where(lo, zero, qp)

    wi = wi_ref[...]

    def index_tile(j, _):
        off = pl.multiple_of(j * TB, TB)
        kk = kk_ref[pl.ds(off, TB), :]
        acc = jnp.zeros((tq, TB), F32)
        for m in range(IDX_HEADS // 2):
            q_lo, q_hi = halves(qi_ref[:, m * LANES:(m + 1) * LANES])
            acc = acc + wi[:, 2 * m:2 * m + 1] * jnp.maximum(_dot_nt(q_lo, kk), 0.0)
            acc = acc + wi[:, 2 * m + 1:2 * m + 2] * jnp.maximum(_dot_nt(q_hi, kk), 0.0)
        row = lax.broadcasted_iota(jnp.int32, (tq, TB), 0) + qblk * TB
        col = lax.broadcasted_iota(jnp.int32, (tq, TB), 1) + j * TB
        sc_ref[j] = jnp.where(col <= row, acc, NEG_INF)
        return 0

    lax.fori_loop(0, nt, index_tile, 0)
    for n_tiles in range(1, sc_ref.shape[0] + 1):
        pl.when(nt == n_tiles)(functools.partial(_topk_select, sc_ref, n_tiles, topk, tri_ref))

    def near(m, j):
        d = _bias_index(qblk - j)
        return jnp.concatenate([bias_ref[2 * m, d], bias_ref[2 * m + 1, d]], axis=0)

    result = _two_pass_attend(qb_ref, kb_ref, vb_ref, nt, near, lambda jc: sc_ref[jc], s_ref, st_ref)
    lo_out = lax.broadcasted_iota(jnp.int32, (tq, LANES), 1) < HEAD_DIM
    for m in range(N_CHAINS):
        acc, l = result(m)
        o = acc / l
        o_ref[:, m * LANES:(m + 1) * LANES] = jnp.where(lo_out, o[:tq], o[tq:]).astype(BF16)


def _dsa_prompt(p, bias_b, tri, b, t, topk):
    nq = t // TB
    return pl.pallas_call(
        functools.partial(_dsa_kernel, topk=topk),
        grid=(b, nq),
        in_specs=[pl.BlockSpec((TB, I_WIDTH), lambda bi, i: (bi * nq + i, 0)),
                  pl.BlockSpec((t, LANES), lambda bi, i: (bi, 0)),
                  pl.BlockSpec((TB, LANES), lambda bi, i: (bi * nq + i, 0)),
                  pl.BlockSpec((TB, B_WIDTH), lambda bi, i: (bi * nq + i, 0)),
                  pl.BlockSpec((t, B_WIDTH), lambda bi, i: (bi, 0)),
                  pl.BlockSpec((t, B_WIDTH), lambda bi, i: (bi, 1)),
                  pl.BlockSpec((B_HEADS, N_BIAS_TILES, TB, TB), lambda bi, i: (0, 0, 0, 0),
                               pipeline_mode=pl.Buffered(1)),
                  pl.BlockSpec((TB, TB), lambda bi, i: (0, 0))],
        out_specs=pl.BlockSpec((TB, B_WIDTH), lambda bi, i: (bi * nq + i, 0)),
        out_shape=jax.ShapeDtypeStruct((b * t, B_WIDTH), BF16),
        scratch_shapes=[pltpu.VMEM((nq, TB, TB), F32)] + _attend_scratch(nq),
        compiler_params=_cparams(("arbitrary", "arbitrary")),
        name="dsa_prompt",
    )(p["qi"], p["kk"], p["wi"], p["qb"], p["kvb"], p["kvb"], bias_b, tri)


def _sample_idx_kernel(pt_ref, q_ref, w_ref, kn_ref, *rest, n_pages):
    page_refs, o_ref = rest[:n_pages], rest[n_pages]
    q = q_ref[0]
    w = w_ref[0]
    nq = q.shape[0] // IDX_HEADS

    def combine(qk):
        rel = jnp.maximum(qk, 0.0) * w
        sc = rel[0:nq]
        for h in range(1, IDX_HEADS):
            sc = sc + rel[h * nq:(h + 1) * nq]
        return sc

    for k in range(n_pages):
        o_ref[k] = combine(_dot(q, page_refs[k][0, 0].astype(BF16)))
    new = jnp.concatenate([kn_ref[0], jnp.zeros((PAGE_ROWS - nq, IDX_DIM), F32)], axis=0).astype(BF16)
    row = lax.broadcasted_iota(jnp.int32, (nq, PAGE_ROWS), 0)
    col = lax.broadcasted_iota(jnp.int32, (nq, PAGE_ROWS), 1)
    o_ref[n_pages] = jnp.where(col <= row, combine(_dot_nt(q, new)), NEG_INF)


def _sample_idx(page_table, q_stack, w_stack, ki_new, cache_kidx_t, layer):
    s, n_pages = page_table.shape
    nq = ki_new.shape[1]
    hq = q_stack.shape[1]
    page_spec = lambda k: pl.BlockSpec((1, 1, IDX_DIM, PAGE_ROWS), lambda i, pt, k=k: (layer, pt[i, k], 0, 0))
    grid_spec = pltpu.PrefetchScalarGridSpec(
        num_scalar_prefetch=1,
        grid=(s,),
        in_specs=[pl.BlockSpec((1, hq, IDX_DIM), lambda i, pt: (i, 0, 0)),
                  pl.BlockSpec((1, hq, 1), lambda i, pt: (i, 0, 0)),
                  pl.BlockSpec((1, nq, IDX_DIM), lambda i, pt: (i, 0, 0))]
                 + [page_spec(k) for k in range(n_pages)],
        out_specs=pl.BlockSpec((n_pages + 1, nq, PAGE_ROWS), lambda i, pt: (0, i, 0)),
    )
    return pl.pallas_call(
        functools.partial(_sample_idx_kernel, n_pages=n_pages),
        grid_spec=grid_spec,
        out_shape=jax.ShapeDtypeStruct((n_pages + 1, s * nq, PAGE_ROWS), F32),
        compiler_params=_cparams(("arbitrary",)),
        name="sample_idx",
    )(page_table, q_stack, w_stack, ki_new, *([cache_kidx_t] * n_pages))


def _select_kernel(sc_ref, tri_ref, o_ref, *, topk):
    o_ref[...] = sc_ref[...]
    _topk_select(o_ref, o_ref.shape[0], topk, tri_ref)


def _sample_select(scores, tri, topk):
    nt, rows, tw = scores.shape
    tr = min(rows, 256)
    return pl.pallas_call(
        functools.partial(_select_kernel, topk=topk),
        grid=(rows // tr,),
        in_specs=[pl.BlockSpec((nt, tr, tw), lambda i: (0, i, 0)),
                  pl.BlockSpec((tw, tw), lambda i: (0, 0))],
        out_specs=pl.BlockSpec((nt, tr, tw), lambda i: (0, i, 0)),
        out_shape=jax.ShapeDtypeStruct(scores.shape, F32),
        compiler_params=_cparams(("arbitrary",)),
        name="sample_select",
    )(scores, tri)


def _sample_attn_kernel(pt_ref, lam_ref, g_ref, qa_ref, qb_ref, mska_ref, mskb_ref, biasa_ref, biasb_ref,
                        mext_ref, bext_ref, kvan_ref, kvbn_ref, selp_ref, seln_ref, *rest, n_chunks, lam_init):
    g_pages = PAGES_PER_STEP
    kva_refs, kvb_refs = rest[:g_pages], rest[g_pages:2 * g_pages]
    oa_ref, ob_ref = rest[2 * g_pages], rest[2 * g_pages + 1]
    qa_s, qn_s, qb_s, ma_s, la_s, acca_s, mb_s, lb_s, accb_s = rest[2 * g_pages + 2:]
    c = pl.program_id(1)
    nq = qa_ref.shape[1]
    rows_a = A_HEADS * 2 * nq
    rows_b = B_HEADS * nq

    @pl.when(c == 0)
    def _():
        qa_s[...] = (jnp.concatenate([qa_ref[0]] * (2 * A_HEADS), axis=0) * mska_ref[...]).astype(BF16)
        qn_s[...] = jnp.concatenate([_masked_pair(qa_ref[0, :, h * LANES:(h + 1) * LANES])
                                     for h in range(A_HEADS)], axis=0).astype(BF16)
        qb_s[...] = (jnp.concatenate([qb_ref[0]] * B_HEADS, axis=0) * mskb_ref[...]).astype(BF16)
        ma_s[...] = jnp.full(ma_s.shape, NEG_INF, F32)
        mb_s[...] = jnp.full(mb_s.shape, NEG_INF, F32)
        la_s[...] = jnp.zeros(la_s.shape, F32)
        lb_s[...] = jnp.zeros(lb_s.shape, F32)
        acca_s[...] = jnp.zeros(acca_s.shape, F32)
        accb_s[...] = jnp.zeros(accb_s.shape, F32)

    def diag_a(r):
        return jnp.concatenate(
            [r[h * 2 * nq:(h + 1) * 2 * nq, h * LANES:(h + 1) * LANES] for h in range(A_HEADS)], axis=0)

    def diag_b(r):
        return jnp.concatenate(
            [r[m * 2 * nq:(m + 1) * 2 * nq, m * LANES:(m + 1) * LANES] for m in range(B_HEADS // 2)], axis=0)

    def update_a(s, values):
        mn = jnp.maximum(ma_s[...], jnp.max(s, axis=-1, keepdims=True))
        p = jnp.exp(s - mn)
        al = jnp.exp(ma_s[...] - mn)
        la_s[...] = al * la_s[...] + jnp.sum(p, axis=-1, keepdims=True)
        acca_s[...] = al * acca_s[...] + values(p.astype(BF16))
        ma_s[...] = mn

    def update_b(s, values):
        mn = jnp.maximum(mb_s[...], jnp.max(s, axis=-1, keepdims=True))
        ms = jnp.where(mn == NEG_INF, 0.0, mn)
        p = jnp.exp(s - ms)
        al = jnp.exp(mb_s[...] - ms)
        lb_s[...] = al * lb_s[...] + jnp.sum(p, axis=-1, keepdims=True)
        accb_s[...] = al * accb_s[...] + diag_b(values(p.astype(BF16)))
        mb_s[...] = mn

    def tile_rows(x, n):
        return jnp.concatenate([x] * n, axis=0)

    last = c == n_chunks - 1
    far = (g_pages - 1) * PAGE_ROWS
    bias_b = jnp.concatenate([jnp.zeros((rows_b, far), F32), jnp.where(last, biasb_ref[:, :PAGE_ROWS], 0.0)], axis=1)

    ext = A_HEADS * PAGE_ROWS
    page_keys = lambda ref, kv: ref[0, 0, :, kv].reshape(ext, 2 * HEAD_DIM).astype(BF16)
    qn = qn_s[...]
    s_pages = [_dot_nt(qn, page_keys(ref, 0)) for ref in kva_refs]
    s_pages = ([sp + mext_ref[...] for sp in s_pages[:-1]]
               + [s_pages[-1] + jnp.where(last, bext_ref[...], mext_ref[...])])

    def page_values(p):
        out = None
        for g, ref in enumerate(kva_refs):
            part = _dot(p[:, g * ext:(g + 1) * ext], page_keys(ref, 1))
            out = part if out is None else out + part
        return out

    update_a(jnp.concatenate(s_pages, axis=1), page_values)

    kt = jnp.concatenate([ref[0, 0, 0].reshape(B_WIDTH, PAGE_ROWS) for ref in kvb_refs], axis=1).astype(BF16)
    vt = jnp.concatenate([ref[0, 0, 1].reshape(B_WIDTH, PAGE_ROWS) for ref in kvb_refs], axis=1).astype(BF16)
    sel = jnp.concatenate([tile_rows(selp_ref[g], B_HEADS) for g in range(g_pages)], axis=1)
    update_b(_dot(qb_s[...], kt) + bias_b + sel, lambda p: _dot_nt(p, vt))

    @pl.when(last)
    def _():
        pad = lambda x: jnp.concatenate([x, jnp.zeros((PAGE_ROWS - nq, x.shape[1]), F32)], axis=0).astype(BF16)
        kn, vn = pad(kvan_ref[0, :, :A_WIDTH]), pad(kvan_ref[0, :, A_WIDTH:])
        update_a(_dot_nt(qa_s[...], kn) + biasa_ref[...], lambda p: diag_a(_dot(p, vn)))
        kn, vn = pad(kvbn_ref[0, :, :B_WIDTH]), pad(kvbn_ref[0, :, B_WIDTH:])
        update_b(_dot_nt(qb_s[...], kn) + biasb_ref[:, PAGE_ROWS:] + tile_rows(seln_ref[0], B_HEADS),
                 lambda p: _dot(p, vn))

        lam = _lambda_value(lam_ref, lam_init)
        oa = acca_s[...] / la_s[...]
        for h in range(A_HEADS):
            o = oa[h * 2 * nq:h * 2 * nq + nq] - lam * oa[h * 2 * nq + nq:(h + 1) * 2 * nq]
            oa_ref[0, :, h * LANES:(h + 1) * LANES] = _sub_layer_norm(o, g_ref[...], lam_init)
        ob = accb_s[...] / lb_s[...]
        lo = lax.broadcasted_iota(jnp.int32, (nq, LANES), 1) < HEAD_DIM
        for m in range(B_HEADS // 2):
            ob_ref[0, :, m * LANES:(m + 1) * LANES] = jnp.where(
                lo, ob[2 * m * nq:(2 * m + 1) * nq], ob[(2 * m + 1) * nq:(2 * m + 2) * nq])


def _sample_attn(page_table, lam4, g_sub, qa, qb, mask_a, mask_b, bias_a, bias_b, kva_new, kvb_new, sel,
                 cache_a, cache_bt, layer, lam_init):
    s, n_pages = page_table.shape
    nq = qa.shape[1]
    g_pages = PAGES_PER_STEP
    n_chunks = n_pages // g_pages
    rows_a, rows_b = A_HEADS * 2 * nq, B_HEADS * nq
    ext = A_HEADS * PAGE_ROWS
    key_head = jnp.arange(ext)[None, :] % A_HEADS
    row_head = jnp.arange(rows_a)[:, None] // (2 * nq)
    mask_ext = jnp.where(key_head == row_head, 0.0, NEG_INF).astype(F32)
    bias_ext = jnp.repeat(bias_a[:, :PAGE_ROWS], A_HEADS, axis=1) + mask_ext
    const = lambda shape: pl.BlockSpec(shape, lambda i, c, pt: (0,) * len(shape))
    seq = lambda shape: pl.BlockSpec(shape, lambda i, c, pt: (i,) + (0,) * (len(shape) - 1))
    page_a = lambda g: pl.BlockSpec((1, 1) + cache_a.shape[2:],
                                    lambda i, c, pt, g=g: (layer, pt[i, c * g_pages + g], 0, 0, 0, 0))
    page_b = lambda g: pl.BlockSpec((1, 1) + cache_bt.shape[2:],
                                    lambda i, c, pt, g=g: (layer, pt[i, c * g_pages + g], 0, 0, 0, 0))
    grid_spec = pltpu.PrefetchScalarGridSpec(
        num_scalar_prefetch=1,
        grid=(s, n_chunks),
        in_specs=[const((4, HEAD_DIM)), const((1, 2 * HEAD_DIM)),
                  seq((1, nq, A_WIDTH)), seq((1, nq, B_WIDTH)),
                  const((rows_a, A_WIDTH)), const((rows_b, B_WIDTH)),
                  const((rows_a, PAGE_ROWS)), const((rows_b, 2 * PAGE_ROWS)),
                  const((rows_a, ext)), const((rows_a, ext)),
                  seq((1, nq, 2 * A_WIDTH)), seq((1, nq, 2 * B_WIDTH)),
                  pl.BlockSpec((g_pages, nq, PAGE_ROWS), lambda i, c, pt: (c, i, 0)),
                  pl.BlockSpec((1, nq, PAGE_ROWS), lambda i, c, pt: (n_pages, i, 0))]
                 + [page_a(g) for g in range(g_pages)]
                 + [page_b(g) for g in range(g_pages)],
        out_specs=[seq((1, nq, A_WIDTH)), seq((1, nq, B_WIDTH))],
        scratch_shapes=[pltpu.VMEM((rows_a, A_WIDTH), BF16), pltpu.VMEM((rows_a, 2 * HEAD_DIM), BF16),
                        pltpu.VMEM((rows_b, B_WIDTH), BF16),
                        pltpu.VMEM((rows_a, 1), F32), pltpu.VMEM((rows_a, 1), F32),
                        pltpu.VMEM((rows_a, LANES), F32),
                        pltpu.VMEM((rows_b, 1), F32), pltpu.VMEM((rows_b, 1), F32),
                        pltpu.VMEM((rows_b, LANES), F32)],
    )
    return pl.pallas_call(
        functools.partial(_sample_attn_kernel, n_chunks=n_chunks, lam_init=lam_init),
        grid_spec=grid_spec,
        out_shape=[jax.ShapeDtypeStruct((s, nq, A_WIDTH), F32), jax.ShapeDtypeStruct((s, nq, B_WIDTH), F32)],
        compiler_params=_cparams(("arbitrary", "arbitrary")),
        name="sample_attn",
    )(page_table, lam4, g_sub, qa, qb, mask_a, mask_b, bias_a[:, PAGE_ROWS:], bias_b, mask_ext, bias_ext,
      kva_new, kvb_new, sel, sel, *([cache_a] * g_pages), *([cache_bt] * g_pages))


def _outproj_kernel(oa_ref, ob_ref, sg_ref, x_ref, g1_ref, sh2_ref, sc2_ref, wba_ref, wbb_ref, wout_ref,
                    lng_ref, lnb_ref, wr_ref, br_ref, x1_ref, hc_ref, *, alpha):
    ya = _dot(oa_ref[...], wba_ref[...])
    yb = _dot(ob_ref[...], wbb_ref[...])
    t = sg_ref[:, :D_MODEL] * ya + sg_ref[:, D_MODEL:] * yb
    mix = _dot(t.astype(BF16), wout_ref[...])
    x1 = _layer_norm(alpha * x_ref[...] + g1_ref[0] * mix, lng_ref[...], lnb_ref[...])
    x1_ref[...] = x1
    h2 = x1 * (1.0 + sc2_ref[0]) + sh2_ref[0]
    hc_ref[:, :D_MODEL] = h2

    logits = _dot3(h2, wr_ref[...]) + br_ref[...]
    lane = lax.broadcasted_iota(jnp.int32, logits.shape, 1).astype(F32)
    big = float(LANES)
    is_group = jnp.logical_and(lane >= N_EXPERTS, lane < N_EXPERTS + N_GROUPS)
    lg = jnp.where(is_group, logits, NEG_INF)
    mg = jnp.max(lg, axis=-1, keepdims=True)
    g_sel = jnp.min(jnp.where(lg == mg, lane, big), axis=-1, keepdims=True) - N_EXPERTS
    p_g = 1.0 / jnp.sum(jnp.exp(lg - mg), axis=-1, keepdims=True)
    first = g_sel * EXPERTS_PER_GROUP
    in_group = jnp.logical_and(lane >= first, lane < first + EXPERTS_PER_GROUP)
    le = jnp.where(in_group, logits, NEG_INF)
    ex = jnp.exp(le - jnp.max(le, axis=-1, keepdims=True))
    pe = jnp.where(in_group, ex / jnp.sum(ex, axis=-1, keepdims=True), -1.0)
    v1 = jnp.max(pe, axis=-1, keepdims=True)
    i1 = jnp.min(jnp.where(pe == v1, lane, big), axis=-1, keepdims=True)
    pe2 = jnp.where(lane == i1, -1.0, pe)
    v2 = jnp.max(pe2, axis=-1, keepdims=True)
    i2 = jnp.min(jnp.where(pe2 == v2, lane, big), axis=-1, keepdims=True)
    tot = v1 + v2
    hc_ref[:, D_MODEL:] = (jnp.where(lane == i1, p_g * (v1 / tot), 0.0)
                           + jnp.where(lane == i2, p_g * (v2 / tot), 0.0)
                           + jnp.where(lane == GROUP_LANE, g_sel, 0.0))


def _out_proj(oa, ob, sg, x, ada3, wba, wbb, wout, lng, lnb, wr, br, rows_per_batch, alpha):
    n = x.shape[0]
    tm = ROW_TILE
    row = lambda w: pl.BlockSpec((tm, w), lambda i: (i, 0))
    full = lambda a: pl.BlockSpec(a.shape, lambda i: (0,) * a.ndim)
    return pl.pallas_call(
        functools.partial(_outproj_kernel, alpha=alpha),
        grid=(n // tm,),
        in_specs=[row(A_WIDTH), row(B_WIDTH), row(2 * D_MODEL), row(D_MODEL),
                  _row_vec_spec(ada3, 2, tm, rows_per_batch),
                  _row_vec_spec(ada3, 3, tm, rows_per_batch),
                  _row_vec_spec(ada3, 4, tm, rows_per_batch),
                  full(wba), full(wbb), full(wout), full(lng), full(lnb), full(wr), full(br)],
        out_specs=[row(D_MODEL), row(HC_WIDTH)],
        out_shape=[jax.ShapeDtypeStruct((n, D_MODEL), F32), jax.ShapeDtypeStruct((n, HC_WIDTH), F32)],
        compiler_params=_cparams(("arbitrary",)),
        name="out_proj",
    )(oa, ob, sg, x, ada3, ada3, ada3, wba, wbb, wout, lng, lnb, wr, br)


def _moe_route_kernel(hc_ref, tri_ref, upper_ref, slot_ref, tile_ref, cnt_ref, off_ref):
    phase, i = pl.program_id(0), pl.program_id(1)
    route = hc_ref[...]
    lane = lax.broadcasted_iota(jnp.int32, route.shape, 1)
    group = jnp.sum(jnp.where(lane == GROUP_LANE, route, 0.0), axis=-1, keepdims=True)
    onehot = jnp.where(lane.astype(F32) == group, 1.0, 0.0)
    col_sum = jnp.sum(onehot, axis=0, keepdims=True)

    @pl.when(jnp.logical_and(phase == 0, i == 0))
    def _():
        cnt_ref[...] = jnp.zeros(cnt_ref.shape, F32)

    @pl.when(phase == 0)
    def _():
        cnt_ref[0:1, :] += col_sum
        slot_ref[...] = jnp.zeros(slot_ref.shape, jnp.int32)

    @pl.when(jnp.logical_and(phase == 1, i == 0))
    def _():
        padded = jnp.ceil(cnt_ref[...] / MOE_ROWS) * MOE_ROWS
        start = _dot3(padded, upper_ref[...])
        off_ref[0] = start
        off_ref[1] = start + padded
        cnt_ref[...] = jnp.zeros(cnt_ref.shape, F32)
        sub = lax.broadcasted_iota(jnp.int32, cnt_ref.shape, 0)
        lane8 = lax.broadcasted_iota(jnp.int32, cnt_ref.shape, 1)
        lane1 = lax.broadcasted_iota(jnp.int32, (1, LANES), 1)
        tile_start = ((sub * LANES + lane8) * MOE_ROWS).astype(F32)
        ends = off_ref[1, 0:1, :]
        tile_group = jnp.zeros(cnt_ref.shape, F32)
        for g in range(N_GROUPS):
            end_g = jnp.sum(jnp.where(lane1 == g, ends, 0.0), axis=-1, keepdims=True)
            tile_group = tile_group + jnp.where(tile_start >= end_g, 1.0, 0.0)
        tile_ref[...] = tile_group.astype(jnp.int32)

    @pl.when(phase == 1)
    def _():
        earlier = _dot(tri_ref[...], onehot.astype(BF16))
        base = off_ref[0, 0:1, :] + cnt_ref[0:1, :]
        slot = jnp.sum(onehot * (base + earlier), axis=-1, keepdims=True)
        slot_ref[...] = jnp.broadcast_to(slot, route.shape).astype(jnp.int32)
        cnt_ref[0:1, :] += col_sum


def _moe_route(hc):
    n = hc.shape[0]
    tm = min(ROUTE_ROWS, n)
    tri = (jnp.arange(tm)[:, None] > jnp.arange(tm)[None, :]).astype(BF16)
    upper = (jnp.arange(LANES)[:, None] < jnp.arange(LANES)[None, :]).astype(F32)
    return pl.pallas_call(
        _moe_route_kernel,
        grid=(2, n // tm),
        in_specs=[pl.BlockSpec((tm, LANES), lambda ph, i: (i, D_MODEL // LANES)),
                  pl.BlockSpec((tm, tm), lambda ph, i: (0, 0)),
                  pl.BlockSpec((LANES, LANES), lambda ph, i: (0, 0))],
        out_specs=[pl.BlockSpec((tm, LANES), lambda ph, i: (i * ph, 0)),
                   pl.BlockSpec((8, LANES), lambda ph, i: (0, 0))],
        out_shape=[jax.ShapeDtypeStruct((n, LANES), jnp.int32), jax.ShapeDtypeStruct((8, LANES), jnp.int32)],
        scratch_shapes=[pltpu.VMEM((8, LANES), F32), pltpu.VMEM((2, 8, LANES), F32)],
        compiler_params=_cparams(("arbitrary", "arbitrary")),
        name="moe_route",
    )(hc, tri, upper)


def _row_copies(n_rows, copy):
    def start_pair(i, c):
        copy(2 * i).start(priority=0)
        copy(2 * i + 1).start(priority=1)
        return c
    lax.fori_loop(0, n_rows // 2, start_pair, 0, unroll=4)
    lax.fori_loop(0, n_rows, lambda t, c: (copy(0).wait(), c)[1], 0, unroll=8)


def _moe_dispatch_kernel(slot_ref, hc_ref, init_ref, out_ref, sem):
    del init_ref
    copy = lambda t: pltpu.make_async_copy(hc_ref.at[pl.ds(t, 1)], out_ref.at[pl.ds(slot_ref[t], 1)], sem)
    _row_copies(hc_ref.shape[0], copy)


def _moe_dispatch(slot, hc, n_rows):
    n = hc.shape[0]
    tm = MOE_ROWS
    return pl.pallas_call(
        _moe_dispatch_kernel,
        grid=(n // tm,),
        in_specs=[pl.BlockSpec((tm,), lambda i: (i,), memory_space=pltpu.SMEM),
                  pl.BlockSpec((tm, HC_WIDTH), lambda i: (i, 0)),
                  pl.BlockSpec(memory_space=pl.ANY)],
        out_specs=pl.BlockSpec(memory_space=pl.ANY),
        out_shape=jax.ShapeDtypeStruct((n_rows, HC_WIDTH), F32),
        scratch_shapes=[pltpu.SemaphoreType.DMA(())],
        input_output_aliases={2: 0},
        compiler_params=_cparams(("arbitrary",)),
        name="moe_dispatch",
    )(slot, hc, jnp.zeros((n_rows, HC_WIDTH), F32))


def _moe_kernel(tile_ref, hc_ref, wup_ref, wdn_ref, o_ref):
    group = tile_ref[pl.program_id(0)]

    @pl.when(group < N_GROUPS)
    def _():
        h = hc_ref[:, :D_MODEL].astype(BF16)
        comb = hc_ref[:, D_MODEL:]
        lane = lax.broadcasted_iota(jnp.int32, comb.shape, 1)
        acc = jnp.zeros(o_ref.shape, F32)
        for e in range(EXPERTS_PER_GROUP):
            hid = _dot(h, wup_ref[e])
            gate, up = hid[:, :EXPERT_HIDDEN], hid[:, EXPERT_HIDDEN:]
            act = (gate * _sigmoid(gate) * up).astype(BF16)
            w = jnp.sum(jnp.where(lane == group * EXPERTS_PER_GROUP + e, comb, 0.0), axis=-1, keepdims=True)
            acc = acc + w * _dot(act, wdn_ref[e])
        o_ref[...] = acc

    @pl.when(group >= N_GROUPS)
    def _():
        o_ref[...] = jnp.zeros(o_ref.shape, F32)


def _moe_experts(tile_group, hc_sorted, wup, wdn):
    n_rows = hc_sorted.shape[0]
    tm = MOE_ROWS
    group = lambda i, tg: (jnp.minimum(tg[i], N_GROUPS - 1), 0, 0)
    grid_spec = pltpu.PrefetchScalarGridSpec(
        num_scalar_prefetch=1,
        grid=(n_rows // tm,),
        in_specs=[pl.BlockSpec((tm, HC_WIDTH), lambda i, tg: (i, 0)),
                  pl.BlockSpec((EXPERTS_PER_GROUP, D_MODEL, 2 * EXPERT_HIDDEN), group),
                  pl.BlockSpec((EXPERTS_PER_GROUP, EXPERT_HIDDEN, D_MODEL), group)],
        out_specs=pl.BlockSpec((tm, D_MODEL), lambda i, tg: (i, 0)),
    )
    return pl.pallas_call(
        _moe_kernel,
        grid_spec=grid_spec,
        out_shape=jax.ShapeDtypeStruct((n_rows, D_MODEL), F32),
        compiler_params=_cparams(("arbitrary",)),
        name="moe_experts",
    )(tile_group, hc_sorted, wup, wdn)


def _moe_combine_kernel(slot_ref, y_ref, x1_ref, g2_ref, lng_ref, lnb_ref, o_ref, buf_ref, sem, *, alpha):
    copy = lambda t: pltpu.make_async_copy(y_ref.at[pl.ds(slot_ref[t], 1)], buf_ref.at[pl.ds(t, 1)], sem)
    _row_copies(buf_ref.shape[0], copy)
    u = alpha * x1_ref[...] + g2_ref[0] * buf_ref[...]
    o_ref[...] = _layer_norm(u, lng_ref[...], lnb_ref[...])


def _moe_combine(slot, y_sorted, x1, ada3, lng, lnb, rows_per_batch, alpha):
    n = x1.shape[0]
    tm = MOE_ROWS
    full = lambda a: pl.BlockSpec(a.shape, lambda i: (0,) * a.ndim)
    return pl.pallas_call(
        functools.partial(_moe_combine_kernel, alpha=alpha),
        grid=(n // tm,),
        in_specs=[pl.BlockSpec((tm,), lambda i: (i,), memory_space=pltpu.SMEM),
                  pl.BlockSpec(memory_space=pl.ANY),
                  pl.BlockSpec((tm, D_MODEL), lambda i: (i, 0)),
                  _row_vec_spec(ada3, 5, tm, rows_per_batch),
                  full(lng), full(lnb)],
        out_specs=pl.BlockSpec((tm, D_MODEL), lambda i: (i, 0)),
        out_shape=jax.ShapeDtypeStruct((n, D_MODEL), F32),
        scratch_shapes=[pltpu.VMEM((tm, D_MODEL), F32), pltpu.SemaphoreType.DMA(())],
        compiler_params=_cparams(("arbitrary",)),
        name="moe_combine",
    )(slot, y_sorted, x1, ada3, lng, lnb)


def _moe(hc, x1, ada3, wup, wdn, lng, lnb, rows_per_batch, alpha):
    n = x1.shape[0]
    n_rows = n + N_GROUPS * MOE_ROWS
    slot_lanes, tile_table = _moe_route(hc)
    slot = slot_lanes[:, 0]
    tile_group = tile_table.reshape(-1)[:n_rows // MOE_ROWS]
    hc_sorted = _moe_dispatch(slot, hc, n_rows)
    y_sorted = _moe_experts(tile_group, hc_sorted, wup, wdn)
    return _moe_combine(slot, y_sorted, x1, ada3, lng, lnb, rows_per_batch, alpha)


def _prompt_dist():
    i = jnp.arange(TB, dtype=jnp.int32)[:, None]
    j = jnp.arange(TB, dtype=jnp.int32)[None, :]
    return jnp.concatenate([d * TB + i - j for d in range(N_BIAS_TILES)], axis=0)


def _sample_dist(nq, past_len):
    i = jnp.arange(nq, dtype=jnp.int32)[:, None]
    j = jnp.arange(PAGE_ROWS, dtype=jnp.int32)[None, :]
    last_page = past_len + i - (past_len - PAGE_ROWS + j)
    new = jnp.where(j < nq, i - j, -1)
    return jnp.concatenate([last_page, new], axis=1)


def kernel(x_prompt, x_sample, cache_kv_diff, cache_kv_dsa, cache_kidx, page_table, c_prompt, c_sample,
           rel_bias, w_ada, b_ada, w_in, lambda_q1, lambda_k1, lambda_q2, lambda_k2, subln_g, w_branch_a,
           w_branch_b, w_out, ln1_g, ln1_b, w_router_group, b_router_group, w_router_expert,
           b_router_expert, w_up, w_down, ln2_g, ln2_b):
    b, t, d = x_prompt.shape
    s, nq, _ = x_sample.shape
    depth = w_in.shape[0]
    n_pool = cache_kidx.shape[1]
    cache_a = cache_kv_diff
    cache_bt = jnp.transpose(cache_kv_dsa, (0, 1, 3, 4, 5, 2))
    cache_it = jnp.transpose(cache_kidx, (0, 1, 3, 2))
    n_pages = page_table.shape[1]
    past_len = n_pages * PAGE_ROWS
    alpha = (2 * depth) ** 0.25
    topk_p = min(DSA_TOPK_MAX, t // 4)
    topk_s = min(DSA_TOPK_MAX, (past_len + nq) // 4)
    assert d == D_MODEL and t % TB == 0 and (s * nq) % ROW_TILE == 0 and n_pages % PAGES_PER_STEP == 0
    assert MOE_ROWS == ROW_TILE == TB and B_HEADS == 2 * N_CHAINS and A_HEADS % N_CHAINS == 0
    assert cache_kidx.shape[2] == PAGE_ROWS and nq <= 8

    bias_p = _bias_tiles(rel_bias, _prompt_dist()).reshape(A_HEADS + B_HEADS, N_BIAS_TILES, TB, TB)
    bias_s = _bias_tiles(rel_bias, _sample_dist(nq, past_len))
    bias_sa = jnp.broadcast_to(bias_s[:A_HEADS, None], (A_HEADS, 2, nq, 2 * PAGE_ROWS)).reshape(
        A_HEADS * 2 * nq, 2 * PAGE_ROWS)
    bias_sb = bias_s[A_HEADS:].reshape(B_HEADS * nq, 2 * PAGE_ROWS)
    tri_p = (jnp.arange(TB)[:, None] <= jnp.arange(TB)[None, :]).astype(BF16)
    tri_s = tri_p[:PAGE_ROWS, :PAGE_ROWS]
    lane_a = jnp.arange(A_WIDTH)[None, :] // HEAD_DIM
    mask_a = (lane_a == (jnp.arange(A_HEADS * 2 * nq)[:, None] // nq)).astype(F32)
    lane_b = jnp.arange(B_WIDTH)[None, :] // HEAD_DIM
    mask_b = (lane_b == (jnp.arange(B_HEADS * nq)[:, None] // nq)).astype(F32)

    xp = x_prompt.reshape(b * t, d)
    xs = x_sample.reshape(s * nq, d)
    c_all = jnp.concatenate([c_prompt, c_sample], axis=0)
    c_all = jnp.pad(c_all, ((0, -(b + s) % 8), (0, 0)))
    outs = [[] for _ in range(6)]
    for l in range(depth):
        lam_init = 0.8 - 0.6 * math.exp(-0.3 * l)
        lam4 = jnp.stack([lambda_q1[l], lambda_k1[l], lambda_q2[l], lambda_k2[l]]).astype(F32)
        g_sub = subln_g[l].reshape(1, 2 * HEAD_DIM)
        w = w_in[l]
        w2 = jnp.concatenate([w[:, :C_KI + IDX_DIM], w[:, C_KI:C_KI + IDX_DIM],
                              w[:, C_KI + IDX_DIM:C_KI + IDX_DIM + IDX_HEADS],
                              jnp.zeros((d, LANES - IDX_HEADS), w.dtype),
                              w[:, C_KI + IDX_DIM + IDX_HEADS:]], axis=1).astype(BF16)
        wba, wbb, wout = w_branch_a[l].astype(BF16), w_branch_b[l].astype(BF16), w_out[l].astype(BF16)
        wup, wdn = w_up[l].astype(BF16), w_down[l].astype(BF16)
        wr = jnp.concatenate([w_router_expert[l], w_router_group[l],
                              jnp.zeros((d, LANES - N_EXPERTS - N_GROUPS), F32)], axis=1)
        br = jnp.concatenate([b_router_expert[l], b_router_group[l],
                              jnp.zeros((LANES - N_EXPERTS - N_GROUPS,), F32)]).reshape(1, LANES)
        ln1 = (ln1_g[l].reshape(1, d), ln1_b[l].reshape(1, d))
        ln2 = (ln2_g[l].reshape(1, d), ln2_b[l].reshape(1, d))

        ada = _ada(c_all, w_ada[l], b_ada[l])
        ada_p = ada[:b].reshape(b, 1, 6 * d)
        ada_s = jnp.broadcast_to(ada[b:b + s, None], (s, nq, 6 * d)).reshape(s * nq // ROW_TILE, ROW_TILE, 6 * d)

        p = _in_proj(xp, ada_p, w2, t, True)
        oa = _diff_attn_prompt(lam4, g_sub, p["qa"], p["kva"], bias_p[:A_HEADS], b, t, lam_init)
        ob = _dsa_prompt(p, bias_p[A_HEADS:], tri_p, b, t, topk_p)
        x1, hc = _out_proj(oa, ob, p["sg"], xp, ada_p, wba, wbb, wout, *ln1, wr, br, t, alpha)
        xp = _moe(hc, x1, ada_p, wup, wdn, *ln2, t, alpha)
        outs[0].append(p["ra"].reshape(b, t, 2, A_HEADS, 2 * HEAD_DIM))
        outs[1].append(p["rb"].reshape(b, 2, B_HEADS, HEAD_DIM, t).transpose(0, 4, 1, 2, 3))
        outs[2].append(p["ki"].reshape(b, t, IDX_DIM))

        q = _in_proj(xs, ada_s, w2, nq, False)
        q_stack = q["qi"].reshape(s, nq, IDX_HEADS, IDX_DIM).transpose(0, 2, 1, 3).reshape(
            s, IDX_HEADS * nq, IDX_DIM)
        w_stack = q["wi"][:, :IDX_HEADS].reshape(s, nq, IDX_HEADS).transpose(0, 2, 1).reshape(
            s, IDX_HEADS * nq, 1)
        scores = _sample_idx(page_table, q_stack, w_stack, q["ki"].reshape(s, nq, IDX_DIM), cache_it, l)
        sel = _sample_select(scores, tri_s, topk_s)
        oa_s, ob_s = _sample_attn(
            page_table, lam4, g_sub,
            q["qa"].astype(F32).reshape(s, nq, A_WIDTH), q["qb"].astype(F32).reshape(s, nq, B_WIDTH),
            mask_a, mask_b, bias_sa, bias_sb,
            q["ra"].reshape(s, nq, 2 * A_WIDTH), q["rb"].reshape(s, nq, 2 * B_WIDTH), sel,
            cache_a, cache_bt, l, lam_init)
        x1, hc = _out_proj(oa_s.reshape(s * nq, A_WIDTH).astype(BF16),
                                 ob_s.reshape(s * nq, B_WIDTH).astype(BF16),
                                 q["sg"], xs, ada_s, wba, wbb, wout, *ln1, wr, br, nq, alpha)
        xs = _moe(hc, x1, ada_s, wup, wdn, *ln2, nq, alpha)
        outs[3].append(q["ra"].reshape(s, nq, 2, A_HEADS, 2 * HEAD_DIM))
        outs[4].append(q["rb"].reshape(s, nq, 2, B_HEADS, HEAD_DIM))
        outs[5].append(q["ki"].reshape(s, nq, IDX_DIM))

    return (xp.reshape(b, t, d), xs.reshape(s, nq, d)) + tuple(jnp.stack(o, 0) for o in outs)
```

```python
import functools
import math

import jax
import jax.numpy as jnp
from jax import lax
from jax.experimental import pallas as pl
from jax.experimental.pallas import tpu as pltpu

D_MODEL = 1024
HEAD_DIM = 64
A_HEADS = 8
B_HEADS = 8
IDX_HEADS = 8
IDX_DIM = 64
DSA_TOPK_MAX = 256
N_BUCKETS = 32
MAX_DISTANCE = 128
N_GROUPS = 4
EXPERTS_PER_GROUP = 4
N_EXPERTS = N_GROUPS * EXPERTS_PER_GROUP
EXPERT_HIDDEN = 512
LN_EPS = 1e-5

LANES = 128
TB = 256
ROW_TILE = 256
MOE_ROWS = 256
ROUTE_ROWS = 512
PAGE_ROWS = 128
PAGES_PER_STEP = 8
VMEM_LIMIT = 56 * 1024 * 1024

A_WIDTH = A_HEADS * 2 * HEAD_DIM
B_WIDTH = B_HEADS * HEAD_DIM
I_WIDTH = IDX_HEADS * IDX_DIM
C_QA, C_KVA, C_QB, C_KVB, C_QI, C_KI, C_WI, C_G = 0, 1024, 3072, 3584, 4608, 5120, 5248, 5376
W2_WIDTH = C_G + 2 * D_MODEL
HC_WIDTH = D_MODEL + LANES
GROUP_LANE = N_EXPERTS + N_GROUPS

F32 = jnp.float32
BF16 = jnp.bfloat16
NEG_INF = float("-inf")
INT_MIN = -2 ** 31


def _cparams(sem):
    return pltpu.CompilerParams(dimension_semantics=sem, vmem_limit_bytes=VMEM_LIMIT)


def _dot(a, b):
    return jnp.dot(a, b, preferred_element_type=F32)


def _dot_nt(a, b):
    return lax.dot_general(a, b, (((1,), (1,)), ((), ())), preferred_element_type=F32)


def _split(a):
    hi = a.astype(BF16)
    lo = (a - hi.astype(F32)).astype(BF16)
    return hi, lo


def _dot3(a, b):
    a_hi, a_lo = _split(a)
    b_hi, b_lo = _split(b)
    return _dot(a_hi, b_hi) + _dot(a_lo, b_hi) + _dot(a_hi, b_lo)


def _sigmoid(x):
    return 1.0 / (1.0 + jnp.exp(-x))


def _layer_norm(u, g, b):
    mu = jnp.mean(u, axis=-1, keepdims=True)
    d = u - mu
    var = jnp.mean(d * d, axis=-1, keepdims=True)
    return d * lax.rsqrt(var + LN_EPS) * g + b


def _bias_kernel(tab_ref, dist_ref, out_ref):
    h = pl.program_id(0)
    d = dist_ref[...]
    n = jnp.maximum(d, 0)
    max_exact = N_BUCKETS // 2
    nf = jnp.maximum(n, 1).astype(F32)
    large = max_exact + jnp.floor(jnp.log(nf / max_exact) / math.log(MAX_DISTANCE / max_exact)
                                  * (N_BUCKETS - max_exact)).astype(jnp.int32)
    large = jnp.minimum(large, N_BUCKETS - 1)
    bucket = jnp.where(n < max_exact, n, large)
    last = tab_ref[N_BUCKETS - 1, h]
    acc = jnp.zeros(d.shape, F32)
    for m in range(N_BUCKETS - 1):
        acc = jnp.where(bucket == m, tab_ref[m, h] - last, acc)
    out_ref[0] = jnp.where(d < 0, NEG_INF, acc)


def _bias_tiles(rel_bias, dist):
    n_heads = rel_bias.shape[1]
    r, c = dist.shape
    return pl.pallas_call(
        _bias_kernel,
        grid=(n_heads,),
        in_specs=[pl.BlockSpec(memory_space=pltpu.SMEM),
                  pl.BlockSpec((r, c), lambda h: (0, 0))],
        out_specs=pl.BlockSpec((1, r, c), lambda h: (h, 0, 0)),
        out_shape=jax.ShapeDtypeStruct((n_heads, r, c), F32),
        compiler_params=_cparams(("arbitrary",)),
        name="bias_tiles",
    )(rel_bias, dist)


def _ada_kernel(c_ref, w_ref, b_ref, o_ref):
    c = c_ref[...]
    o_ref[...] = _dot3(c * _sigmoid(c), w_ref[...]) + b_ref[...]


def _ada(c_all, w_ada, b_ada):
    r, d = c_all.shape
    n = w_ada.shape[1]
    tn = 512
    return pl.pallas_call(
        _ada_kernel,
        grid=(n // tn,),
        in_specs=[pl.BlockSpec((r, d), lambda j: (0, 0)),
                  pl.BlockSpec((d, tn), lambda j: (0, j)),
                  pl.BlockSpec((1, tn), lambda j: (0, j))],
        out_specs=pl.BlockSpec((r, tn), lambda j: (0, j)),
        out_shape=jax.ShapeDtypeStruct((r, n), F32),
        compiler_params=_cparams(("arbitrary",)),
        name="ada",
    )(c_all, w_ada, b_ada.reshape(1, n))


def _inproj_kernel(x_ref, sh_ref, sc_ref, w_ref, qa_ref, ra_ref, kva_ref, qb_ref, rb_ref, kvb_ref,
                   qi_ref, ki_ref, kk_ref, wi_ref, sg_ref):
    h = (x_ref[...] * (1.0 + sc_ref[0]) + sh_ref[0]).astype(BF16)
    q_scale = HEAD_DIM ** -0.5

    def mm(c0, n):
        return _dot(h, w_ref[:, c0:c0 + n])

    for c in range(0, A_WIDTH, 512):
        qa_ref[:, c:c + 512] = (mm(C_QA + c, 512) * q_scale).astype(BF16)
    for c in range(0, 2 * A_WIDTH, 512):
        a = mm(C_KVA + c, 512)
        ra_ref[:, c:c + 512] = a
        kva_ref[:, c:c + 512] = a.astype(BF16)
    qb_ref[...] = (mm(C_QB, 512) * q_scale).astype(BF16)
    for c in range(0, 2 * B_WIDTH, 512):
        a = mm(C_KVB + c, 512)
        if len(rb_ref.shape) == 3:
            rb_ref[0, c:c + 512, :] = a.T
        else:
            rb_ref[:, c:c + 512] = a
        kvb_ref[:, c:c + 512] = a.astype(BF16)
    qi_ref[...] = (mm(C_QI, 512) * q_scale).astype(BF16)
    a = mm(C_KI, 2 * LANES)
    ki_ref[...] = a[:, :IDX_DIM]
    kk_ref[...] = a[:, :LANES].astype(BF16)
    wi_ref[...] = a[:, LANES:] * IDX_HEADS ** -0.5
    for c in range(0, 2 * D_MODEL, 512):
        sg_ref[:, c:c + 512] = _sigmoid(mm(C_G + c, 512))


def _row_vec_spec(arr, col, tm, rows_per_batch):
    if arr.shape[1] == 1:
        per = rows_per_batch // tm
        return pl.BlockSpec((1, 1, D_MODEL), lambda i, *_: (i // per, 0, col))
    return pl.BlockSpec((1, tm, D_MODEL), lambda i, *_: (i, 0, col))


def _in_proj(x, ada3, w2, rows_per_batch, rb_token_minor):
    n = x.shape[0]
    tm = ROW_TILE
    row = lambda w: pl.BlockSpec((tm, w), lambda i: (i, 0))
    outs = [("qa", A_WIDTH, BF16), ("ra", 2 * A_WIDTH, F32), ("kva", 2 * A_WIDTH, BF16),
            ("qb", B_WIDTH, BF16), ("rb", 2 * B_WIDTH, F32), ("kvb", 2 * B_WIDTH, BF16),
            ("qi", I_WIDTH, BF16), ("ki", IDX_DIM, F32), ("kk", LANES, BF16), ("wi", LANES, F32),
            ("sg", 2 * D_MODEL, F32)]
    out_specs = [row(w) for _, w, _ in outs]
    out_shape = [jax.ShapeDtypeStruct((n, w), dt) for _, w, dt in outs]
    if rb_token_minor:
        per = rows_per_batch // tm
        out_specs[4] = pl.BlockSpec((1, 2 * B_WIDTH, tm), lambda i: (i // per, 0, i % per))
        out_shape[4] = jax.ShapeDtypeStruct((n // rows_per_batch, 2 * B_WIDTH, rows_per_batch), F32)
    res = pl.pallas_call(
        _inproj_kernel,
        grid=(n // tm,),
        in_specs=[row(D_MODEL),
                  _row_vec_spec(ada3, 0, tm, rows_per_batch),
                  _row_vec_spec(ada3, 1, tm, rows_per_batch),
                  pl.BlockSpec((D_MODEL, W2_WIDTH), lambda i: (0, 0), pipeline_mode=pl.Buffered(1))],
        out_specs=out_specs,
        out_shape=out_shape,
        compiler_params=_cparams(("arbitrary",)),
        name="in_proj",
    )(x, ada3, ada3, w2)
    return {name: r for (name, _, _), r in zip(outs, res)}


def _stack2(x):
    return jnp.concatenate([x, x], axis=0)


LOG2_E = 1.4426950408889634
N_CHAINS = 4
N_BIAS_TILES = 3


def _bias_index(tile_distance):
    return jnp.minimum(tile_distance, N_BIAS_TILES - 1)


def _lane_fold(x, op):
    r = x[:, :LANES]
    for c in range(1, x.shape[1] // LANES):
        r = op(r, x[:, c * LANES:(c + 1) * LANES])
    return r


def _two_pass_attend(q_ref, k_ref, v_ref, nt, near_bias, every_bias, s_ref, st_ref):
    rows = 2 * TB
    st_ref[0] = jnp.full((N_CHAINS, rows, LANES), NEG_INF, F32)
    st_ref[1] = jnp.zeros((N_CHAINS, rows, LANES), F32)
    st_ref[2] = jnp.zeros((N_CHAINS, rows, LANES), F32)

    def rows_of(ref, c, tiles):
        return jnp.concatenate([ref[pl.ds(pl.multiple_of(j * TB, TB), TB), c * LANES:(c + 1) * LANES]
                                for j in tiles], axis=0)

    def scores(jj, tiles, near):
        width = len(tiles) * TB
        every = None
        if every_bias is not None:
            every = _stack2(jnp.concatenate([every_bias(j) for j in tiles], axis=1))
        for c in range(N_CHAINS):
            s = _dot_nt(_masked_pair(q_ref[:, c * LANES:(c + 1) * LANES]), rows_of(k_ref, c, tiles))
            if near:
                s = s + jnp.concatenate([near_bias(c, j) for j in tiles], axis=1)
            if every is not None:
                s = s + every
            s = s * LOG2_E
            s_ref[jj, c, :, :width] = s
            st_ref[0, c] = jnp.maximum(st_ref[0, c], _lane_fold(s, jnp.maximum))

    def values(jj, tiles):
        width = len(tiles) * TB
        for c in range(N_CHAINS):
            s = s_ref[jj, c, :, :width]
            mb = st_ref[0, c]
            p = [jnp.exp2(s[:, i * LANES:(i + 1) * LANES] - mb) for i in range(width // LANES)]
            st_ref[1, c] += _lane_fold(jnp.concatenate(p, axis=1), jnp.add)
            st_ref[2, c] += _dot(jnp.concatenate(p, axis=1).astype(BF16), rows_of(v_ref, c, tiles))

    pair = lambda jj: [2 * jj, 2 * jj + 1]
    n_pairs = nt // 2
    n_far = jnp.maximum(n_pairs - 1, 0)
    odd = nt % 2 == 1
    loop = lambda lo, hi, fn: lax.fori_loop(lo, hi, lambda jj, c: (fn(jj), c)[1], 0)
    loop(0, n_far, lambda jj: scores(jj, pair(jj), False))
    loop(n_far, n_pairs, lambda jj: scores(jj, pair(jj), True))
    pl.when(odd)(lambda: scores(n_pairs, [nt - 1], True))
    for c in range(N_CHAINS):
        st_ref[0, c] = jnp.broadcast_to(jnp.max(st_ref[0, c], axis=-1, keepdims=True), (rows, LANES))
    loop(0, n_pairs, lambda jj: values(jj, pair(jj)))
    pl.when(odd)(lambda: values(n_pairs, [nt - 1]))
    return lambda c: (st_ref[2, c], jnp.sum(st_ref[1, c], axis=-1, keepdims=True))


def _lambda_value(lam_ref, lam_init):
    a = jnp.sum(lam_ref[0:1, :] * lam_ref[1:2, :], axis=-1, keepdims=True)
    b = jnp.sum(lam_ref[2:3, :] * lam_ref[3:4, :], axis=-1, keepdims=True)
    return jnp.exp(a) - jnp.exp(b) + lam_init


def _sub_layer_norm(o, g, lam_init):
    o = o * lax.rsqrt(jnp.mean(o * o, axis=-1, keepdims=True) + LN_EPS)
    return o * g * (1.0 - lam_init)


def _masked_pair(q):
    lo = lax.broadcasted_iota(jnp.int32, q.shape, 1) < HEAD_DIM
    zero = jnp.zeros_like(q)
    return jnp.concatenate([jnp.where(lo, q, zero), jnp.where(lo, zero, q)], axis=0)


def _attend_scratch(nq):
    return [pltpu.VMEM(((nq + 1) // 2, N_CHAINS, 2 * TB, 2 * TB), F32),
            pltpu.VMEM((3, N_CHAINS, 2 * TB, LANES), F32)]


def _diff_attn_kernel(lam_ref, g_ref, q_ref, k_ref, v_ref, bias_ref, o_ref, s_ref, st_ref, *, lam_init):
    qi = pl.program_id(2)
    near = lambda c, j: _stack2(bias_ref[c, _bias_index(qi - j)])
    result = _two_pass_attend(q_ref, k_ref, v_ref, qi + 1, near, None, s_ref, st_ref)
    lam = _lambda_value(lam_ref, lam_init)
    for c in range(N_CHAINS):
        acc, l = result(c)
        o = acc / l
        o = o[:TB] - lam * o[TB:]
        o_ref[:, c * LANES:(c + 1) * LANES] = _sub_layer_norm(o, g_ref[...], lam_init).astype(BF16)


def _diff_attn_prompt(lam4, g_sub, qa, kva, bias_p, b, t, lam_init):
    nq = t // TB
    groups = A_HEADS // N_CHAINS
    width = N_CHAINS * LANES
    return pl.pallas_call(
        functools.partial(_diff_attn_kernel, lam_init=lam_init),
        grid=(b, groups, nq),
        in_specs=[pl.BlockSpec((4, HEAD_DIM), lambda bi, h, i: (0, 0)),
                  pl.BlockSpec((1, 2 * HEAD_DIM), lambda bi, h, i: (0, 0)),
                  pl.BlockSpec((TB, width), lambda bi, h, i: (bi * nq + i, h)),
                  pl.BlockSpec((t, width), lambda bi, h, i: (bi, h)),
                  pl.BlockSpec((t, width), lambda bi, h, i: (bi, groups + h)),
                  pl.BlockSpec((N_CHAINS, N_BIAS_TILES, TB, TB), lambda bi, h, i: (h, 0, 0, 0))],
        out_specs=pl.BlockSpec((TB, width), lambda bi, h, i: (bi * nq + i, h)),
        out_shape=jax.ShapeDtypeStruct((b * t, A_WIDTH), BF16),
        scratch_shapes=_attend_scratch(nq),
        compiler_params=_cparams(("arbitrary", "arbitrary", "arbitrary")),
        name="diff_attn_prompt",
    )(lam4, g_sub, qa, kva, kva, bias_p)


KEY_NEG_INF = INT_MIN + 0x7FFFFF


def _key_to_float(key):
    bits = jnp.where(key < 0, key ^ jnp.int32(0x7FFFFFFF), key)
    return jnp.where(key < KEY_NEG_INF, NEG_INF, lax.bitcast_convert_type(bits, F32))


SELECT_ROW_GROUPS = 4


def _topk_select(sc_ref, nt, topk, tri_ref):
    rows, tw = sc_ref.shape[1], sc_ref.shape[2]
    rg = rows // SELECT_ROW_GROUPS
    groups = [slice(g * rg, (g + 1) * rg) for g in range(SELECT_ROW_GROUPS)]
    kf = float(topk)

    def count_ge(rs, cf):
        acc = None
        for j in range(nt):
            r = _lane_fold(jnp.where(sc_ref[j, rs, :] >= cf, 1.0, 0.0), jnp.add)
            acc = r if acc is None else acc + r
        return jnp.sum(acc, axis=-1, keepdims=True)

    def search(p, ts):
        inc = lax.shift_left(jnp.int32(1), jnp.int32(31) - p)
        out = []
        for rs, t in zip(groups, ts):
            cand = t + inc
            out.append(jnp.where(count_ge(rs, _key_to_float(cand)) >= kf, cand, t))
        return tuple(out)

    t0 = jnp.full((rg, 1), INT_MIN, jnp.int32)
    ts = lax.fori_loop(0, 32, search, (t0,) * SELECT_ROW_GROUPS)
    t_los = [_key_to_float(t) for t in ts]
    n_tied = 0.0
    for rs, t, t_lo in zip(groups, ts, t_los):
        exact = jnp.logical_or(count_ge(rs, t_lo) == kf, t == KEY_NEG_INF)
        n_tied = n_tied + jnp.sum(jnp.where(exact, 0.0, 1.0))

    @pl.when(n_tied == 0.0)
    def _():
        for rs, t_lo in zip(groups, t_los):
            for j in range(nt):
                s = sc_ref[j, rs, :]
                sel = jnp.logical_and(s >= t_lo, s > NEG_INF)
                sc_ref[j, rs, :] = jnp.where(sel, 0.0, NEG_INF)

    @pl.when(n_tied > 0.0)
    def _():
        tri = tri_ref[...]
        for rs, t, t_lo in zip(groups, ts, t_los):
            t_hi = _key_to_float(t + 1)
            need = kf - count_ge(rs, t_hi)
            c = jnp.zeros((rg, 1), F32)
            for j in range(nt):
                s = sc_ref[j, rs, :]
                gt = s >= t_hi
                eq = jnp.logical_and(s >= t_lo, jnp.logical_not(gt))
                e = jnp.where(eq, 1.0, 0.0)
                rank = _dot(e.astype(BF16), tri) + c
                sel = jnp.logical_or(gt, jnp.logical_and(eq, rank <= need))
                sel = jnp.logical_and(sel, s > NEG_INF)
                sc_ref[j, rs, :] = jnp.where(sel, 0.0, NEG_INF)
                c = c + jnp.sum(e, axis=-1, keepdims=True)


def _dsa_kernel(qi_ref, kk_ref, wi_ref, qb_ref, kb_ref, vb_ref, bias_ref, tri_ref, o_ref, sc_ref, s_ref,
                st_ref, *, topk):
    qblk = pl.program_id(1)
    nt = qblk + 1
    tq = qi_ref.shape[0]
    lo = lax.broadcasted_iota(jnp.int32, (tq, LANES), 1) < HEAD_DIM
    zero = jnp.zeros((tq, LANES), BF16)

    def halves(qp):
        return jnp.where(lo, qp, zero), jnp.where(lo, zero, qp)

    wi = wi_ref[...]

    def index_tile(j, _):
        off = pl.multiple_of(j * TB, TB)
        kk = kk_ref[pl.ds(off, TB), :]
        acc = jnp.zeros((tq, TB), F32)
        for m in range(IDX_HEADS // 2):
            q_lo, q_hi = halves(qi_ref[:, m * LANES:(m + 1) * LANES])
            acc = acc + wi[:, 2 * m:2 * m + 1] * jnp.maximum(_dot_nt(q_lo, kk), 0.0)
            acc = acc + wi[:, 2 * m + 1:2 * m + 2] * jnp.maximum(_dot_nt(q_hi, kk), 0.0)
        row = lax.broadcasted_iota(jnp.int32, (tq, TB), 0) + qblk * TB
        col = lax.broadcasted_iota(jnp.int32, (tq, TB), 1) + j * TB
        sc_ref[j] = jnp.where(col <= row, acc, NEG_INF)
        return 0

    lax.fori_loop(0, nt, index_tile, 0)
    for n_tiles in range(1, sc_ref.shape[0] + 1):
        pl.when(nt == n_tiles)(functools.partial(_topk_select, sc_ref, n_tiles, topk, tri_ref))

    def near(m, j):
        d = _bias_index(qblk - j)
        return jnp.concatenate([bias_ref[2 * m, d], bias_ref[2 * m + 1, d]], axis=0)

    result = _two_pass_attend(qb_ref, kb_ref, vb_ref, nt, near, lambda jc: sc_ref[jc], s_ref, st_ref)
    lo_out = lax.broadcasted_iota(jnp.int32, (tq, LANES), 1) < HEAD_DIM
    for m in range(N_CHAINS):
        acc, l = result(m)
        o = acc / l
        o_ref[:, m * LANES:(m + 1) * LANES] = jnp.where(lo_out, o[:tq], o[tq:]).astype(BF16)


def _dsa_prompt(p, bias_b, tri, b, t, topk):
    nq = t // TB
    return pl.pallas_call(
        functools.partial(_dsa_kernel, topk=topk),
        grid=(b, nq),
        in_specs=[pl.BlockSpec((TB, I_WIDTH), lambda bi, i: (bi * nq + i, 0)),
                  pl.BlockSpec((t, LANES), lambda bi, i: (bi, 0)),
                  pl.BlockSpec((TB, LANES), lambda bi, i: (bi * nq + i, 0)),
                  pl.BlockSpec((TB, B_WIDTH), lambda bi, i: (bi * nq + i, 0)),
                  pl.BlockSpec((t, B_WIDTH), lambda bi, i: (bi, 0)),
                  pl.BlockSpec((t, B_WIDTH), lambda bi, i: (bi, 1)),
                  pl.BlockSpec((B_HEADS, N_BIAS_TILES, TB, TB), lambda bi, i: (0, 0, 0, 0),
                               pipeline_mode=pl.Buffered(1)),
                  pl.BlockSpec((TB, TB), lambda bi, i: (0, 0))],
        out_specs=pl.BlockSpec((TB, B_WIDTH), lambda bi, i: (bi * nq + i, 0)),
        out_shape=jax.ShapeDtypeStruct((b * t, B_WIDTH), BF16),
        scratch_shapes=[pltpu.VMEM((nq, TB, TB), F32)] + _attend_scratch(nq),
        compiler_params=_cparams(("arbitrary", "arbitrary")),
        name="dsa_prompt",
    )(p["qi"], p["kk"], p["wi"], p["qb"], p["kvb"], p["kvb"], bias_b, tri)


IDX_SEQS = 2


def _sample_idx_kernel(pt_ref, q_ref, w_ref, kn_ref, *rest, n_pages):
    page_refs, o_ref = rest[:IDX_SEQS * n_pages], rest[IDX_SEQS * n_pages]
    nq = q_ref.shape[1] // IDX_HEADS
    for u in range(IDX_SEQS):
        q = q_ref[u]
        w = w_ref[u]
        rows = slice(u * nq, (u + 1) * nq)

        def combine(qk, w=w):
            rel = jnp.maximum(qk, 0.0) * w
            sc = rel[0:nq]
            for h in range(1, IDX_HEADS):
                sc = sc + rel[h * nq:(h + 1) * nq]
            return sc

        for k in range(n_pages):
            o_ref[k, rows, :] = combine(_dot(q, page_refs[u * n_pages + k][0, 0].astype(BF16)))
        new = jnp.concatenate([kn_ref[u], jnp.zeros((PAGE_ROWS - nq, IDX_DIM), F32)], axis=0).astype(BF16)
        row = lax.broadcasted_iota(jnp.int32, (nq, PAGE_ROWS), 0)
        col = lax.broadcasted_iota(jnp.int32, (nq, PAGE_ROWS), 1)
        o_ref[n_pages, rows, :] = jnp.where(col <= row, combine(_dot_nt(q, new)), NEG_INF)


def _sample_idx(page_table, q_stack, w_stack, ki_new, cache_kidx_t, layer):
    s, n_pages = page_table.shape
    nq = ki_new.shape[1]
    hq = q_stack.shape[1]
    page_spec = lambda u, k: pl.BlockSpec((1, 1, IDX_DIM, PAGE_ROWS),
                                          lambda i, pt, u=u, k=k: (layer, pt[IDX_SEQS * i + u, k], 0, 0))
    grid_spec = pltpu.PrefetchScalarGridSpec(
        num_scalar_prefetch=1,
        grid=(s // IDX_SEQS,),
        in_specs=[pl.BlockSpec((IDX_SEQS, hq, IDX_DIM), lambda i, pt: (i, 0, 0)),
                  pl.BlockSpec((IDX_SEQS, hq, 1), lambda i, pt: (i, 0, 0)),
                  pl.BlockSpec((IDX_SEQS, nq, IDX_DIM), lambda i, pt: (i, 0, 0))]
                 + [page_spec(u, k) for u in range(IDX_SEQS) for k in range(n_pages)],
        out_specs=pl.BlockSpec((n_pages + 1, IDX_SEQS * nq, PAGE_ROWS), lambda i, pt: (0, i, 0)),
    )
    return pl.pallas_call(
        functools.partial(_sample_idx_kernel, n_pages=n_pages),
        grid_spec=grid_spec,
        out_shape=jax.ShapeDtypeStruct((n_pages + 1, s * nq, PAGE_ROWS), F32),
        compiler_params=_cparams(("arbitrary",)),
        name="sample_idx",
    )(page_table, q_stack, w_stack, ki_new, *([cache_kidx_t] * (IDX_SEQS * n_pages)))


def _select_kernel(sc_ref, tri_ref, o_ref, *, topk):
    o_ref[...] = sc_ref[...]
    _topk_select(o_ref, o_ref.shape[0], topk, tri_ref)


def _sample_select(scores, tri, topk):
    nt, rows, tw = scores.shape
    tr = min(rows, 256)
    return pl.pallas_call(
        functools.partial(_select_kernel, topk=topk),
        grid=(rows // tr,),
        in_specs=[pl.BlockSpec((nt, tr, tw), lambda i: (0, i, 0)),
                  pl.BlockSpec((tw, tw), lambda i: (0, 0))],
        out_specs=pl.BlockSpec((nt, tr, tw), lambda i: (0, i, 0)),
        out_shape=jax.ShapeDtypeStruct(scores.shape, F32),
        compiler_params=_cparams(("arbitrary",)),
        name="sample_select",
    )(scores, tri)


def _sample_attn_kernel(pt_ref, lam_ref, g_ref, qa_ref, qb_ref, mska_ref, mskb_ref, biasa_ref, biasb_ref,
                        mext_ref, bext_ref, kvan_ref, kvbn_ref, selp_ref, seln_ref, *rest, n_chunks, lam_init):
    g_pages = PAGES_PER_STEP
    kva_refs, kvb_refs = rest[:g_pages], rest[g_pages:2 * g_pages]
    oa_ref, ob_ref = rest[2 * g_pages], rest[2 * g_pages + 1]
    qa_s, qn_s, qb_s, ma_s, la_s, acca_s, mb_s, lb_s, accb_s = rest[2 * g_pages + 2:]
    c = pl.program_id(1)
    nq = qa_ref.shape[1]
    rows_a = A_HEADS * 2 * nq
    rows_b = B_HEADS * nq

    @pl.when(c == 0)
    def _():
        qa_s[...] = (jnp.concatenate([qa_ref[0]] * (2 * A_HEADS), axis=0) * mska_ref[...]).astype(BF16)
        qn_s[...] = jnp.concatenate([_masked_pair(qa_ref[0, :, h * LANES:(h + 1) * LANES])
                                     for h in range(A_HEADS)], axis=0).astype(BF16)
        qb_s[...] = (jnp.concatenate([qb_ref[0]] * B_HEADS, axis=0) * mskb_ref[...]).astype(BF16)
        ma_s[...] = jnp.full(ma_s.shape, NEG_INF, F32)
        mb_s[...] = jnp.full(mb_s.shape, NEG_INF, F32)
        la_s[...] = jnp.zeros(la_s.shape, F32)
        lb_s[...] = jnp.zeros(lb_s.shape, F32)
        acca_s[...] = jnp.zeros(acca_s.shape, F32)
        accb_s[...] = jnp.zeros(accb_s.shape, F32)

    def diag_a(r):
        return jnp.concatenate(
            [r[h * 2 * nq:(h + 1) * 2 * nq, h * LANES:(h + 1) * LANES] for h in range(A_HEADS)], axis=0)

    def diag_b(r):
        return jnp.concatenate(
            [r[m * 2 * nq:(m + 1) * 2 * nq, m * LANES:(m + 1) * LANES] for m in range(B_HEADS // 2)], axis=0)

    def update_a(s, values):
        mn = jnp.maximum(ma_s[...], jnp.max(s, axis=-1, keepdims=True))
        p = jnp.exp(s - mn)
        al = jnp.exp(ma_s[...] - mn)
        la_s[...] = al * la_s[...] + jnp.sum(p, axis=-1, keepdims=True)
        acca_s[...] = al * acca_s[...] + values(p.astype(BF16))
        ma_s[...] = mn

    def update_b(s, values):
        mn = jnp.maximum(mb_s[...], jnp.max(s, axis=-1, keepdims=True))
        ms = jnp.where(mn == NEG_INF, 0.0, mn)
        p = jnp.exp(s - ms)
        al = jnp.exp(mb_s[...] - ms)
        lb_s[...] = al * lb_s[...] + jnp.sum(p, axis=-1, keepdims=True)
        accb_s[...] = al * accb_s[...] + diag_b(values(p.astype(BF16)))
        mb_s[...] = mn

    def tile_rows(x, n):
        return jnp.concatenate([x] * n, axis=0)

    last = c == n_chunks - 1
    far = (g_pages - 1) * PAGE_ROWS
    bias_b = jnp.concatenate([jnp.zeros((rows_b, far), F32), jnp.where(last, biasb_ref[:, :PAGE_ROWS], 0.0)], axis=1)

    ext = A_HEADS * PAGE_ROWS
    page_keys = lambda ref, kv: ref[0, 0, :, kv].reshape(ext, 2 * HEAD_DIM).astype(BF16)
    qn = qn_s[...]
    s_pages = [_dot_nt(qn, page_keys(ref, 0)) for ref in kva_refs]
    s_pages = ([sp + mext_ref[...] for sp in s_pages[:-1]]
               + [s_pages[-1] + jnp.where(last, bext_ref[...], mext_ref[...])])

    def page_values(p):
        out = None
        for g, ref in enumerate(kva_refs):
            part = _dot(p[:, g * ext:(g + 1) * ext], page_keys(ref, 1))
            out = part if out is None else out + part
        return out

    update_a(jnp.concatenate(s_pages, axis=1), page_values)

    kt = jnp.concatenate([ref[0, 0, 0].reshape(B_WIDTH, PAGE_ROWS) for ref in kvb_refs], axis=1).astype(BF16)
    vt = jnp.concatenate([ref[0, 0, 1].reshape(B_WIDTH, PAGE_ROWS) for ref in kvb_refs], axis=1).astype(BF16)
    sel = jnp.concatenate([tile_rows(selp_ref[g], B_HEADS) for g in range(g_pages)], axis=1)
    update_b(_dot(qb_s[...], kt) + bias_b + sel, lambda p: _dot_nt(p, vt))

    @pl.when(last)
    def _():
        pad = lambda x: jnp.concatenate([x, jnp.zeros((PAGE_ROWS - nq, x.shape[1]), F32)], axis=0).astype(BF16)
        kn, vn = pad(kvan_ref[0, :, :A_WIDTH]), pad(kvan_ref[0, :, A_WIDTH:])
        update_a(_dot_nt(qa_s[...], kn) + biasa_ref[...], lambda p: diag_a(_dot(p, vn)))
        kn, vn = pad(kvbn_ref[0, :, :B_WIDTH]), pad(kvbn_ref[0, :, B_WIDTH:])
        update_b(_dot_nt(qb_s[...], kn) + biasb_ref[:, PAGE_ROWS:] + tile_rows(seln_ref[0], B_HEADS),
                 lambda p: _dot(p, vn))

        lam = _lambda_value(lam_ref, lam_init)
        oa = acca_s[...] / la_s[...]
        for h in range(A_HEADS):
            o = oa[h * 2 * nq:h * 2 * nq + nq] - lam * oa[h * 2 * nq + nq:(h + 1) * 2 * nq]
            oa_ref[0, :, h * LANES:(h + 1) * LANES] = _sub_layer_norm(o, g_ref[...], lam_init)
        ob = accb_s[...] / lb_s[...]
        lo = lax.broadcasted_iota(jnp.int32, (nq, LANES), 1) < HEAD_DIM
        for m in range(B_HEADS // 2):
            ob_ref[0, :, m * LANES:(m + 1) * LANES] = jnp.where(
                lo, ob[2 * m * nq:(2 * m + 1) * nq], ob[(2 * m + 1) * nq:(2 * m + 2) * nq])


def _sample_attn(page_table, lam4, g_sub, qa, qb, mask_a, mask_b, bias_a, bias_b, kva_new, kvb_new, sel,
                 cache_a, cache_bt, layer, lam_init):
    s, n_pages = page_table.shape
    nq = qa.shape[1]
    g_pages = PAGES_PER_STEP
    n_chunks = n_pages // g_pages
    rows_a, rows_b = A_HEADS * 2 * nq, B_HEADS * nq
    ext = A_HEADS * PAGE_ROWS
    key_head = jnp.arange(ext)[None, :] % A_HEADS
    row_head = jnp.arange(rows_a)[:, None] // (2 * nq)
    mask_ext = jnp.where(key_head == row_head, 0.0, NEG_INF).astype(F32)
    bias_ext = jnp.repeat(bias_a[:, :PAGE_ROWS], A_HEADS, axis=1) + mask_ext
    const = lambda shape: pl.BlockSpec(shape, lambda i, c, pt: (0,) * len(shape))
    seq = lambda shape: pl.BlockSpec(shape, lambda i, c, pt: (i,) + (0,) * (len(shape) - 1))
    page_a = lambda g: pl.BlockSpec((1, 1) + cache_a.shape[2:],
                                    lambda i, c, pt, g=g: (layer, pt[i, c * g_pages + g], 0, 0, 0, 0))
    page_b = lambda g: pl.BlockSpec((1, 1) + cache_bt.shape[2:],
                                    lambda i, c, pt, g=g: (layer, pt[i, c * g_pages + g], 0, 0, 0, 0))
    grid_spec = pltpu.PrefetchScalarGridSpec(
        num_scalar_prefetch=1,
        grid=(s, n_chunks),
        in_specs=[const((4, HEAD_DIM)), const((1, 2 * HEAD_DIM)),
                  seq((1, nq, A_WIDTH)), seq((1, nq, B_WIDTH)),
                  const((rows_a, A_WIDTH)), const((rows_b, B_WIDTH)),
                  const((rows_a, PAGE_ROWS)), const((rows_b, 2 * PAGE_ROWS)),
                  const((rows_a, ext)), const((rows_a, ext)),
                  seq((1, nq, 2 * A_WIDTH)), seq((1, nq, 2 * B_WIDTH)),
                  pl.BlockSpec((g_pages, nq, PAGE_ROWS), lambda i, c, pt: (c, i, 0)),
                  pl.BlockSpec((1, nq, PAGE_ROWS), lambda i, c, pt: (n_pages, i, 0))]
                 + [page_a(g) for g in range(g_pages)]
                 + [page_b(g) for g in range(g_pages)],
        out_specs=[seq((1, nq, A_WIDTH)), seq((1, nq, B_WIDTH))],
        scratch_shapes=[pltpu.VMEM((rows_a, A_WIDTH), BF16), pltpu.VMEM((rows_a, 2 * HEAD_DIM), BF16),
                        pltpu.VMEM((rows_b, B_WIDTH), BF16),
                        pltpu.VMEM((rows_a, 1), F32), pltpu.VMEM((rows_a, 1), F32),
                        pltpu.VMEM((rows_a, LANES), F32),
                        pltpu.VMEM((rows_b, 1), F32), pltpu.VMEM((rows_b, 1), F32),
                        pltpu.VMEM((rows_b, LANES), F32)],
    )
    return pl.pallas_call(
        functools.partial(_sample_attn_kernel, n_chunks=n_chunks, lam_init=lam_init),
        grid_spec=grid_spec,
        out_shape=[jax.ShapeDtypeStruct((s, nq, A_WIDTH), F32), jax.ShapeDtypeStruct((s, nq, B_WIDTH), F32)],
        compiler_params=_cparams(("arbitrary", "arbitrary")),
        name="sample_attn",
    )(page_table, lam4, g_sub, qa, qb, mask_a, mask_b, bias_a[:, PAGE_ROWS:], bias_b, mask_ext, bias_ext,
      kva_new, kvb_new, sel, sel, *([cache_a] * g_pages), *([cache_bt] * g_pages))


def _outproj_kernel(oa_ref, ob_ref, sg_ref, x_ref, g1_ref, sh2_ref, sc2_ref, wba_ref, wbb_ref, wout_ref,
                    lng_ref, lnb_ref, wr_ref, br_ref, x1_ref, hc_ref, *, alpha):
    _outproj_rows(slice(0, x_ref.shape[0]), oa_ref, ob_ref, sg_ref, x_ref, g1_ref, sh2_ref, sc2_ref, wba_ref,
                  wbb_ref, wout_ref, lng_ref, lnb_ref, wr_ref, br_ref, x1_ref, hc_ref, alpha)


def _outproj_rows(rows, oa_ref, ob_ref, sg_ref, x_ref, g1_ref, sh2_ref, sc2_ref, wba_ref, wbb_ref, wout_ref,
                  lng_ref, lnb_ref, wr_ref, br_ref, x1_ref, hc_ref, alpha):
    per_row = lambda ref: ref[0] if ref.shape[1] == 1 else ref[0, rows, :]
    ya = _dot(oa_ref[rows, :], wba_ref[...])
    yb = _dot(ob_ref[rows, :], wbb_ref[...])
    t = sg_ref[rows, :D_MODEL] * ya + sg_ref[rows, D_MODEL:] * yb
    mix = _dot(t.astype(BF16), wout_ref[...])
    x1 = _layer_norm(alpha * x_ref[rows, :] + per_row(g1_ref) * mix, lng_ref[...], lnb_ref[...])
    x1_ref[rows, :] = x1
    h2 = x1 * (1.0 + per_row(sc2_ref)) + per_row(sh2_ref)
    hc_ref[rows, :D_MODEL] = h2

    logits = _dot3(h2, wr_ref[...]) + br_ref[...]
    lane = lax.broadcasted_iota(jnp.int32, logits.shape, 1).astype(F32)
    big = float(LANES)
    is_group = jnp.logical_and(lane >= N_EXPERTS, lane < N_EXPERTS + N_GROUPS)
    lg = jnp.where(is_group, logits, NEG_INF)
    mg = jnp.max(lg, axis=-1, keepdims=True)
    g_sel = jnp.min(jnp.where(lg == mg, lane, big), axis=-1, keepdims=True) - N_EXPERTS
    p_g = 1.0 / jnp.sum(jnp.exp(lg - mg), axis=-1, keepdims=True)
    first = g_sel * EXPERTS_PER_GROUP
    in_group = jnp.logical_and(lane >= first, lane < first + EXPERTS_PER_GROUP)
    le = jnp.where(in_group, logits, NEG_INF)
    ex = jnp.exp(le - jnp.max(le, axis=-1, keepdims=True))
    pe = jnp.where(in_group, ex / jnp.sum(ex, axis=-1, keepdims=True), -1.0)
    v1 = jnp.max(pe, axis=-1, keepdims=True)
    i1 = jnp.min(jnp.where(pe == v1, lane, big), axis=-1, keepdims=True)
    pe2 = jnp.where(lane == i1, -1.0, pe)
    v2 = jnp.max(pe2, axis=-1, keepdims=True)
    i2 = jnp.min(jnp.where(pe2 == v2, lane, big), axis=-1, keepdims=True)
    tot = v1 + v2
    hc_ref[rows, D_MODEL:] = (jnp.where(lane == i1, p_g * (v1 / tot), 0.0)
                              + jnp.where(lane == i2, p_g * (v2 / tot), 0.0)
                              + jnp.where(lane == GROUP_LANE, g_sel, 0.0))


def _out_proj(oa, ob, sg, x, ada3, wba, wbb, wout, lng, lnb, wr, br, rows_per_batch, alpha):
    n = x.shape[0]
    tm = ROW_TILE
    row = lambda w: pl.BlockSpec((tm, w), lambda i: (i, 0))
    full = lambda a: pl.BlockSpec(a.shape, lambda i: (0,) * a.ndim)
    return pl.pallas_call(
        functools.partial(_outproj_kernel, alpha=alpha),
        grid=(n // tm,),
        in_specs=[row(A_WIDTH), row(B_WIDTH), row(2 * D_MODEL), row(D_MODEL),
                  _row_vec_spec(ada3, 2, tm, rows_per_batch),
                  _row_vec_spec(ada3, 3, tm, rows_per_batch),
                  _row_vec_spec(ada3, 4, tm, rows_per_batch),
                  full(wba), full(wbb), full(wout), full(lng), full(lnb), full(wr), full(br)],
        out_specs=[row(D_MODEL), row(HC_WIDTH)],
        out_shape=[jax.ShapeDtypeStruct((n, D_MODEL), F32), jax.ShapeDtypeStruct((n, HC_WIDTH), F32)],
        compiler_params=_cparams(("arbitrary",)),
        name="out_proj",
    )(oa, ob, sg, x, ada3, ada3, ada3, wba, wbb, wout, lng, lnb, wr, br)


def _moe_route_kernel(hc_ref, tri_ref, upper_ref, slot_ref, tile_ref, bounds_ref, cnt_ref, off_ref):
    phase, i = pl.program_id(0), pl.program_id(1)
    route = hc_ref[...]
    lane = lax.broadcasted_iota(jnp.int32, route.shape, 1)
    group = jnp.sum(jnp.where(lane == GROUP_LANE, route, 0.0), axis=-1, keepdims=True)
    onehot = jnp.where(lane.astype(F32) == group, 1.0, 0.0)
    col_sum = jnp.sum(onehot, axis=0, keepdims=True)

    @pl.when(jnp.logical_and(phase == 0, i == 0))
    def _():
        cnt_ref[...] = jnp.zeros(cnt_ref.shape, F32)

    @pl.when(phase == 0)
    def _():
        cnt_ref[0:1, :] += col_sum
        slot_ref[...] = jnp.zeros(slot_ref.shape, jnp.int32)

    @pl.when(jnp.logical_and(phase == 1, i == 0))
    def _():
        padded = jnp.ceil(cnt_ref[...] / MOE_ROWS) * MOE_ROWS
        start = _dot3(padded, upper_ref[...])
        off_ref[0] = start
        off_ref[1] = start + padded
        cnt_ref[...] = jnp.zeros(cnt_ref.shape, F32)
        sub = lax.broadcasted_iota(jnp.int32, cnt_ref.shape, 0)
        lane8 = lax.broadcasted_iota(jnp.int32, cnt_ref.shape, 1)
        lane1 = lax.broadcasted_iota(jnp.int32, (1, LANES), 1)
        tile_start = ((sub * LANES + lane8) * MOE_ROWS).astype(F32)
        ends = off_ref[1, 0:1, :]
        tile_group = jnp.zeros(cnt_ref.shape, F32)
        for g in range(N_GROUPS):
            end_g = jnp.sum(jnp.where(lane1 == g, ends, 0.0), axis=-1, keepdims=True)
            tile_group = tile_group + jnp.where(tile_start >= end_g, 1.0, 0.0)
        tile_ref[...] = tile_group.astype(jnp.int32)
        first = jnp.sum(start, axis=0, keepdims=True) / MOE_ROWS
        past = jnp.sum(start + padded, axis=0, keepdims=True) / MOE_ROWS
        bounds_ref[...] = jnp.where(sub == 0, first, jnp.where(sub == 1, past, 0.0)).astype(jnp.int32)

    @pl.when(phase == 1)
    def _():
        earlier = _dot(tri_ref[...], onehot.astype(BF16))
        base = off_ref[0, 0:1, :] + cnt_ref[0:1, :]
        slot = jnp.sum(onehot * (base + earlier), axis=-1, keepdims=True)
        slot_ref[...] = jnp.broadcast_to(slot, route.shape).astype(jnp.int32)
        cnt_ref[0:1, :] += col_sum


def _moe_route(hc):
    n = hc.shape[0]
    tm = min(ROUTE_ROWS, n)
    tri = (jnp.arange(tm)[:, None] > jnp.arange(tm)[None, :]).astype(BF16)
    upper = (jnp.arange(LANES)[:, None] < jnp.arange(LANES)[None, :]).astype(F32)
    return pl.pallas_call(
        _moe_route_kernel,
        grid=(2, n // tm),
        in_specs=[pl.BlockSpec((tm, LANES), lambda ph, i: (i, D_MODEL // LANES)),
                  pl.BlockSpec((tm, tm), lambda ph, i: (0, 0)),
                  pl.BlockSpec((LANES, LANES), lambda ph, i: (0, 0))],
        out_specs=[pl.BlockSpec((tm, LANES), lambda ph, i: (i * ph, 0)),
                   pl.BlockSpec((8, LANES), lambda ph, i: (0, 0)),
                   pl.BlockSpec((8, LANES), lambda ph, i: (0, 0))],
        out_shape=[jax.ShapeDtypeStruct((n, LANES), jnp.int32), jax.ShapeDtypeStruct((8, LANES), jnp.int32),
                   jax.ShapeDtypeStruct((8, LANES), jnp.int32)],
        scratch_shapes=[pltpu.VMEM((8, LANES), F32), pltpu.VMEM((2, 8, LANES), F32)],
        compiler_params=_cparams(("arbitrary", "arbitrary")),
        name="moe_route",
    )(hc, tri, upper)


def _row_copies(n_rows, copy):
    def start_pair(i, c):
        copy(2 * i).start(priority=0)
        copy(2 * i + 1).start(priority=1)
        return c
    lax.fori_loop(0, n_rows // 2, start_pair, 0, unroll=4)
    lax.fori_loop(0, n_rows, lambda t, c: (copy(0).wait(), c)[1], 0, unroll=8)


def _moe_dispatch_kernel(slot_ref, bounds_ref, hc_ref, out_ref, zero_ref, sem):
    @pl.when(pl.program_id(0) == 0)
    def _():
        zero_ref[...] = jnp.zeros(zero_ref.shape, F32)

        def clear_tile(i):
            tile = pl.ds(pl.multiple_of(i * MOE_ROWS, MOE_ROWS), MOE_ROWS)
            clear = pltpu.make_async_copy(zero_ref, out_ref.at[tile], sem)
            clear.start()
            clear.wait()

        for g in range(N_GROUPS):
            first, past = bounds_ref[g], bounds_ref[N_GROUPS + g]
            pl.when(past > first)(functools.partial(clear_tile, past - 1))
        used_tiles = bounds_ref[2 * N_GROUPS - 1]
        lax.fori_loop(used_tiles, out_ref.shape[0] // MOE_ROWS, lambda i, c: (clear_tile(i), c)[1], 0)

    copy = lambda t: pltpu.make_async_copy(hc_ref.at[pl.ds(t, 1)], out_ref.at[pl.ds(slot_ref[t], 1)], sem)
    _row_copies(hc_ref.shape[0], copy)


def _moe_dispatch(slot, bounds, hc, n_rows):
    n = hc.shape[0]
    tm = MOE_ROWS
    return pl.pallas_call(
        _moe_dispatch_kernel,
        grid=(n // tm,),
        in_specs=[pl.BlockSpec((tm,), lambda i: (i,), memory_space=pltpu.SMEM),
                  pl.BlockSpec(memory_space=pltpu.SMEM),
                  pl.BlockSpec((tm, HC_WIDTH), lambda i: (i, 0))],
        out_specs=pl.BlockSpec(memory_space=pl.ANY),
        out_shape=jax.ShapeDtypeStruct((n_rows, HC_WIDTH), F32),
        scratch_shapes=[pltpu.VMEM((MOE_ROWS, HC_WIDTH), F32), pltpu.SemaphoreType.DMA(())],
        compiler_params=_cparams(("arbitrary",)),
        name="moe_dispatch",
    )(slot, bounds, hc)


def _moe_kernel(tile_ref, hc_ref, wup_ref, wdn_ref, o_ref):
    group = tile_ref[pl.program_id(0)]

    @pl.when(group < N_GROUPS)
    def _():
        h = hc_ref[:, :D_MODEL].astype(BF16)
        comb = hc_ref[:, D_MODEL:]
        lane = lax.broadcasted_iota(jnp.int32, comb.shape, 1)
        acc = jnp.zeros(o_ref.shape, F32)
        for e in range(EXPERTS_PER_GROUP):
            hid = _dot(h, wup_ref[e])
            gate, up = hid[:, :EXPERT_HIDDEN], hid[:, EXPERT_HIDDEN:]
            act = (gate * _sigmoid(gate) * up).astype(BF16)
            w = jnp.sum(jnp.where(lane == group * EXPERTS_PER_GROUP + e, comb, 0.0), axis=-1, keepdims=True)
            acc = acc + w * _dot(act, wdn_ref[e])
        o_ref[...] = acc

    @pl.when(group >= N_GROUPS)
    def _():
        o_ref[...] = jnp.zeros(o_ref.shape, F32)


def _moe_experts(tile_group, hc_sorted, wup, wdn):
    n_rows = hc_sorted.shape[0]
    tm = MOE_ROWS
    group = lambda i, tg: (jnp.minimum(tg[i], N_GROUPS - 1), 0, 0)
    grid_spec = pltpu.PrefetchScalarGridSpec(
        num_scalar_prefetch=1,
        grid=(n_rows // tm,),
        in_specs=[pl.BlockSpec((tm, HC_WIDTH), lambda i, tg: (i, 0)),
                  pl.BlockSpec((EXPERTS_PER_GROUP, D_MODEL, 2 * EXPERT_HIDDEN), group),
                  pl.BlockSpec((EXPERTS_PER_GROUP, EXPERT_HIDDEN, D_MODEL), group)],
        out_specs=pl.BlockSpec((tm, D_MODEL), lambda i, tg: (i, 0)),
    )
    return pl.pallas_call(
        _moe_kernel,
        grid_spec=grid_spec,
        out_shape=jax.ShapeDtypeStruct((n_rows, D_MODEL), F32),
        compiler_params=_cparams(("arbitrary",)),
        name="moe_experts",
    )(tile_group, hc_sorted, wup, wdn)


def _moe_combine_kernel(slot_ref, y_ref, x1_ref, g2_ref, lng_ref, lnb_ref, o_ref, buf_ref, sem, *, alpha):
    copy = lambda t: pltpu.make_async_copy(y_ref.at[pl.ds(slot_ref[t], 1)], buf_ref.at[pl.ds(t, 1)], sem)
    _row_copies(buf_ref.shape[0], copy)
    u = alpha * x1_ref[...] + g2_ref[0] * buf_ref[...]
    o_ref[...] = _layer_norm(u, lng_ref[...], lnb_ref[...])


def _moe_combine(slot, y_sorted, x1, ada3, lng, lnb, rows_per_batch, alpha):
    n = x1.shape[0]
    tm = MOE_ROWS
    full = lambda a: pl.BlockSpec(a.shape, lambda i: (0,) * a.ndim)
    return pl.pallas_call(
        functools.partial(_moe_combine_kernel, alpha=alpha),
        grid=(n // tm,),
        in_specs=[pl.BlockSpec((tm,), lambda i: (i,), memory_space=pltpu.SMEM),
                  pl.BlockSpec(memory_space=pl.ANY),
                  pl.BlockSpec((tm, D_MODEL), lambda i: (i, 0)),
                  _row_vec_spec(ada3, 5, tm, rows_per_batch),
                  full(lng), full(lnb)],
        out_specs=pl.BlockSpec((tm, D_MODEL), lambda i: (i, 0)),
        out_shape=jax.ShapeDtypeStruct((n, D_MODEL), F32),
        scratch_shapes=[pltpu.VMEM((tm, D_MODEL), F32), pltpu.SemaphoreType.DMA(())],
        compiler_params=_cparams(("arbitrary",)),
        name="moe_combine",
    )(slot, y_sorted, x1, ada3, lng, lnb)


def _moe(hc, x1, ada3, wup, wdn, lng, lnb, rows_per_batch, alpha):
    n = x1.shape[0]
    n_rows = n + N_GROUPS * MOE_ROWS
    slot_lanes, tile_table, bounds_table = _moe_route(hc)
    bounds = bounds_table[:2, :N_GROUPS].reshape(-1)
    slot = slot_lanes[:, 0]
    tile_group = tile_table.reshape(-1)[:n_rows // MOE_ROWS]
    hc_sorted = _moe_dispatch(slot, bounds, hc, n_rows)
    y_sorted = _moe_experts(tile_group, hc_sorted, wup, wdn)
    return _moe_combine(slot, y_sorted, x1, ada3, lng, lnb, rows_per_batch, alpha)


def _prompt_dist():
    i = jnp.arange(TB, dtype=jnp.int32)[:, None]
    j = jnp.arange(TB, dtype=jnp.int32)[None, :]
    return jnp.concatenate([d * TB + i - j for d in range(N_BIAS_TILES)], axis=0)


def _sample_dist(nq, past_len):
    i = jnp.arange(nq, dtype=jnp.int32)[:, None]
    j = jnp.arange(PAGE_ROWS, dtype=jnp.int32)[None, :]
    last_page = past_len + i - (past_len - PAGE_ROWS + j)
    new = jnp.where(j < nq, i - j, -1)
    return jnp.concatenate([last_page, new], axis=1)


def kernel(x_prompt, x_sample, cache_kv_diff, cache_kv_dsa, cache_kidx, page_table, c_prompt, c_sample,
           rel_bias, w_ada, b_ada, w_in, lambda_q1, lambda_k1, lambda_q2, lambda_k2, subln_g, w_branch_a,
           w_branch_b, w_out, ln1_g, ln1_b, w_router_group, b_router_group, w_router_expert,
           b_router_expert, w_up, w_down, ln2_g, ln2_b):
    b, t, d = x_prompt.shape
    s, nq, _ = x_sample.shape
    depth = w_in.shape[0]
    n_pool = cache_kidx.shape[1]
    cache_a = cache_kv_diff
    cache_bt = jnp.transpose(cache_kv_dsa, (0, 1, 3, 4, 5, 2))
    cache_it = jnp.transpose(cache_kidx, (0, 1, 3, 2))
    n_pages = page_table.shape[1]
    past_len = n_pages * PAGE_ROWS
    alpha = (2 * depth) ** 0.25
    topk_p = min(DSA_TOPK_MAX, t // 4)
    topk_s = min(DSA_TOPK_MAX, (past_len + nq) // 4)
    assert d == D_MODEL and t % TB == 0 and (s * nq) % ROW_TILE == 0 and n_pages % PAGES_PER_STEP == 0
    assert MOE_ROWS == ROW_TILE == TB and B_HEADS == 2 * N_CHAINS and A_HEADS % N_CHAINS == 0
    assert cache_kidx.shape[2] == PAGE_ROWS and nq <= 8 and s % IDX_SEQS == 0

    bias_p = _bias_tiles(rel_bias, _prompt_dist()).reshape(A_HEADS + B_HEADS, N_BIAS_TILES, TB, TB)
    bias_s = _bias_tiles(rel_bias, _sample_dist(nq, past_len))
    bias_sa = jnp.broadcast_to(bias_s[:A_HEADS, None], (A_HEADS, 2, nq, 2 * PAGE_ROWS)).reshape(
        A_HEADS * 2 * nq, 2 * PAGE_ROWS)
    bias_sb = bias_s[A_HEADS:].reshape(B_HEADS * nq, 2 * PAGE_ROWS)
    tri_p = (jnp.arange(TB)[:, None] <= jnp.arange(TB)[None, :]).astype(BF16)
    tri_s = tri_p[:PAGE_ROWS, :PAGE_ROWS]
    lane_a = jnp.arange(A_WIDTH)[None, :] // HEAD_DIM
    mask_a = (lane_a == (jnp.arange(A_HEADS * 2 * nq)[:, None] // nq)).astype(F32)
    lane_b = jnp.arange(B_WIDTH)[None, :] // HEAD_DIM
    mask_b = (lane_b == (jnp.arange(B_HEADS * nq)[:, None] // nq)).astype(F32)

    xp = x_prompt.reshape(b * t, d)
    xs = x_sample.reshape(s * nq, d)
    c_all = jnp.concatenate([c_prompt, c_sample], axis=0)
    c_all = jnp.pad(c_all, ((0, -(b + s) % 8), (0, 0)))
    outs = [[] for _ in range(6)]
    for l in range(depth):
        lam_init = 0.8 - 0.6 * math.exp(-0.3 * l)
        lam4 = jnp.stack([lambda_q1[l], lambda_k1[l], lambda_q2[l], lambda_k2[l]]).astype(F32)
        g_sub = subln_g[l].reshape(1, 2 * HEAD_DIM)
        w = w_in[l]
        w2 = jnp.concatenate([w[:, :C_KI + IDX_DIM], w[:, C_KI:C_KI + IDX_DIM],
                              w[:, C_KI + IDX_DIM:C_KI + IDX_DIM + IDX_HEADS],
                              jnp.zeros((d, LANES - IDX_HEADS), w.dtype),
                              w[:, C_KI + IDX_DIM + IDX_HEADS:]], axis=1).astype(BF16)
        wba, wbb, wout = w_branch_a[l].astype(BF16), w_branch_b[l].astype(BF16), w_out[l].astype(BF16)
        wup, wdn = w_up[l].astype(BF16), w_down[l].astype(BF16)
        wr = jnp.concatenate([w_router_expert[l], w_router_group[l],
                              jnp.zeros((d, LANES - N_EXPERTS - N_GROUPS), F32)], axis=1)
        br = jnp.concatenate([b_router_expert[l], b_router_group[l],
                              jnp.zeros((LANES - N_EXPERTS - N_GROUPS,), F32)]).reshape(1, LANES)
        ln1 = (ln1_g[l].reshape(1, d), ln1_b[l].reshape(1, d))
        ln2 = (ln2_g[l].reshape(1, d), ln2_b[l].reshape(1, d))

        ada = _ada(c_all, w_ada[l], b_ada[l])
        ada_p = ada[:b].reshape(b, 1, 6 * d)
        ada_s = jnp.broadcast_to(ada[b:b + s, None], (s, nq, 6 * d)).reshape(s * nq // ROW_TILE, ROW_TILE, 6 * d)

        p = _in_proj(xp, ada_p, w2, t, True)
        oa = _diff_attn_prompt(lam4, g_sub, p["qa"], p["kva"], bias_p[:A_HEADS], b, t, lam_init)
        ob = _dsa_prompt(p, bias_p[A_HEADS:], tri_p, b, t, topk_p)
        x1, hc = _out_proj(oa, ob, p["sg"], xp, ada_p, wba, wbb, wout, *ln1, wr, br, t, alpha)
        xp = _moe(hc, x1, ada_p, wup, wdn, *ln2, t, alpha)
        outs[0].append(p["ra"].reshape(b, t, 2, A_HEADS, 2 * HEAD_DIM))
        outs[1].append(p["rb"].reshape(b, 2, B_HEADS, HEAD_DIM, t).transpose(0, 4, 1, 2, 3))
        outs[2].append(p["ki"].reshape(b, t, IDX_DIM))

        q = _in_proj(xs, ada_s, w2, nq, False)
        q_stack = q["qi"].reshape(s, nq, IDX_HEADS, IDX_DIM).transpose(0, 2, 1, 3).reshape(
            s, IDX_HEADS * nq, IDX_DIM)
        w_stack = q["wi"][:, :IDX_HEADS].reshape(s, nq, IDX_HEADS).transpose(0, 2, 1).reshape(
            s, IDX_HEADS * nq, 1)
        scores = _sample_idx(page_table, q_stack, w_stack, q["ki"].reshape(s, nq, IDX_DIM), cache_it, l)
        sel = _sample_select(scores, tri_s, topk_s)
        oa_s, ob_s = _sample_attn(
            page_table, lam4, g_sub,
            q["qa"].astype(F32).reshape(s, nq, A_WIDTH), q["qb"].astype(F32).reshape(s, nq, B_WIDTH),
            mask_a, mask_b, bias_sa, bias_sb,
            q["ra"].reshape(s, nq, 2 * A_WIDTH), q["rb"].reshape(s, nq, 2 * B_WIDTH), sel,
            cache_a, cache_bt, l, lam_init)
        x1, hc = _out_proj(oa_s.reshape(s * nq, A_WIDTH).astype(BF16),
                                 ob_s.reshape(s * nq, B_WIDTH).astype(BF16),
                                 q["sg"], xs, ada_s, wba, wbb, wout, *ln1, wr, br, nq, alpha)
        xs = _moe(hc, x1, ada_s, wup, wdn, *ln2, nq, alpha)
        outs[3].append(q["ra"].reshape(s, nq, 2, A_HEADS, 2 * HEAD_DIM))
        outs[4].append(q["rb"].reshape(s, nq, 2, B_HEADS, HEAD_DIM))
        outs[5].append(q["ki"].reshape(s, nq, IDX_DIM))

    return (xp.reshape(b, t, d), xs.reshape(s, nq, d)) + tuple(jnp.stack(o, 0) for o in outs)
```
